```python
import jax, jax.numpy as jnp
from jax import lax
import numpy as np

D_MODEL = 1024
BATCH = 8
SEQ = 4096
DEPTH = 4

N_A_LAYERS = DEPTH // 2
N_B_LAYERS = DEPTH - N_A_LAYERS
MAIN_W = 3 * D_MODEL // 4
MEM_W = D_MODEL // 4
MIX_W = MAIN_W + MEM_W
GLA_HEADS = 4
GLA_DK = MAIN_W // 2
GLA_DV = MAIN_W
GLA_DK_HEAD = GLA_DK // GLA_HEADS
GLA_DV_HEAD = GLA_DV // GLA_HEADS
GLA_GATE_RANK = 16
GLA_GATE_TEMP = 16.0
GLA_CHUNK = 64
FOX_HEADS = 6
FOX_HEAD_DIM = MAIN_W // FOX_HEADS
FOX_BLOCK_Q = 128
FOX_FGATE_BIAS_INIT = 3.0
MEM_HEADS = 4
MEM_HEAD_DIM = MEM_W // MEM_HEADS
MEM_LEN = 256
D_FF = 256 * ((8 * D_MODEL // 3 + 255) // 256)
EPS = 1e-6
A_IN_W = 2 * GLA_DK + 2 * GLA_DV + GLA_GATE_RANK + MEM_W
B_IN_W = MAIN_W + MEM_W

kernel_name = "yoco_gla_fox_macaron_memory"


def rmsnorm(x, g):
    xf = x.astype(jnp.float32)
    y = xf * lax.rsqrt(jnp.mean(xf * xf, axis=-1, keepdims=True) + EPS)
    return (y * g.astype(jnp.float32)).astype(x.dtype)


def split_heads(t, n):
    b, s, w = t.shape
    return t.reshape(b, s, n, w // n).transpose(0, 2, 1, 3)


def merge_heads(t):
    b, n, s, d = t.shape
    return t.transpose(0, 2, 1, 3).reshape(b, s, n * d)


def swiglu_half_step(h, g, w_in, w_out):
    u = rmsnorm(h, g) @ w_in
    a, c = jnp.split(u, 2, axis=-1)
    return h + 0.5 * ((jax.nn.silu(a) * c) @ w_out)


def memory_attention(q_mem, mem_n, w_mem_kv):
    k, v = jnp.split(mem_n @ w_mem_kv, 2, axis=-1)
    q = split_heads(q_mem, MEM_HEADS)
    k = split_heads(k, MEM_HEADS)
    v = split_heads(v, MEM_HEADS)
    logits = jnp.einsum('bhqd,bhkd->bhqk', q, k).astype(jnp.float32) * (MEM_HEAD_DIM ** -0.5)
    p = jax.nn.softmax(logits, axis=-1).astype(v.dtype)
    return merge_heads(jnp.einsum('bhqk,bhkd->bhqd', p, v))


def gla_chunked(q, k, v, log_a):
    b, nh, s, dk = q.shape
    dv = v.shape[-1]
    nc = s // GLA_CHUNK

    def chunks(t):
        return t.astype(jnp.float32).reshape(b, nh, nc, GLA_CHUNK, t.shape[-1]).transpose(2, 0, 1, 3, 4)

    causal = jnp.tril(jnp.ones((GLA_CHUNK, GLA_CHUNK), dtype=bool))[:, :, None]

    def step(state, inp):
        qc, kc, vc, gc = inp
        cum = jnp.cumsum(gc, axis=2)
        o_inter = jnp.einsum('bhid,bhde->bhie', qc * jnp.exp(cum), state)
        diff = cum[:, :, :, None, :] - cum[:, :, None, :, :]
        decay = jnp.exp(jnp.where(causal, diff, -jnp.inf))
        scores = jnp.einsum('bhid,bhjd,bhijd->bhij', qc, kc, decay)
        o_intra = jnp.einsum('bhij,bhje->bhie', scores, vc)
        last = cum[:, :, -1:, :]
        k_dec = kc * jnp.exp(last - cum)
        state = jnp.exp(last[:, :, 0, :])[..., None] * state + jnp.einsum('bhjd,bhje->bhde', k_dec, vc)
        return state, o_inter + o_intra

    state0 = jnp.zeros((b, nh, dk, dv), jnp.float32)
    _, out = lax.scan(step, state0, (chunks(q), chunks(k), chunks(v), chunks(log_a)))
    return out.transpose(1, 2, 0, 3, 4).reshape(b, nh, s, dv)


def fox_attention(q, k, v, dcum):
    b, nh, s, dh = q.shape
    nb = s // FOX_BLOCK_Q
    qb = q.reshape(b, nh, nb, FOX_BLOCK_Q, dh).transpose(2, 0, 1, 3, 4)
    db = dcum.reshape(b, nh, nb, FOX_BLOCK_Q).transpose(2, 0, 1, 3)
    kpos = jnp.arange(s)
    scale = dh ** -0.5

    def block(args):
        i, qi, di = args
        logits = jnp.einsum('bhqd,bhkd->bhqk', qi, k).astype(jnp.float32) * scale
        logits = logits + di[..., :, None] - dcum[:, :, None, :]
        qpos = i * FOX_BLOCK_Q + jnp.arange(FOX_BLOCK_Q)
        logits = jnp.where(kpos[None, :] <= qpos[:, None], logits, -jnp.inf)
        p = jax.nn.softmax(logits, axis=-1).astype(v.dtype)
        return jnp.einsum('bhqk,bhkd->bhqd', p, v)

    out = lax.map(block, (jnp.arange(nb), qb, db))
    return out.transpose(1, 2, 0, 3, 4).reshape(b, nh, s, dh)


def gla_mixer(xn, mem_n, w_in, w_gate_up, b_gate, g_head, w_mem_kv, w_out):
    proj = xn @ w_in
    o1 = GLA_DK
    o2 = o1 + GLA_DK
    o3 = o2 + GLA_DV
    o4 = o3 + GLA_DV
    o5 = o4 + GLA_GATE_RANK
    q, k, v, r, g_low, q_mem = (proj[..., :o1], proj[..., o1:o2], proj[..., o2:o3],
                                proj[..., o3:o4], proj[..., o4:o5], proj[..., o5:])
    log_a = jax.nn.log_sigmoid((g_low @ w_gate_up + b_gate).astype(jnp.float32)) / GLA_GATE_TEMP
    o = gla_chunked(split_heads(q * (GLA_DK_HEAD ** -0.5), GLA_HEADS), split_heads(k, GLA_HEADS),
                    split_heads(v, GLA_HEADS), split_heads(log_a, GLA_HEADS))
    o = rmsnorm(o, g_head).astype(xn.dtype)
    o_main = merge_heads(o) * jax.nn.silu(r)
    o_mem = memory_attention(q_mem, mem_n, w_mem_kv)
    return jnp.concatenate([o_main, o_mem], axis=-1) @ w_out


def fox_mixer(xn, mem_n, w_in, k_sh, v_sh, dcum_sh, w_mem_kv, w_out):
    proj = xn @ w_in
    q, q_mem = proj[..., :MAIN_W], proj[..., MAIN_W:]
    o_main = merge_heads(fox_attention(split_heads(q, FOX_HEADS), k_sh, v_sh, dcum_sh))
    o_mem = memory_attention(q_mem, mem_n, w_mem_kv)
    return jnp.concatenate([o_main, o_mem], axis=-1) @ w_out


def setup_inputs(seed: int = 0) -> dict:
    key = jax.random.key(seed)
    ks = jax.random.split(key, 20)

    def dense(k, shape, fan_in, scale=1.0):
        return jax.random.normal(k, shape, jnp.float32) * (scale * fan_in ** -0.5)

    def gain(k, shape):
        return 1.0 + 0.02 * jax.random.normal(k, shape, jnp.float32)

    return {
        "x": jax.random.normal(ks[0], (BATCH, SEQ, D_MODEL), jnp.float32),
        "mem": jax.random.normal(ks[1], (BATCH, MEM_LEN, D_MODEL), jnp.float32),
        "norm_ffn": gain(ks[2], (DEPTH, 2, D_MODEL)),
        "w_ffn_in": dense(ks[3], (DEPTH, 2, D_MODEL, 2 * D_FF), D_MODEL),
        "w_ffn_out": dense(ks[4], (DEPTH, 2, D_FF, D_MODEL), D_FF, 0.5),
        "norm_mix": gain(ks[5], (DEPTH, D_MODEL)),
        "norm_mem": gain(ks[6], (DEPTH, D_MODEL)),
        "w_mem_kv": dense(ks[7], (DEPTH, D_MODEL, 2 * MEM_W), D_MODEL),
        "w_out": dense(ks[8], (DEPTH, MIX_W, D_MODEL), MIX_W, 0.5),
        "w_in_a": dense(ks[9], (N_A_LAYERS, D_MODEL, A_IN_W), D_MODEL),
        "w_gate_up": dense(ks[10], (N_A_LAYERS, GLA_GATE_RANK, GLA_DK), GLA_GATE_RANK),
        "b_gate": 0.1 * jax.random.normal(ks[11], (N_A_LAYERS, GLA_DK), jnp.float32),
        "norm_gla_head": gain(ks[12], (N_A_LAYERS, GLA_DV_HEAD)),
        "w_in_b": dense(ks[13], (N_B_LAYERS, D_MODEL, B_IN_W), D_MODEL),
        "norm_shared": gain(ks[14], (D_MODEL,)),
        "w_kv_shared": dense(ks[15], (D_MODEL, 2 * MAIN_W), D_MODEL),
        "w_fgate": dense(ks[16], (D_MODEL, FOX_HEADS), D_MODEL),
        "b_fgate": FOX_FGATE_BIAS_INIT + 0.1 * jax.random.normal(ks[17], (FOX_HEADS,), jnp.float32),
        "norm_final": gain(ks[18], (D_MODEL,)),
    }


def reference(x, mem, norm_ffn, w_ffn_in, w_ffn_out, norm_mix, norm_mem, w_mem_kv, w_out,
              w_in_a, w_gate_up, b_gate, norm_gla_head, w_in_b, norm_shared, w_kv_shared,
              w_fgate, b_fgate, norm_final):
    h = x
    k_sh = v_sh = dcum_sh = None
    for l in range(DEPTH):
        if l == N_A_LAYERS:
            hs = rmsnorm(h, norm_shared)
            k_all, v_all = jnp.split(hs @ w_kv_shared, 2, axis=-1)
            k_sh = split_heads(k_all, FOX_HEADS)
            v_sh = split_heads(v_all, FOX_HEADS)
            log_f = jax.nn.log_sigmoid((hs @ w_fgate + b_fgate).astype(jnp.float32))
            dcum_sh = jnp.cumsum(log_f, axis=1).transpose(0, 2, 1)
        h = swiglu_half_step(h, norm_ffn[l, 0], w_ffn_in[l, 0], w_ffn_out[l, 0])
        xn = rmsnorm(h, norm_mix[l])
        mem_n = rmsnorm(mem, norm_mem[l])
        if l < N_A_LAYERS:
            mixed = gla_mixer(xn, mem_n, w_in_a[l], w_gate_up[l], b_gate[l], norm_gla_head[l],
                              w_mem_kv[l], w_out[l])
        else:
            j = l - N_A_LAYERS
            mixed = fox_mixer(xn, mem_n, w_in_b[j], k_sh, v_sh, dcum_sh, w_mem_kv[l], w_out[l])
        h = h + mixed
        h = swiglu_half_step(h, norm_ffn[l, 1], w_ffn_in[l, 1], w_ffn_out[l, 1])
    return rmsnorm(h, norm_final)
```

```python
import functools

import numpy as np
import jax
import jax.numpy as jnp
from jax import lax
from jax.experimental import pallas as pl
from jax.experimental.pallas import tpu as pltpu

F32 = jnp.float32
BF16 = jnp.bfloat16

D_MODEL = 1024
DEPTH = 4
N_A = DEPTH // 2
MAIN_W = 768
MEM_W = 256
GLA_HEADS = 4
GLA_DK_HEAD = 96
GLA_DV_HEAD = 192
GLA_RANK = 16
GLA_TEMP = 16.0
FOX_HEADS = 6
FOX_DH = 128
MEM_HEADS = 4
MEM_DH = 64
MEM_LEN = 256
D_FF = 2816
EPS = 1e-6

LANES = 128
MXU_N = 256
VMEM_LIMIT = 56 * 1024 * 1024

DKP = LANES
DVP = MXU_N
GLA_QW = GLA_HEADS * DKP
GLA_VW = GLA_HEADS * DVP
GLA_CHUNK = 64
GLA_LEVELS = 6

TM = 512
FOX_BQ = 512
FOX_BK = 512


def _cparams(sem):
    return pltpu.CompilerParams(dimension_semantics=sem, vmem_limit_bytes=VMEM_LIMIT)


def _rmsnorm(x, g):
    return x * lax.rsqrt(jnp.mean(x * x, axis=-1, keepdims=True) + EPS) * g


def _log_sigmoid(x):
    return jnp.minimum(x, 0.0) - jnp.log(1.0 + jnp.exp(-jnp.abs(x)))


def _sigmoid(x):
    return 1.0 / (1.0 + jnp.exp(-x))


def _split3(x):
    hi = x.astype(BF16)
    r1 = x - hi.astype(F32)
    mid = r1.astype(BF16)
    lo = (r1 - mid.astype(F32)).astype(BF16)
    return jnp.concatenate([hi, mid, lo], axis=1)


def _dot(a, b):
    return jnp.dot(a, b, preferred_element_type=F32)


def _dot_nt(a, b):
    return lax.dot_general(a, b, (((1,), (1,)), ((), ())), preferred_element_type=F32)


def _dot_tn(a, b):
    return lax.dot_general(a, b, (((0,), (0,)), ((), ())), preferred_element_type=F32)


def _exact_sum3(r, w):
    return r[:, :w] + r[:, w:2 * w] + r[:, 2 * w:]


FFN_SPLIT = 2


def _ffn_kernel(h_ref, g_ref, w1_ref, w2_ref, o_ref):
    h = h_ref[...]
    xn = _rmsnorm(h, g_ref[...]).astype(BF16)
    fc = D_FF // FFN_SPLIT
    y = None
    for i in range(FFN_SPLIT):
        a = _dot(xn, w1_ref[:, i * fc:(i + 1) * fc])
        c = _dot(xn, w1_ref[:, D_FF + i * fc:D_FF + (i + 1) * fc])
        act = (a * _sigmoid(a) * c).astype(BF16)
        part = _dot(act, w2_ref[i * fc:(i + 1) * fc, :])
        y = part if y is None else y + part
    o_ref[...] = h + 0.5 * y


def _ffn(h, g, w1, w2):
    n = h.shape[0]
    return pl.pallas_call(
        _ffn_kernel,
        out_shape=jax.ShapeDtypeStruct((n, D_MODEL), F32),
        grid=(n // TM,),
        in_specs=[
            pl.BlockSpec((TM, D_MODEL), lambda i: (i, 0)),
            pl.BlockSpec((1, D_MODEL), lambda i: (0, 0)),
            pl.BlockSpec((D_MODEL, 2 * D_FF), lambda i: (0, 0), pipeline_mode=pl.Buffered(1)),
            pl.BlockSpec((D_FF, D_MODEL), lambda i: (0, 0), pipeline_mode=pl.Buffered(1)),
        ],
        out_specs=pl.BlockSpec((TM, D_MODEL), lambda i: (i, 0)),
        compiler_params=_cparams(("parallel",)),
        name="ffn",
    )(h, g, w1, w2)


GLA_PW = 2 * GLA_QW + 2 * GLA_VW + MEM_W + LANES


def _gla_proj_kernel(h_ref, g_ref, w_ref, wup_ref, bg_ref, q_ref, k_ref, v_ref, r_ref, la_ref, qm_ref):
    xn = _rmsnorm(h_ref[...], g_ref[...]).astype(BF16)
    o = 0
    q_ref[...] = (_dot(xn, w_ref[:, o:o + GLA_QW]) * (GLA_DK_HEAD ** -0.5)).astype(BF16)
    o += GLA_QW
    k_ref[...] = _dot(xn, w_ref[:, o:o + GLA_QW]).astype(BF16)
    o += GLA_QW
    v_ref[...] = _dot(xn, w_ref[:, o:o + GLA_VW]).astype(BF16)
    o += GLA_VW
    r_ref[...] = _dot(xn, w_ref[:, o:o + GLA_VW]).astype(BF16)
    o += GLA_VW
    qm_ref[...] = _dot(xn, w_ref[:, o:o + MEM_W]).astype(BF16)
    o += MEM_W
    g_low = _dot(xn, w_ref[:, o:o + LANES]).astype(BF16)
    x = _dot(g_low, wup_ref[...]) + bg_ref[...]
    la_ref[...] = _log_sigmoid(x) * (1.0 / GLA_TEMP)


def _gla_proj(h, g, w, wup, bg):
    n = h.shape[0]
    row = lambda i: (i, 0)
    fixed = lambda i: (0, 0)
    return pl.pallas_call(
        _gla_proj_kernel,
        out_shape=(
            jax.ShapeDtypeStruct((n, GLA_QW), BF16),
            jax.ShapeDtypeStruct((n, GLA_QW), BF16),
            jax.ShapeDtypeStruct((n, GLA_VW), BF16),
            jax.ShapeDtypeStruct((n, GLA_VW), BF16),
            jax.ShapeDtypeStruct((n, GLA_QW), F32),
            jax.ShapeDtypeStruct((n, MEM_W), BF16),
        ),
        grid=(n // TM,),
        in_specs=[
            pl.BlockSpec((TM, D_MODEL), row),
            pl.BlockSpec((1, D_MODEL), fixed),
            pl.BlockSpec((D_MODEL, GLA_PW), fixed, pipeline_mode=pl.Buffered(1)),
            pl.BlockSpec((LANES, GLA_QW), fixed),
            pl.BlockSpec((1, GLA_QW), fixed),
        ],
        out_specs=(
            pl.BlockSpec((TM, GLA_QW), row),
            pl.BlockSpec((TM, GLA_QW), row),
            pl.BlockSpec((TM, GLA_VW), row),
            pl.BlockSpec((TM, GLA_VW), row),
            pl.BlockSpec((TM, GLA_QW), row),
            pl.BlockSpec((TM, MEM_W), row),
        ),
        compiler_params=_cparams(("parallel",)),
        name="gla_proj",
    )(h, g, w, wup, bg)


GLA_TS = 512


def _gla_consts():
    c = GLA_CHUNK
    i = np.arange(c)[:, None]
    j = np.arange(c)[None, :]
    a_rows = [(j <= i)]
    masks = []
    for lv in range(GLA_LEVELS):
        hs = 1 << lv
        seg = (i // (2 * hs)) * (2 * hs)
        a_rows.append(j <= seg + hs - 1)
        masks.append(((i // (2 * hs)) == (j // (2 * hs))) & ((i % (2 * hs)) >= hs) & ((j % (2 * hs)) < hs))
    masks.append(i == j)
    a = np.concatenate(a_rows, axis=0).astype(np.float32)
    m = np.stack(masks, axis=0).astype(np.float32)
    return jnp.asarray(a, dtype=BF16), jnp.asarray(m, dtype=F32)


def _gla_kernel(q_ref, k_ref, la_ref, v_ref, r_ref, gh_ref, a_ref, m_ref, o_ref, st_ref):
    c = GLA_CHUNK

    @pl.when(pl.program_id(1) == 0)
    def _():
        st_ref[...] = jnp.zeros_like(st_ref)

    row = lax.broadcasted_iota(jnp.int32, (c, DKP), 0)
    gh = gh_ref[...]

    def chunk(ci, carry):
        r0 = pl.multiple_of(ci * c, c)
        g3 = _split3(la_ref[pl.ds(r0, c), :])
        refs = _exact_sum3(_dot(a_ref[...], g3), GLA_QW)
        cum = refs[0:c]
        last = cum[c - 1:c, :]
        q = q_ref[pl.ds(r0, c), :].astype(F32)
        k = k_ref[pl.ds(r0, c), :].astype(F32)
        v = v_ref[pl.ds(r0, c), :]
        r = r_ref[pl.ds(r0, c), :].astype(F32)
        for h in range(GLA_HEADS):
            ks = slice(h * DKP, (h + 1) * DKP)
            vs = slice(h * DVP, (h + 1) * DVP)
            qh, kh, vh = q[:, ks], k[:, ks], v[:, vs]
            cum_h, last_h = cum[:, ks], last[:, ks]
            st = st_ref[h]
            qb = qh.astype(BF16)
            s = m_ref[GLA_LEVELS] * _dot_nt(qb, kh.astype(BF16))
            for lv in range(GLA_LEVELS):
                ref = refs[(lv + 1) * c:(lv + 2) * c, ks]
                lower = ((row >> lv) & 1) == 1
                x = (jnp.where(lower, qh, kh) * jnp.exp(-jnp.abs(cum_h - ref))).astype(BF16)
                s = s + m_ref[lv] * _dot_nt(x, x)
            o = _dot_nt((qh * jnp.exp(cum_h)).astype(BF16), st.astype(BF16)) + _dot(s.astype(BF16), vh)
            kd = (kh * jnp.exp(last_h - cum_h)).astype(BF16)
            st_ref[h] = st * jnp.exp(last_h) + _dot_tn(vh, kd)
            ms = jnp.sum(o * o, axis=-1, keepdims=True) * (1.0 / GLA_DV_HEAD)
            rh = r[:, vs]
            y = o * lax.rsqrt(ms + EPS) * gh * (rh * _sigmoid(rh))
            o_ref[pl.ds(r0, c), vs] = y.astype(BF16)
        return carry

    lax.fori_loop(0, GLA_TS // c, chunk, 0)


def _gla(q, k, la, v, r, gh, batch, seq):
    n = q.shape[0]
    nt = seq // GLA_TS
    a, m = _gla_consts()
    row = lambda b, t: (b * nt + t, 0)
    fixed2 = lambda b, t: (0, 0)
    fixed3 = lambda b, t: (0, 0, 0)
    return pl.pallas_call(
        _gla_kernel,
        out_shape=jax.ShapeDtypeStruct((n, GLA_VW), BF16),
        grid=(batch, nt),
        in_specs=[
            pl.BlockSpec((GLA_TS, GLA_QW), row),
            pl.BlockSpec((GLA_TS, GLA_QW), row),
            pl.BlockSpec((GLA_TS, GLA_QW), row),
            pl.BlockSpec((GLA_TS, GLA_VW), row),
            pl.BlockSpec((GLA_TS, GLA_VW), row),
            pl.BlockSpec((1, DVP), fixed2),
            pl.BlockSpec(a.shape, fixed2),
            pl.BlockSpec(m.shape, fixed3),
        ],
        out_specs=pl.BlockSpec((GLA_TS, GLA_VW), row),
        scratch_shapes=[pltpu.VMEM((GLA_HEADS, DVP, DKP), F32)],
        compiler_params=_cparams(("arbitrary", "arbitrary")),
        name="gla",
    )(q, k, la, v, r, gh, a, m)


MEM_BW = MEM_HEADS * MEM_LEN


def _mem_kv_kernel(mem_ref, g_ref, wkt_ref, wv_ref, kb_ref, vb_ref):
    mn = _rmsnorm(mem_ref[0], g_ref[0]).astype(BF16)
    kt = _dot_nt(wkt_ref[0], mn) * (MEM_DH ** -0.5)
    v = _dot(mn, wv_ref[0])
    d_row = lax.broadcasted_iota(jnp.int32, (MEM_W, MEM_LEN), 0) // MEM_DH
    d_col = lax.broadcasted_iota(jnp.int32, (MEM_LEN, MEM_W), 1) // MEM_DH
    for hh in range(MEM_HEADS):
        kb_ref[0, 0, :, hh * MEM_LEN:(hh + 1) * MEM_LEN] = jnp.where(d_row == hh, kt, 0.0).astype(BF16)
        vb_ref[0, 0, hh * MEM_LEN:(hh + 1) * MEM_LEN, :] = jnp.where(d_col == hh, v, 0.0).astype(BF16)


def _mem_kv(mem, g, wkt, wv):
    b = mem.shape[0]
    return pl.pallas_call(
        _mem_kv_kernel,
        out_shape=(
            jax.ShapeDtypeStruct((DEPTH, b, MEM_W, MEM_BW), BF16),
            jax.ShapeDtypeStruct((DEPTH, b, MEM_BW, MEM_W), BF16),
        ),
        grid=(DEPTH, b),
        in_specs=[
            pl.BlockSpec((1, MEM_LEN, D_MODEL), lambda l, i: (i, 0, 0)),
            pl.BlockSpec((1, 1, D_MODEL), lambda l, i: (l, 0, 0)),
            pl.BlockSpec((1, MEM_W, D_MODEL), lambda l, i: (l, 0, 0)),
            pl.BlockSpec((1, D_MODEL, MEM_W), lambda l, i: (l, 0, 0)),
        ],
        out_specs=(
            pl.BlockSpec((1, 1, MEM_W, MEM_BW), lambda l, i: (l, i, 0, 0)),
            pl.BlockSpec((1, 1, MEM_BW, MEM_W), lambda l, i: (l, i, 0, 0)),
        ),
        compiler_params=_cparams(("parallel", "parallel")),
        name="mem_kv",
    )(mem, g, wkt, wv)


def _mix_out_kernel(h_ref, om_ref, qm_ref, kb_ref, vb_ref, wo_ref, o_ref):
    lg = _dot(qm_ref[...], kb_ref[0, 0])
    ps = []
    for hh in range(MEM_HEADS):
        s = lg[:, hh * MEM_LEN:(hh + 1) * MEM_LEN]
        e = jnp.exp(s - jnp.max(s, axis=-1, keepdims=True))
        ps.append((e / jnp.sum(e, axis=-1, keepdims=True)).astype(BF16))
    o_mem = _dot(jnp.concatenate(ps, axis=1), vb_ref[0, 0])
    x = jnp.concatenate([om_ref[...], o_mem.astype(BF16)], axis=1)
    o_ref[...] = h_ref[...] + _dot(x, wo_ref[...])


def _mix_out(h, o_main, q_mem, kb, vb, wo, layer, seq):
    n = h.shape[0]
    wm = o_main.shape[1]
    per_b = seq // TM
    row = lambda i: (i, 0)
    return pl.pallas_call(
        _mix_out_kernel,
        out_shape=jax.ShapeDtypeStruct((n, D_MODEL), F32),
        grid=(n // TM,),
        in_specs=[
            pl.BlockSpec((TM, D_MODEL), row),
            pl.BlockSpec((TM, wm), row),
            pl.BlockSpec((TM, MEM_W), row),
            pl.BlockSpec((1, 1, MEM_W, MEM_BW), lambda i: (layer, i // per_b, 0, 0)),
            pl.BlockSpec((1, 1, MEM_BW, MEM_W), lambda i: (layer, i // per_b, 0, 0)),
            pl.BlockSpec((wm + MEM_W, D_MODEL), lambda i: (0, 0)),
        ],
        out_specs=pl.BlockSpec((TM, D_MODEL), row),
        compiler_params=_cparams(("parallel",)),
        name="mix_out",
    )(h, o_main, q_mem, kb, vb, wo)


SH_PW = 2 * MAIN_W + LANES


def _fox_aux_consts():
    pq = np.zeros((3 * LANES, MAIN_W), np.float32)
    pk = np.zeros((3 * LANES, MAIN_W), np.float32)
    cq = np.zeros((1, MAIN_W), np.float32)
    ck = np.zeros((1, MAIN_W), np.float32)
    for h in range(FOX_HEADS):
        for p in range(3):
            pq[p * LANES + h, h * FOX_DH + p] = 1.0
            cq[0, h * FOX_DH + 3 + p] = 1.0
            pk[p * LANES + h, h * FOX_DH + 3 + p] = -1.0
            ck[0, h * FOX_DH + p] = 1.0
    tril = np.tril(np.ones((TM, TM), np.float32))
    return (jnp.asarray(tril, dtype=BF16), jnp.asarray(pq, dtype=BF16), jnp.asarray(pk, dtype=BF16),
            jnp.asarray(cq), jnp.asarray(ck))


def _shared_kernel(h_ref, g_ref, w_ref, bf_ref, tril_ref, pq_ref, pk_ref, cq_ref, ck_ref,
                   k_ref, v_ref, aq_ref, ak_ref, carry_ref):
    @pl.when(pl.program_id(1) == 0)
    def _():
        carry_ref[...] = jnp.zeros_like(carry_ref)

    xn = _rmsnorm(h_ref[...], g_ref[...]).astype(BF16)
    k_ref[...] = _dot(xn, w_ref[:, 0:MAIN_W]).astype(BF16)
    v_ref[...] = _dot(xn, w_ref[:, MAIN_W:2 * MAIN_W]).astype(BF16)
    log_f = _log_sigmoid(_dot(xn, w_ref[:, 2 * MAIN_W:SH_PW]) + bf_ref[...])
    d = _exact_sum3(_dot(tril_ref[...], _split3(log_f)), LANES) + carry_ref[...]
    carry_ref[...] = d[TM - 1:TM, :]
    d3 = _split3(d)
    aq_ref[...] = (_dot(d3, pq_ref[...]) + cq_ref[...]).astype(BF16)
    ak_ref[...] = (_dot(d3, pk_ref[...]) + ck_ref[...]).astype(BF16)


def _shared_kv(h, g, w, bf, batch, seq):
    n = h.shape[0]
    nt = seq // TM
    tril, pq, pk, cq, ck = _fox_aux_consts()
    row = lambda b, t: (b * nt + t, 0)
    fixed = lambda b, t: (0, 0)
    out = jax.ShapeDtypeStruct((n, MAIN_W), BF16)
    return pl.pallas_call(
        _shared_kernel,
        out_shape=(out, out, out, out),
        grid=(batch, nt),
        in_specs=[
            pl.BlockSpec((TM, D_MODEL), row),
            pl.BlockSpec((1, D_MODEL), fixed),
            pl.BlockSpec((D_MODEL, SH_PW), fixed),
            pl.BlockSpec((1, LANES), fixed),
            pl.BlockSpec((TM, TM), fixed),
            pl.BlockSpec((3 * LANES, MAIN_W), fixed),
            pl.BlockSpec((3 * LANES, MAIN_W), fixed),
            pl.BlockSpec((1, MAIN_W), fixed),
            pl.BlockSpec((1, MAIN_W), fixed),
        ],
        out_specs=tuple(pl.BlockSpec((TM, MAIN_W), row) for _ in range(4)),
        scratch_shapes=[pltpu.VMEM((1, LANES), F32)],
        compiler_params=_cparams(("arbitrary", "arbitrary")),
        name="shared_kv",
    )(h, g, w, bf, tril, pq, pk, cq, ck)


def _fox_proj_kernel(h_ref, g_ref, w_ref, q_ref, qm_ref):
    xn = _rmsnorm(h_ref[...], g_ref[...]).astype(BF16)
    q_ref[...] = (_dot(xn, w_ref[:, 0:MAIN_W]) * (FOX_DH ** -0.5)).astype(BF16)
    qm_ref[...] = _dot(xn, w_ref[:, MAIN_W:MAIN_W + MEM_W]).astype(BF16)


def _fox_proj(h, g, w):
    n = h.shape[0]
    row = lambda i: (i, 0)
    fixed = lambda i: (0, 0)
    return pl.pallas_call(
        _fox_proj_kernel,
        out_shape=(jax.ShapeDtypeStruct((n, MAIN_W), BF16), jax.ShapeDtypeStruct((n, MEM_W), BF16)),
        grid=(n // TM,),
        in_specs=[
            pl.BlockSpec((TM, D_MODEL), row),
            pl.BlockSpec((1, D_MODEL), fixed),
            pl.BlockSpec((D_MODEL, MAIN_W + MEM_W), fixed),
        ],
        out_specs=(pl.BlockSpec((TM, MAIN_W), row), pl.BlockSpec((TM, MEM_W), row)),
        compiler_params=_cparams(("parallel",)),
        name="fox_proj",
    )(h, g, w)


def _fox_attn_kernel(q_ref, aq_ref, k_ref, ak_ref, v_ref, o_ref):
    qi = pl.program_id(2)
    qq = jnp.concatenate([q_ref[0], aq_ref[0]], axis=1)

    def step(kj, carry, diagonal):
        m, l, acc = carry
        c0 = pl.multiple_of(kj * FOX_BK, FOX_BK)
        kk = jnp.concatenate([k_ref[0, pl.ds(c0, FOX_BK), :], ak_ref[0, pl.ds(c0, FOX_BK), :]], axis=1)
        s = _dot_nt(qq, kk)
        if diagonal:
            r_i = lax.broadcasted_iota(jnp.int32, s.shape, 0)
            c_i = lax.broadcasted_iota(jnp.int32, s.shape, 1)
            s = jnp.where(c_i <= r_i, s, -jnp.inf)
        m_new = jnp.maximum(m, jnp.max(s, axis=-1, keepdims=True))
        alpha = jnp.exp(m - m_new)
        p = jnp.exp(s - m_new)
        l = alpha * l + jnp.sum(p, axis=-1, keepdims=True)
        acc = alpha * acc + _dot(p.astype(BF16), v_ref[0, pl.ds(c0, FOX_BK), :])
        return m_new, l, acc

    init = (jnp.full((FOX_BQ, 1), -jnp.inf, F32), jnp.zeros((FOX_BQ, 1), F32), jnp.zeros((FOX_BQ, FOX_DH), F32))
    carry = lax.fori_loop(0, qi, lambda kj, cr: step(kj, cr, False), init)
    _, l, acc = step(qi, carry, True)
    o_ref[0] = (acc / l).astype(BF16)


def _fox_attn(q, aq, k, ak, v, batch, seq):
    q3 = lambda a: a.reshape(batch, seq, MAIN_W)
    qblk = pl.BlockSpec((1, FOX_BQ, FOX_DH), lambda b, h, i: (b, i, h))
    kblk = pl.BlockSpec((1, seq, FOX_DH), lambda b, h, i: (b, 0, h))
    out = pl.pallas_call(
        _fox_attn_kernel,
        out_shape=jax.ShapeDtypeStruct((batch, seq, MAIN_W), BF16),
        grid=(batch, FOX_HEADS, seq // FOX_BQ),
        in_specs=[qblk, qblk, kblk, kblk, kblk],
        out_specs=qblk,
        compiler_params=_cparams(("parallel", "parallel", "arbitrary")),
        name="fox_attn",
    )(q3(q), q3(aq), q3(k), q3(ak), q3(v))
    return out.reshape(batch * seq, MAIN_W)


def _final_norm_kernel(h_ref, g_ref, o_ref):
    o_ref[...] = _rmsnorm(h_ref[...], g_ref[...])


def _final_norm(h, g):
    n = h.shape[0]
    return pl.pallas_call(
        _final_norm_kernel,
        out_shape=jax.ShapeDtypeStruct((n, D_MODEL), F32),
        grid=(n // TM,),
        in_specs=[pl.BlockSpec((TM, D_MODEL), lambda i: (i, 0)), pl.BlockSpec((1, D_MODEL), lambda i: (0, 0))],
        out_specs=pl.BlockSpec((TM, D_MODEL), lambda i: (i, 0)),
        compiler_params=_cparams(("parallel",)),
        name="final_norm",
    )(h, g)


def _pad_heads_cols(w, heads, width, padded):
    lead = w.shape[:-1]
    w = w.reshape(lead + (heads, width))
    w = jnp.pad(w, [(0, 0)] * len(lead) + [(0, 0), (0, padded - width)])
    return w.reshape(lead + (heads * padded,))


def _pad_heads_rows(w, heads, width, padded):
    n = w.shape[-1]
    w = w.reshape(heads, width, n)
    w = jnp.pad(w, [(0, 0), (0, padded - width), (0, 0)])
    return w.reshape(heads * padded, n)


def _gla_weights(w_in, w_gate_up, b_gate, g_head, w_out):
    dk = GLA_HEADS * GLA_DK_HEAD
    dv = GLA_HEADS * GLA_DV_HEAD
    o1, o2, o3, o4, o5 = dk, 2 * dk, 2 * dk + dv, 2 * dk + 2 * dv, 2 * dk + 2 * dv + GLA_RANK
    w = jnp.concatenate([
        _pad_heads_cols(w_in[:, :o1], GLA_HEADS, GLA_DK_HEAD, DKP),
        _pad_heads_cols(w_in[:, o1:o2], GLA_HEADS, GLA_DK_HEAD, DKP),
        _pad_heads_cols(w_in[:, o2:o3], GLA_HEADS, GLA_DV_HEAD, DVP),
        _pad_heads_cols(w_in[:, o3:o4], GLA_HEADS, GLA_DV_HEAD, DVP),
        w_in[:, o5:],
        jnp.pad(w_in[:, o4:o5], [(0, 0), (0, LANES - GLA_RANK)]),
    ], axis=1).astype(BF16)
    wup = jnp.pad(_pad_heads_cols(w_gate_up, GLA_HEADS, GLA_DK_HEAD, DKP), [(0, LANES - GLA_RANK), (0, 0)]).astype(BF16)
    bg = _pad_heads_cols(b_gate[None, :], GLA_HEADS, GLA_DK_HEAD, DKP)
    gh = jnp.pad(g_head[None, :], [(0, 0), (0, DVP - GLA_DV_HEAD)])
    wo = jnp.concatenate([_pad_heads_rows(w_out[:MAIN_W], GLA_HEADS, GLA_DV_HEAD, DVP), w_out[MAIN_W:]], axis=0).astype(BF16)
    return w, wup, bg, gh, wo


def kernel(x, mem, norm_ffn, w_ffn_in, w_ffn_out, norm_mix, norm_mem, w_mem_kv, w_out, w_in_a, w_gate_up, b_gate,
           norm_gla_head, w_in_b, norm_shared, w_kv_shared, w_fgate, b_fgate, norm_final):
    batch, seq, _ = x.shape
    h = x.reshape(batch * seq, D_MODEL)

    w1 = w_ffn_in.astype(BF16)
    w2 = w_ffn_out.astype(BF16)
    wkt = jnp.swapaxes(w_mem_kv[:, :, :MEM_W], 1, 2).astype(BF16)
    wv = w_mem_kv[:, :, MEM_W:].astype(BF16)
    kb, vb = _mem_kv(mem, norm_mem[:, None, :], wkt, wv)

    k_sh = v_sh = aq_sh = ak_sh = None
    for l in range(DEPTH):
        if l == N_A:
            w_sh = jnp.concatenate([w_kv_shared, jnp.pad(w_fgate, [(0, 0), (0, LANES - FOX_HEADS)])], axis=1).astype(BF16)
            bf = jnp.pad(b_fgate[None, :], [(0, 0), (0, LANES - FOX_HEADS)])
            k_sh, v_sh, aq_sh, ak_sh = _shared_kv(h, norm_shared[None, :], w_sh, bf, batch, seq)
        h = _ffn(h, norm_ffn[l, 0][None, :], w1[l, 0], w2[l, 0])
        if l < N_A:
            w, wup, bg, gh, wo = _gla_weights(w_in_a[l], w_gate_up[l], b_gate[l], norm_gla_head[l], w_out[l])
            q, k, v, r, la, q_mem = _gla_proj(h, norm_mix[l][None, :], w, wup, bg)
            o_main = _gla(q, k, la, v, r, gh, batch, seq)
        else:
            q, q_mem = _fox_proj(h, norm_mix[l][None, :], w_in_b[l - N_A].astype(BF16))
            o_main = _fox_attn(q, aq_sh, k_sh, ak_sh, v_sh, batch, seq)
            wo = w_out[l].astype(BF16)
        h = _mix_out(h, o_main, q_mem, kb, vb, wo, l, seq)
        h = _ffn(h, norm_ffn[l, 1][None, :], w1[l, 1], w2[l, 1])
    return _final_norm(h, norm_final[None, :]).reshape(batch, seq, D_MODEL)
```

```python
import functools

import numpy as np
import jax
import jax.numpy as jnp
from jax import lax
from jax.experimental import pallas as pl
from jax.experimental.pallas import tpu as pltpu

F32 = jnp.float32
BF16 = jnp.bfloat16

D_MODEL = 1024
DEPTH = 4
N_A = DEPTH // 2
MAIN_W = 768
MEM_W = 256
GLA_HEADS = 4
GLA_DK_HEAD = 96
GLA_DV_HEAD = 192
GLA_RANK = 16
GLA_TEMP = 16.0
FOX_HEADS = 6
FOX_DH = 128
MEM_HEADS = 4
MEM_DH = 64
MEM_LEN = 256
D_FF = 2816
EPS = 1e-6
LOG2E = 1.4426950408889634

LANES = 128
MXU_N = 256
BF16_ROWS = 16
VMEM_LIMIT = 56 * 1024 * 1024

DKP = LANES
DVP = MXU_N
GLA_QW = GLA_HEADS * DKP
GLA_VW = GLA_HEADS * DVP
GLA_CHUNK = 64
GLA_LEVELS = 6

TM = 512
FOX_BQ = TM
FOX_BK = TM


def _cparams(sem):
    return pltpu.CompilerParams(dimension_semantics=sem, vmem_limit_bytes=VMEM_LIMIT)


def _rmsnorm(x, g):
    return x * lax.rsqrt(jnp.mean(x * x, axis=-1, keepdims=True) + EPS) * g


def _log_sigmoid(x):
    return jnp.minimum(x, 0.0) - jnp.log(1.0 + jnp.exp(-jnp.abs(x)))


def _sigmoid(x):
    return 1.0 / (1.0 + jnp.exp(-x))


def _split3(x):
    hi = x.astype(BF16)
    r1 = x - hi.astype(F32)
    mid = r1.astype(BF16)
    lo = (r1 - mid.astype(F32)).astype(BF16)
    return jnp.concatenate([hi, mid, lo], axis=1)


def _split2(x):
    hi = x.astype(BF16)
    lo = (x - hi.astype(F32)).astype(BF16)
    return jnp.concatenate([hi, lo], axis=1)


def _dot(a, b):
    return jnp.dot(a, b, preferred_element_type=F32)


def _dot_nt(a, b):
    return lax.dot_general(a, b, (((1,), (1,)), ((), ())), preferred_element_type=F32)


def _dot_tn(a, b):
    return lax.dot_general(a, b, (((0,), (0,)), ((), ())), preferred_element_type=F32)


def _exact_sum3(r, w):
    return r[:, :w] + r[:, w:2 * w] + r[:, 2 * w:]


FFN_BOUNDS = (0, 6 * MXU_N, D_FF)


def _ffn_kernel(h_ref, g_ref, w1_ref, w2_ref, o_ref):
    h = h_ref[...]
    xn = _rmsnorm(h, g_ref[...]).astype(BF16)
    y = None
    for f0, f1 in zip(FFN_BOUNDS[:-1], FFN_BOUNDS[1:]):
        a = _dot(xn, w1_ref[:, f0:f1])
        c = _dot(xn, w1_ref[:, D_FF + f0:D_FF + f1])
        act = (a * _sigmoid(a) * c).astype(BF16)
        part = _dot(act, w2_ref[f0:f1, :])
        y = part if y is None else y + part
    o_ref[...] = h + 0.5 * y


def _ffn(h, g, w1, w2):
    n = h.shape[0]
    return pl.pallas_call(
        _ffn_kernel,
        out_shape=jax.ShapeDtypeStruct((n, D_MODEL), F32),
        grid=(n // TM,),
        in_specs=[
            pl.BlockSpec((TM, D_MODEL), lambda i: (i, 0)),
            pl.BlockSpec((1, D_MODEL), lambda i: (0, 0)),
            pl.BlockSpec((D_MODEL, 2 * D_FF), lambda i: (0, 0), pipeline_mode=pl.Buffered(1)),
            pl.BlockSpec((D_FF, D_MODEL), lambda i: (0, 0), pipeline_mode=pl.Buffered(1)),
        ],
        out_specs=pl.BlockSpec((TM, D_MODEL), lambda i: (i, 0)),
        compiler_params=_cparams(("parallel",)),
        name="ffn",
    )(h, g, w1, w2)


GLA_PW = 2 * GLA_QW + 2 * GLA_VW + MEM_W + LANES


def _gla_proj_kernel(h_ref, g_ref, w_ref, wup_ref, bg_ref, q_ref, k_ref, v_ref, r_ref, la_ref, qm_ref):
    xn = _rmsnorm(h_ref[...], g_ref[...]).astype(BF16)
    o = 0
    q_ref[...] = (_dot(xn, w_ref[:, o:o + GLA_QW]) * (GLA_DK_HEAD ** -0.5)).astype(BF16)
    o += GLA_QW
    k_ref[...] = _dot(xn, w_ref[:, o:o + GLA_QW]).astype(BF16)
    o += GLA_QW
    v_ref[...] = _dot(xn, w_ref[:, o:o + GLA_VW]).astype(BF16)
    o += GLA_VW
    r_ref[...] = _dot(xn, w_ref[:, o:o + GLA_VW]).astype(BF16)
    o += GLA_VW
    qm_ref[...] = _dot(xn, w_ref[:, o:o + MEM_W]).astype(BF16)
    o += MEM_W
    g_low = _dot(xn, w_ref[:, o:o + LANES]).astype(BF16)
    x = _dot(g_low, wup_ref[...]) + bg_ref[...]
    la_ref[...] = _log_sigmoid(x) * (LOG2E / GLA_TEMP)


def _gla_proj(h, g, w, wup, bg):
    n = h.shape[0]
    row = lambda i: (i, 0)
    fixed = lambda i: (0, 0)
    return pl.pallas_call(
        _gla_proj_kernel,
        out_shape=(
            jax.ShapeDtypeStruct((n, GLA_QW), BF16),
            jax.ShapeDtypeStruct((n, GLA_QW), BF16),
            jax.ShapeDtypeStruct((n, GLA_VW), BF16),
            jax.ShapeDtypeStruct((n, GLA_VW), BF16),
            jax.ShapeDtypeStruct((n, GLA_QW), F32),
            jax.ShapeDtypeStruct((n, MEM_W), BF16),
        ),
        grid=(n // TM,),
        in_specs=[
            pl.BlockSpec((TM, D_MODEL), row),
            pl.BlockSpec((1, D_MODEL), fixed),
            pl.BlockSpec((D_MODEL, GLA_PW), fixed, pipeline_mode=pl.Buffered(1)),
            pl.BlockSpec((LANES, GLA_QW), fixed),
            pl.BlockSpec((1, GLA_QW), fixed),
        ],
        out_specs=(
            pl.BlockSpec((TM, GLA_QW), row),
            pl.BlockSpec((TM, GLA_QW), row),
            pl.BlockSpec((TM, GLA_VW), row),
            pl.BlockSpec((TM, GLA_VW), row),
            pl.BlockSpec((TM, GLA_QW), row),
            pl.BlockSpec((TM, MEM_W), row),
        ),
        compiler_params=_cparams(("parallel",)),
        name="gla_proj",
    )(h, g, w, wup, bg)


GLA_TS = 512


GLA_UNROLL = 2


def _gla_consts():
    c = GLA_CHUNK
    i = np.arange(c)[:, None]
    j = np.arange(c)[None, :]
    tril = (j <= i)
    b_rows, masks = [], []
    for lv in range(GLA_LEVELS):
        hs = 1 << lv
        mid = (i // (2 * hs)) * (2 * hs) + hs
        lower = (i % (2 * hs)) >= hs
        b_rows.append(np.where(lower, (j >= mid) & (j <= i), (j > i) & (j < mid)))
        masks.append(((i // (2 * hs)) == (j // (2 * hs))) & lower & ((j % (2 * hs)) < hs))
    masks.append(i == j)
    asb = lambda x: jnp.asarray(x.astype(np.float32), dtype=BF16)
    return asb(tril), asb(np.concatenate(b_rows, axis=0)), asb(np.stack(masks, axis=0))


def _gla_kernel(q_ref, k_ref, la_ref, v_ref, r_ref, gh_ref, tril_ref, b_ref, m_ref, o_ref, st_ref):
    c = GLA_CHUNK

    @pl.when(pl.program_id(1) == 0)
    def _():
        st_ref[...] = jnp.zeros_like(st_ref)

    gh = gh_ref[...]

    def chunk(ci, carry):
        r0 = pl.multiple_of(ci * c, c)
        g = la_ref[pl.ds(r0, c), :]
        cum = _dot(tril_ref[...], _split2(g))
        cum = cum[:, :GLA_QW] + cum[:, GLA_QW:]
        lev = _dot(b_ref[...], g.astype(BF16))
        qa = q_ref[pl.ds(r0, c), :]
        ka = k_ref[pl.ds(r0, c), :]
        v = v_ref[pl.ds(r0, c), :]
        r = r_ref[pl.ds(r0, c), :].astype(F32)
        for h in range(GLA_HEADS):
            ks = slice(h * DKP, (h + 1) * DKP)
            vs = slice(h * DVP, (h + 1) * DVP)
            qb, kb, vh = qa[:, ks], ka[:, ks], v[:, vs]
            cum_h = cum[:, ks]
            last_h = cum_h[c - 1:c, :]
            st = st_ref[h]
            s = _dot_nt(qb, kb).astype(BF16) * m_ref[GLA_LEVELS]
            for lv in range(GLA_LEVELS):
                e = jnp.exp2(lev[lv * c:(lv + 1) * c, ks]).astype(BF16)
                s = s + _dot_nt(qb * e, kb * e).astype(BF16) * m_ref[lv]
            qg = (qb.astype(F32) * jnp.exp2(cum_h)).astype(BF16)
            o = _dot_nt(qg, st.astype(BF16)) + _dot(s, vh)
            kd = (kb.astype(F32) * jnp.exp2(last_h - cum_h)).astype(BF16)
            st_ref[h] = st * jnp.exp2(last_h) + _dot_tn(vh, kd)
            ms = jnp.sum(o * o, axis=-1, keepdims=True) * (1.0 / GLA_DV_HEAD)
            rh = r[:, vs]
            y = o * lax.rsqrt(ms + EPS) * gh * (rh * _sigmoid(rh))
            o_ref[pl.ds(r0, c), vs] = y.astype(BF16)
        return carry

    lax.fori_loop(0, GLA_TS // c, chunk, 0, unroll=GLA_UNROLL)


def _gla(q, k, la, v, r, gh, batch, seq):
    n = q.shape[0]
    nt = seq // GLA_TS
    tril, bmat, m = _gla_consts()
    row = lambda b, t: (b * nt + t, 0)
    fixed2 = lambda b, t: (0, 0)
    fixed3 = lambda b, t: (0, 0, 0)
    return pl.pallas_call(
        _gla_kernel,
        out_shape=jax.ShapeDtypeStruct((n, GLA_VW), BF16),
        grid=(batch, nt),
        in_specs=[
            pl.BlockSpec((GLA_TS, GLA_QW), row),
            pl.BlockSpec((GLA_TS, GLA_QW), row),
            pl.BlockSpec((GLA_TS, GLA_QW), row),
            pl.BlockSpec((GLA_TS, GLA_VW), row),
            pl.BlockSpec((GLA_TS, GLA_VW), row),
            pl.BlockSpec((1, DVP), fixed2),
            pl.BlockSpec(tril.shape, fixed2),
            pl.BlockSpec(bmat.shape, fixed2),
            pl.BlockSpec(m.shape, fixed3),
        ],
        out_specs=pl.BlockSpec((GLA_TS, GLA_VW), row),
        scratch_shapes=[pltpu.VMEM((GLA_HEADS, DVP, DKP), F32)],
        compiler_params=_cparams(("arbitrary", "arbitrary")),
        name="gla",
    )(q, k, la, v, r, gh, tril, bmat, m)


MEM_BW = MEM_HEADS * MEM_LEN


def _mem_kv_kernel(mem_ref, g_ref, wkt_ref, wv_ref, kb_ref, vb_ref):
    mn = _rmsnorm(mem_ref[0], g_ref[0]).astype(BF16)
    kt = _dot_nt(wkt_ref[0], mn) * (MEM_DH ** -0.5)
    v = _dot(mn, wv_ref[0])
    d_row = lax.broadcasted_iota(jnp.int32, (MEM_W, MEM_LEN), 0) // MEM_DH
    d_col = lax.broadcasted_iota(jnp.int32, (MEM_LEN, MEM_W), 1) // MEM_DH
    for hh in range(MEM_HEADS):
        kb_ref[0, 0, :, hh * MEM_LEN:(hh + 1) * MEM_LEN] = jnp.where(d_row == hh, kt, 0.0).astype(BF16)
        vb_ref[0, 0, hh * MEM_LEN:(hh + 1) * MEM_LEN, :] = jnp.where(d_col == hh, v, 0.0).astype(BF16)


def _mem_kv(mem, g, wkt, wv):
    b = mem.shape[0]
    return pl.pallas_call(
        _mem_kv_kernel,
        out_shape=(
            jax.ShapeDtypeStruct((DEPTH, b, MEM_W, MEM_BW), BF16),
            jax.ShapeDtypeStruct((DEPTH, b, MEM_BW, MEM_W), BF16),
        ),
        grid=(DEPTH, b),
        in_specs=[
            pl.BlockSpec((1, MEM_LEN, D_MODEL), lambda l, i: (i, 0, 0)),
            pl.BlockSpec((1, 1, D_MODEL), lambda l, i: (l, 0, 0)),
            pl.BlockSpec((1, MEM_W, D_MODEL), lambda l, i: (l, 0, 0)),
            pl.BlockSpec((1, D_MODEL, MEM_W), lambda l, i: (l, 0, 0)),
        ],
        out_specs=(
            pl.BlockSpec((1, 1, MEM_W, MEM_BW), lambda l, i: (l, i, 0, 0)),
            pl.BlockSpec((1, 1, MEM_BW, MEM_W), lambda l, i: (l, i, 0, 0)),
        ),
        compiler_params=_cparams(("parallel", "parallel")),
        name="mem_kv",
    )(mem, g, wkt, wv)


def _mix_out_kernel(h_ref, om_ref, qm_ref, kb_ref, vb_ref, wo_ref, o_ref):
    lg = _dot(qm_ref[...], kb_ref[0, 0])
    ps = []
    for hh in range(MEM_HEADS):
        s = lg[:, hh * MEM_LEN:(hh + 1) * MEM_LEN]
        e = jnp.exp(s - jnp.max(s, axis=-1, keepdims=True))
        ps.append((e / jnp.sum(e, axis=-1, keepdims=True)).astype(BF16))
    o_mem = _dot(jnp.concatenate(ps, axis=1), vb_ref[0, 0])
    x = jnp.concatenate([om_ref[...], o_mem.astype(BF16)], axis=1)
    o_ref[...] = h_ref[...] + _dot(x, wo_ref[...])


def _mix_out(h, o_main, q_mem, kb, vb, wo, layer, seq):
    n = h.shape[0]
    wm = o_main.shape[1]
    per_b = seq // TM
    row = lambda i: (i, 0)
    return pl.pallas_call(
        _mix_out_kernel,
        out_shape=jax.ShapeDtypeStruct((n, D_MODEL), F32),
        grid=(n // TM,),
        in_specs=[
            pl.BlockSpec((TM, D_MODEL), row),
            pl.BlockSpec((TM, wm), row),
            pl.BlockSpec((TM, MEM_W), row),
            pl.BlockSpec((1, 1, MEM_W, MEM_BW), lambda i: (layer, i // per_b, 0, 0)),
            pl.BlockSpec((1, 1, MEM_BW, MEM_W), lambda i: (layer, i // per_b, 0, 0)),
            pl.BlockSpec((wm + MEM_W, D_MODEL), lambda i: (0, 0)),
        ],
        out_specs=pl.BlockSpec((TM, D_MODEL), row),
        compiler_params=_cparams(("parallel",)),
        name="mix_out",
    )(h, o_main, q_mem, kb, vb, wo)


SH_PW = MAIN_W + LANES


ONE_LANE = LANES - 1


def _fox_aux_consts():
    pqt = np.zeros((MAIN_W, 3 * LANES), np.float32)
    pk = np.zeros((3 * LANES, MAIN_W), np.float32)
    for h in range(FOX_HEADS):
        for p in range(3):
            pqt[h * FOX_DH + p, p * LANES + h] = 1.0
            pqt[h * FOX_DH + 3 + p, ONE_LANE] = 1.0
            pk[ONE_LANE, h * FOX_DH + p] = 1.0
            pk[p * LANES + h, h * FOX_DH + 3 + p] = -1.0
    tril = np.tril(np.ones((TM, TM), np.float32))
    return jnp.asarray(tril, dtype=BF16), jnp.asarray(pqt, dtype=BF16), jnp.asarray(pk, dtype=BF16)


def _shared_kernel(h_ref, g_ref, w_ref, wvt_ref, bf_ref, tril_ref, pqt_ref, pk_ref,
                   k_ref, vt_ref, aqt_ref, ak_ref, carry_ref):
    @pl.when(pl.program_id(1) == 0)
    def _():
        carry_ref[...] = jnp.zeros_like(carry_ref)

    xn = _rmsnorm(h_ref[...], g_ref[...]).astype(BF16)
    k_ref[...] = _dot(xn, w_ref[:, 0:MAIN_W]).astype(BF16)
    vt_ref[0, 0] = _dot_nt(wvt_ref[...], xn).astype(BF16)
    log_f = _log_sigmoid(_dot(xn, w_ref[:, MAIN_W:SH_PW]) + bf_ref[...])
    d = _exact_sum3(_dot(tril_ref[...], _split3(log_f)), LANES) + carry_ref[...]
    carry_ref[...] = d[TM - 1:TM, :]
    d3 = _split3(d * LOG2E)
    lane = lax.broadcasted_iota(jnp.int32, d3.shape, 1)
    d3 = jnp.where(lane == ONE_LANE, jnp.ones_like(d3), d3)
    aqt_ref[0, 0] = _dot_nt(pqt_ref[...], d3).astype(BF16)
    ak_ref[...] = _dot(d3, pk_ref[...]).astype(BF16)


def _shared_kv(h, g, w, wvt, bf, batch, seq):
    n = h.shape[0]
    nt = seq // TM
    tril, pqt, pk = _fox_aux_consts()
    row = lambda b, t: (b * nt + t, 0)
    fixed = lambda b, t: (0, 0)
    out = jax.ShapeDtypeStruct((n, MAIN_W), BF16)
    out_t = jax.ShapeDtypeStruct((batch, nt, MAIN_W, TM), BF16)
    row_spec = pl.BlockSpec((TM, MAIN_W), row)
    slab_spec = pl.BlockSpec((1, 1, MAIN_W, TM), lambda b, t: (b, t, 0, 0))
    return pl.pallas_call(
        _shared_kernel,
        out_shape=(out, out_t, out_t, out),
        grid=(batch, nt),
        in_specs=[
            pl.BlockSpec((TM, D_MODEL), row),
            pl.BlockSpec((1, D_MODEL), fixed),
            pl.BlockSpec((D_MODEL, SH_PW), fixed),
            pl.BlockSpec((MAIN_W, D_MODEL), fixed),
            pl.BlockSpec((1, LANES), fixed),
            pl.BlockSpec((TM, TM), fixed),
            pl.BlockSpec((MAIN_W, 3 * LANES), fixed),
            pl.BlockSpec((3 * LANES, MAIN_W), fixed),
        ],
        out_specs=(row_spec, slab_spec, slab_spec, row_spec),
        scratch_shapes=[pltpu.VMEM((1, LANES), F32)],
        compiler_params=_cparams(("arbitrary", "arbitrary")),
        name="shared_kv",
    )(h, g, w, wvt, bf, tril, pqt, pk)


def _fox_proj_kernel(h_ref, g_ref, wqt_ref, wm_ref, qt_ref, qm_ref):
    xn = _rmsnorm(h_ref[...], g_ref[...]).astype(BF16)
    qt_ref[0] = (_dot_nt(wqt_ref[...], xn) * (FOX_DH ** -0.5 * LOG2E)).astype(BF16)
    qm_ref[...] = _dot(xn, wm_ref[...]).astype(BF16)


def _fox_proj(h, g, wqt, wm):
    n = h.shape[0]
    row = lambda i: (i, 0)
    fixed = lambda i: (0, 0)
    return pl.pallas_call(
        _fox_proj_kernel,
        out_shape=(jax.ShapeDtypeStruct((n // TM, MAIN_W, TM), BF16), jax.ShapeDtypeStruct((n, MEM_W), BF16)),
        grid=(n // TM,),
        in_specs=[
            pl.BlockSpec((TM, D_MODEL), row),
            pl.BlockSpec((1, D_MODEL), fixed),
            pl.BlockSpec((MAIN_W, D_MODEL), fixed),
            pl.BlockSpec((D_MODEL, MEM_W), fixed),
        ],
        out_specs=(pl.BlockSpec((1, MAIN_W, TM), lambda i: (i, 0, 0)), pl.BlockSpec((TM, MEM_W), row)),
        compiler_params=_cparams(("parallel",)),
        name="fox_proj",
    )(h, g, wqt, wm)


FOX_HPS = 3


FOX_SUB = MXU_N
FOX_NSUB = FOX_BK // FOX_SUB


def _fox_attn_kernel(qt_ref, aqt_ref, k_ref, ak_ref, vt_ref, o_ref):
    qi = pl.program_id(2)
    chains = [(hh, qs) for hh in range(FOX_HPS) for qs in range(FOX_BQ // FOX_SUB)]

    def head(hh):
        return slice(hh * FOX_DH, (hh + 1) * FOX_DH)

    def strip(s):
        return slice(s * FOX_SUB, (s + 1) * FOX_SUB)

    qq = [jnp.concatenate([qt_ref[0, 0, head(hh), strip(qs)], aqt_ref[0, 0, head(hh), strip(qs)]], axis=0)
          for hh, qs in chains]

    def step(kj, carry, diagonal):
        c0 = pl.multiple_of(kj * FOX_BK, FOX_BK)
        sts = {}
        for ks in range(FOX_NSUB):
            rows = pl.ds(c0 + ks * FOX_SUB, FOX_SUB)
            for ci, (hh, qs) in enumerate(chains):
                if diagonal and ks > qs:
                    continue
                kk = jnp.concatenate([k_ref[0, rows, head(hh)], ak_ref[0, rows, head(hh)]], axis=1)
                st = _dot(kk, qq[ci])
                if diagonal and ks == qs:
                    key = lax.broadcasted_iota(jnp.int32, st.shape, 0)
                    qry = lax.broadcasted_iota(jnp.int32, st.shape, 1)
                    st = jnp.where(key <= qry, st, -jnp.inf)
                sts[ks, ci] = st
        carry = list(carry)
        for ks in range(FOX_NSUB):
            for ci, (hh, qs) in enumerate(chains):
                if (ks, ci) not in sts:
                    continue
                m, acc = carry[ci]
                st = sts[ks, ci]
                m_new = jnp.maximum(m, jnp.max(st, axis=0, keepdims=True))
                alpha = jnp.exp2(m - m_new)
                p = jnp.exp2(st - m_new).astype(BF16)
                vt1 = jnp.concatenate([vt_ref[0, kj, head(hh), strip(ks)], ones], axis=0)
                acc = alpha * acc + _dot(vt1, p)
                carry[ci] = (m_new, acc)
        return tuple(carry)

    ones = jnp.ones((BF16_ROWS, FOX_SUB), BF16)
    init = tuple((jnp.full((1, FOX_SUB), -jnp.inf, F32), jnp.zeros((FOX_DH + BF16_ROWS, FOX_SUB), F32))
                 for _ in chains)
    carry = lax.fori_loop(0, qi, lambda kj, cr: step(kj, cr, False), init)
    carry = step(qi, carry, True)
    for ci, (hh, qs) in enumerate(chains):
        _, acc = carry[ci]
        o_ref[0, strip(qs), head(hh)] = (acc[:FOX_DH] / acc[FOX_DH:FOX_DH + 1]).T.astype(BF16)


def _fox_attn(qt, aqt, k, ak, vt, batch, seq):
    w = FOX_HPS * FOX_DH
    nq = seq // FOX_BQ
    qblk = pl.BlockSpec((1, 1, w, FOX_BQ), lambda b, g, i: (b, i, g, 0))
    kblk = pl.BlockSpec((1, seq, w), lambda b, g, i: (b, 0, g))
    vblk = pl.BlockSpec((1, seq // FOX_BK, w, FOX_BK), lambda b, g, i: (b, 0, g, 0))
    out = pl.pallas_call(
        _fox_attn_kernel,
        out_shape=jax.ShapeDtypeStruct((batch, seq, MAIN_W), BF16),
        grid=(batch, FOX_HEADS // FOX_HPS, nq),
        in_specs=[qblk, qblk, kblk, kblk, vblk],
        out_specs=pl.BlockSpec((1, FOX_BQ, w), lambda b, g, i: (b, i, g)),
        compiler_params=_cparams(("parallel", "parallel", "arbitrary")),
        name="fox_attn",
    )(qt.reshape(batch, nq, MAIN_W, FOX_BQ), aqt, k.reshape(batch, seq, MAIN_W), ak.reshape(batch, seq, MAIN_W), vt)
    return out.reshape(batch * seq, MAIN_W)


def _final_norm_kernel(h_ref, g_ref, o_ref):
    o_ref[...] = _rmsnorm(h_ref[...], g_ref[...])


def _final_norm(h, g):
    n = h.shape[0]
    return pl.pallas_call(
        _final_norm_kernel,
        out_shape=jax.ShapeDtypeStruct((n, D_MODEL), F32),
        grid=(n // TM,),
        in_specs=[pl.BlockSpec((TM, D_MODEL), lambda i: (i, 0)), pl.BlockSpec((1, D_MODEL), lambda i: (0, 0))],
        out_specs=pl.BlockSpec((TM, D_MODEL), lambda i: (i, 0)),
        compiler_params=_cparams(("parallel",)),
        name="final_norm",
    )(h, g)


def _pad_heads_cols(w, heads, width, padded):
    lead = w.shape[:-1]
    w = w.reshape(lead + (heads, width))
    w = jnp.pad(w, [(0, 0)] * len(lead) + [(0, 0), (0, padded - width)])
    return w.reshape(lead + (heads * padded,))


def _pad_heads_rows(w, heads, width, padded):
    n = w.shape[-1]
    w = w.reshape(heads, width, n)
    w = jnp.pad(w, [(0, 0), (0, padded - width), (0, 0)])
    return w.reshape(heads * padded, n)


def _gla_weights(w_in, w_gate_up, b_gate, g_head, w_out):
    dk = GLA_HEADS * GLA_DK_HEAD
    dv = GLA_HEADS * GLA_DV_HEAD
    o1, o2, o3, o4, o5 = dk, 2 * dk, 2 * dk + dv, 2 * dk + 2 * dv, 2 * dk + 2 * dv + GLA_RANK
    w = jnp.concatenate([
        _pad_heads_cols(w_in[:, :o1], GLA_HEADS, GLA_DK_HEAD, DKP),
        _pad_heads_cols(w_in[:, o1:o2], GLA_HEADS, GLA_DK_HEAD, DKP),
        _pad_heads_cols(w_in[:, o2:o3], GLA_HEADS, GLA_DV_HEAD, DVP),
        _pad_heads_cols(w_in[:, o3:o4], GLA_HEADS, GLA_DV_HEAD, DVP),
        w_in[:, o5:],
        jnp.pad(w_in[:, o4:o5], [(0, 0), (0, LANES - GLA_RANK)]),
    ], axis=1).astype(BF16)
    wup = jnp.pad(_pad_heads_cols(w_gate_up, GLA_HEADS, GLA_DK_HEAD, DKP), [(0, LANES - GLA_RANK), (0, 0)]).astype(BF16)
    bg = _pad_heads_cols(b_gate[None, :], GLA_HEADS, GLA_DK_HEAD, DKP)
    gh = jnp.pad(g_head[None, :], [(0, 0), (0, DVP - GLA_DV_HEAD)])
    wo = jnp.concatenate([_pad_heads_rows(w_out[:MAIN_W], GLA_HEADS, GLA_DV_HEAD, DVP), w_out[MAIN_W:]], axis=0).astype(BF16)
    return w, wup, bg, gh, wo


def kernel(x, mem, norm_ffn, w_ffn_in, w_ffn_out, norm_mix, norm_mem, w_mem_kv, w_out, w_in_a, w_gate_up, b_gate,
           norm_gla_head, w_in_b, norm_shared, w_kv_shared, w_fgate, b_fgate, norm_final):
    batch, seq, _ = x.shape
    h = x.reshape(batch * seq, D_MODEL)

    w1 = w_ffn_in.astype(BF16)
    w2 = w_ffn_out.astype(BF16)
    wkt = jnp.swapaxes(w_mem_kv[:, :, :MEM_W], 1, 2).astype(BF16)
    wv = w_mem_kv[:, :, MEM_W:].astype(BF16)
    kb, vb = _mem_kv(mem, norm_mem[:, None, :], wkt, wv)

    k_sh = v_sh = aq_sh = ak_sh = None
    for l in range(DEPTH):
        if l == N_A:
            w_sh = jnp.concatenate([w_kv_shared[:, :MAIN_W], jnp.pad(w_fgate, [(0, 0), (0, LANES - FOX_HEADS)])],
                                   axis=1).astype(BF16)
            wvt = w_kv_shared[:, MAIN_W:].T.astype(BF16)
            bf = jnp.pad(b_fgate[None, :], [(0, 0), (0, LANES - FOX_HEADS)])
            k_sh, v_sh, aq_sh, ak_sh = _shared_kv(h, norm_shared[None, :], w_sh, wvt, bf, batch, seq)
        h = _ffn(h, norm_ffn[l, 0][None, :], w1[l, 0], w2[l, 0])
        if l < N_A:
            w, wup, bg, gh, wo = _gla_weights(w_in_a[l], w_gate_up[l], b_gate[l], norm_gla_head[l], w_out[l])
            q, k, v, r, la, q_mem = _gla_proj(h, norm_mix[l][None, :], w, wup, bg)
            o_main = _gla(q, k, la, v, r, gh, batch, seq)
        else:
            wb = w_in_b[l - N_A]
            qt, q_mem = _fox_proj(h, norm_mix[l][None, :], wb[:, :MAIN_W].T.astype(BF16), wb[:, MAIN_W:].astype(BF16))
            o_main = _fox_attn(qt, aq_sh, k_sh, ak_sh, v_sh, batch, seq)
            wo = w_out[l].astype(BF16)
        h = _mix_out(h, o_main, q_mem, kb, vb, wo, l, seq)
        h = _ffn(h, norm_ffn[l, 1][None, :], w1[l, 1], w2[l, 1])
    return _final_norm(h, norm_final[None, :]).reshape(batch, seq, D_MODEL)
```

```python
import functools

import numpy as np
import jax
import jax.numpy as jnp
from jax import lax
from jax.experimental import pallas as pl
from jax.experimental.pallas import tpu as pltpu

F32 = jnp.float32
BF16 = jnp.bfloat16

D_MODEL = 1024
DEPTH = 4
N_A = DEPTH // 2
MAIN_W = 768
MEM_W = 256
GLA_HEADS = 4
GLA_DK_HEAD = 96
GLA_DV_HEAD = 192
GLA_RANK = 16
GLA_TEMP = 16.0
FOX_HEADS = 6
FOX_DH = 128
MEM_HEADS = 4
MEM_DH = 64
MEM_LEN = 256
D_FF = 2816
EPS = 1e-6
LOG2E = 1.4426950408889634

LANES = 128
MXU_N = 256
BF16_ROWS = 16
VMEM_LIMIT = 56 * 1024 * 1024

DKP = LANES
DVP = MXU_N
GLA_QW = GLA_HEADS * DKP
GLA_VW = GLA_HEADS * DVP
GLA_CHUNK = MXU_N
GLA_LEVELS = 8

TM = 512
FOX_BQ = TM
FOX_BK = TM


def _cparams(sem):
    return pltpu.CompilerParams(dimension_semantics=sem, vmem_limit_bytes=VMEM_LIMIT)


def _rmsnorm(x, g):
    return x * lax.rsqrt(jnp.mean(x * x, axis=-1, keepdims=True) + EPS) * g


def _log_sigmoid(x):
    return jnp.minimum(x, 0.0) - jnp.log(1.0 + jnp.exp(-jnp.abs(x)))


def _sigmoid(x):
    return 1.0 / (1.0 + jnp.exp(-x))


def _split3(x):
    hi = x.astype(BF16)
    r1 = x - hi.astype(F32)
    mid = r1.astype(BF16)
    lo = (r1 - mid.astype(F32)).astype(BF16)
    return jnp.concatenate([hi, mid, lo], axis=1)


def _split2(x):
    hi = x.astype(BF16)
    lo = (x - hi.astype(F32)).astype(BF16)
    return jnp.concatenate([hi, lo], axis=1)


def _dot(a, b):
    return jnp.dot(a, b, preferred_element_type=F32)


def _dot_nt(a, b):
    return lax.dot_general(a, b, (((1,), (1,)), ((), ())), preferred_element_type=F32)


def _dot_tn(a, b):
    return lax.dot_general(a, b, (((0,), (0,)), ((), ())), preferred_element_type=F32)


def _exact_sum3(r, w):
    return r[:, :w] + r[:, w:2 * w] + r[:, 2 * w:]


FFN_BOUNDS = (0, 6 * MXU_N, D_FF)
FFN_TM = 1024


def _ffn_kernel(h_ref, g_ref, w1_ref, w2_ref, *rest):
    o_ref = rest[-1]
    h = h_ref[...]
    xn = _rmsnorm(h, g_ref[0, 0]).astype(BF16)
    y = None
    for f0, f1 in zip(FFN_BOUNDS[:-1], FFN_BOUNDS[1:]):
        a = _dot(xn, w1_ref[0, 0, :, f0:f1])
        c = _dot(xn, w1_ref[0, 0, :, D_FF + f0:D_FF + f1])
        act = (a * _sigmoid(a) * c).astype(BF16)
        part = _dot(act, w2_ref[0, 0, f0:f1, :])
        y = part if y is None else y + part
    out = h + 0.5 * y
    o_ref[...] = _rmsnorm(out, rest[0][...]) if len(rest) == 2 else out


def _ffn(h, g_all, w1_all, w2_all, layer, half, g_final=None):
    n = h.shape[0]
    pick = lambda i: (layer, half, 0, 0)
    in_specs = [
        pl.BlockSpec((FFN_TM, D_MODEL), lambda i: (i, 0)),
        pl.BlockSpec((1, 1, 1, D_MODEL), pick),
        pl.BlockSpec((1, 1, D_MODEL, 2 * D_FF), pick, pipeline_mode=pl.Buffered(1)),
        pl.BlockSpec((1, 1, D_FF, D_MODEL), pick, pipeline_mode=pl.Buffered(1)),
    ]
    args = [h, g_all, w1_all, w2_all]
    if g_final is not None:
        in_specs.append(pl.BlockSpec((1, D_MODEL), lambda i: (0, 0)))
        args.append(g_final)
    return pl.pallas_call(
        _ffn_kernel,
        out_shape=jax.ShapeDtypeStruct((n, D_MODEL), F32),
        grid=(n // FFN_TM,),
        in_specs=in_specs,
        out_specs=pl.BlockSpec((FFN_TM, D_MODEL), lambda i: (i, 0)),
        compiler_params=_cparams(("parallel",)),
        name="ffn",
    )(*args)


GLA_PW = 2 * GLA_QW + 2 * GLA_VW + MEM_W + LANES


def _gla_proj_kernel(h_ref, g_ref, w_ref, wup_ref, bg_ref, q_ref, k_ref, v_ref, r_ref, la_ref, qm_ref):
    xn = _rmsnorm(h_ref[...], g_ref[...]).astype(BF16)
    o = 0
    q_ref[...] = (_dot(xn, w_ref[:, o:o + GLA_QW]) * (GLA_DK_HEAD ** -0.5)).astype(BF16)
    o += GLA_QW
    k_ref[...] = _dot(xn, w_ref[:, o:o + GLA_QW]).astype(BF16)
    o += GLA_QW
    v_ref[...] = _dot(xn, w_ref[:, o:o + GLA_VW]).astype(BF16)
    o += GLA_VW
    r_ref[...] = _dot(xn, w_ref[:, o:o + GLA_VW]).astype(BF16)
    o += GLA_VW
    qm_ref[...] = _dot(xn, w_ref[:, o:o + MEM_W]).astype(BF16)
    o += MEM_W
    g_low = _dot(xn, w_ref[:, o:o + LANES]).astype(BF16)
    x = _dot(g_low, wup_ref[...]) + bg_ref[...]
    la_ref[...] = _log_sigmoid(x) * (LOG2E / GLA_TEMP)


def _gla_proj(h, g, w, wup, bg):
    n = h.shape[0]
    row = lambda i: (i, 0)
    fixed = lambda i: (0, 0)
    return pl.pallas_call(
        _gla_proj_kernel,
        out_shape=(
            jax.ShapeDtypeStruct((n, GLA_QW), BF16),
            jax.ShapeDtypeStruct((n, GLA_QW), BF16),
            jax.ShapeDtypeStruct((n, GLA_VW), BF16),
            jax.ShapeDtypeStruct((n, GLA_VW), BF16),
            jax.ShapeDtypeStruct((n, GLA_QW), F32),
            jax.ShapeDtypeStruct((n, MEM_W), BF16),
        ),
        grid=(n // TM,),
        in_specs=[
            pl.BlockSpec((TM, D_MODEL), row),
            pl.BlockSpec((1, D_MODEL), fixed),
            pl.BlockSpec((D_MODEL, GLA_PW), fixed, pipeline_mode=pl.Buffered(1)),
            pl.BlockSpec((LANES, GLA_QW), fixed),
            pl.BlockSpec((1, GLA_QW), fixed),
        ],
        out_specs=(
            pl.BlockSpec((TM, GLA_QW), row),
            pl.BlockSpec((TM, GLA_QW), row),
            pl.BlockSpec((TM, GLA_VW), row),
            pl.BlockSpec((TM, GLA_VW), row),
            pl.BlockSpec((TM, GLA_QW), row),
            pl.BlockSpec((TM, MEM_W), row),
        ),
        compiler_params=_cparams(("parallel",)),
        name="gla_proj",
    )(h, g, w, wup, bg)


GLA_TS = 512


GLA_UNROLL = 2


def _gla_consts():
    c = GLA_CHUNK
    i = np.arange(c)[:, None]
    j = np.arange(c)[None, :]
    tril = (j <= i)
    b_rows, masks = [], []
    for lv in range(GLA_LEVELS):
        hs = 1 << lv
        mid = (i // (2 * hs)) * (2 * hs) + hs
        lower = (i % (2 * hs)) >= hs
        b_rows.append(np.where(lower, (j >= mid) & (j <= i), (j > i) & (j < mid)))
        masks.append(((i // (2 * hs)) == (j // (2 * hs))) & lower & ((j % (2 * hs)) < hs))
    masks.append(i == j)
    asb = lambda x: jnp.asarray(x.astype(np.float32), dtype=BF16)
    return asb(tril), asb(np.concatenate(b_rows, axis=0)), asb(np.stack(masks, axis=0))


def _gla_kernel(q_ref, k_ref, la_ref, v_ref, r_ref, gh_ref, tril_ref, b_ref, m_ref, o_ref, st_ref):
    c = GLA_CHUNK

    @pl.when(pl.program_id(1) == 0)
    def _():
        st_ref[...] = jnp.zeros_like(st_ref)

    gh = gh_ref[...]

    def chunk(ci, carry):
        r0 = pl.multiple_of(ci * c, c)
        g = la_ref[pl.ds(r0, c), :]
        cum = _dot(tril_ref[...], _split2(g))
        cum = cum[:, :GLA_QW] + cum[:, GLA_QW:]
        lev = _dot(b_ref[...], g.astype(BF16))
        qa = q_ref[pl.ds(r0, c), :]
        ka = k_ref[pl.ds(r0, c), :]
        v = v_ref[pl.ds(r0, c), :]
        r = r_ref[pl.ds(r0, c), :].astype(F32)
        for h in range(GLA_HEADS):
            ks = slice(h * DKP, (h + 1) * DKP)
            vs = slice(h * DVP, (h + 1) * DVP)
            qb, kb, vh = qa[:, ks], ka[:, ks], v[:, vs]
            cum_h = cum[:, ks]
            last_h = cum_h[c - 1:c, :]
            st = st_ref[h]
            s = _dot_nt(qb, kb).astype(BF16) * m_ref[GLA_LEVELS]
            for lv in range(GLA_LEVELS):
                e = jnp.exp2(lev[lv * c:(lv + 1) * c, ks]).astype(BF16)
                s = s + _dot_nt(qb * e, kb * e).astype(BF16) * m_ref[lv]
            qg = (qb.astype(F32) * jnp.exp2(cum_h)).astype(BF16)
            o = _dot_nt(qg, st.astype(BF16)) + _dot(s, vh)
            kd = (kb.astype(F32) * jnp.exp2(last_h - cum_h)).astype(BF16)
            st_ref[h] = st * jnp.exp2(last_h) + _dot_tn(vh, kd)
            ms = jnp.sum(o * o, axis=-1, keepdims=True) * (1.0 / GLA_DV_HEAD)
            rh = r[:, vs]
            y = o * lax.rsqrt(ms + EPS) * gh * (rh * _sigmoid(rh))
            o_ref[pl.ds(r0, c), vs] = y.astype(BF16)
        return carry

    lax.fori_loop(0, GLA_TS // c, chunk, 0, unroll=GLA_UNROLL)


def _gla(q, k, la, v, r, gh, batch, seq):
    n = q.shape[0]
    nt = seq // GLA_TS
    tril, bmat, m = _gla_consts()
    row = lambda b, t: (b * nt + t, 0)
    fixed2 = lambda b, t: (0, 0)
    fixed3 = lambda b, t: (0, 0, 0)
    return pl.pallas_call(
        _gla_kernel,
        out_shape=jax.ShapeDtypeStruct((n, GLA_VW), BF16),
        grid=(batch, nt),
        in_specs=[
            pl.BlockSpec((GLA_TS, GLA_QW), row),
            pl.BlockSpec((GLA_TS, GLA_QW), row),
            pl.BlockSpec((GLA_TS, GLA_QW), row),
            pl.BlockSpec((GLA_TS, GLA_VW), row),
            pl.BlockSpec((GLA_TS, GLA_VW), row),
            pl.BlockSpec((1, DVP), fixed2),
            pl.BlockSpec(tril.shape, fixed2),
            pl.BlockSpec(bmat.shape, fixed2),
            pl.BlockSpec(m.shape, fixed3),
        ],
        out_specs=pl.BlockSpec((GLA_TS, GLA_VW), row),
        scratch_shapes=[pltpu.VMEM((GLA_HEADS, DVP, DKP), F32)],
        compiler_params=_cparams(("arbitrary", "arbitrary")),
        name="gla",
    )(q, k, la, v, r, gh, tril, bmat, m)


MEM_BW = MEM_HEADS * MEM_LEN


def _mem_kv_kernel(mem_ref, g_ref, wkt_ref, wv_ref, kb_ref, vb_ref):
    mn = _rmsnorm(mem_ref[0], g_ref[0]).astype(BF16)
    kt = _dot_nt(wkt_ref[0], mn) * (MEM_DH ** -0.5)
    v = _dot(mn, wv_ref[0])
    d_row = lax.broadcasted_iota(jnp.int32, (MEM_W, MEM_LEN), 0) // MEM_DH
    d_col = lax.broadcasted_iota(jnp.int32, (MEM_LEN, MEM_W), 1) // MEM_DH
    for hh in range(MEM_HEADS):
        kb_ref[0, 0, :, hh * MEM_LEN:(hh + 1) * MEM_LEN] = jnp.where(d_row == hh, kt, 0.0).astype(BF16)
        vb_ref[0, 0, hh * MEM_LEN:(hh + 1) * MEM_LEN, :] = jnp.where(d_col == hh, v, 0.0).astype(BF16)


def _mem_kv(mem, g, wkt, wv):
    b = mem.shape[0]
    return pl.pallas_call(
        _mem_kv_kernel,
        out_shape=(
            jax.ShapeDtypeStruct((DEPTH, b, MEM_W, MEM_BW), BF16),
            jax.ShapeDtypeStruct((DEPTH, b, MEM_BW, MEM_W), BF16),
        ),
        grid=(DEPTH, b),
        in_specs=[
            pl.BlockSpec((1, MEM_LEN, D_MODEL), lambda l, i: (i, 0, 0)),
            pl.BlockSpec((1, 1, D_MODEL), lambda l, i: (l, 0, 0)),
            pl.BlockSpec((1, MEM_W, D_MODEL), lambda l, i: (l, 0, 0)),
            pl.BlockSpec((1, D_MODEL, MEM_W), lambda l, i: (l, 0, 0)),
        ],
        out_specs=(
            pl.BlockSpec((1, 1, MEM_W, MEM_BW), lambda l, i: (l, i, 0, 0)),
            pl.BlockSpec((1, 1, MEM_BW, MEM_W), lambda l, i: (l, i, 0, 0)),
        ),
        compiler_params=_cparams(("parallel", "parallel")),
        name="mem_kv",
    )(mem, g, wkt, wv)


def _mix_out_kernel(h_ref, om_ref, qm_ref, kb_ref, vb_ref, wo_ref, o_ref):
    lg = _dot(qm_ref[...], kb_ref[0, 0])
    ps = []
    for hh in range(MEM_HEADS):
        s = lg[:, hh * MEM_LEN:(hh + 1) * MEM_LEN]
        e = jnp.exp(s - jnp.max(s, axis=-1, keepdims=True))
        ps.append((e / jnp.sum(e, axis=-1, keepdims=True)).astype(BF16))
    o_mem = _dot(jnp.concatenate(ps, axis=1), vb_ref[0, 0])
    x = jnp.concatenate([om_ref[...], o_mem.astype(BF16)], axis=1)
    o_ref[...] = h_ref[...] + _dot(x, wo_ref[...])


def _mix_out(h, o_main, q_mem, kb, vb, wo, layer, seq):
    n = h.shape[0]
    wm = o_main.shape[1]
    per_b = seq // TM
    row = lambda i: (i, 0)
    return pl.pallas_call(
        _mix_out_kernel,
        out_shape=jax.ShapeDtypeStruct((n, D_MODEL), F32),
        grid=(n // TM,),
        in_specs=[
            pl.BlockSpec((TM, D_MODEL), row),
            pl.BlockSpec((TM, wm), row),
            pl.BlockSpec((TM, MEM_W), row),
            pl.BlockSpec((1, 1, MEM_W, MEM_BW), lambda i: (layer, i // per_b, 0, 0)),
            pl.BlockSpec((1, 1, MEM_BW, MEM_W), lambda i: (layer, i // per_b, 0, 0)),
            pl.BlockSpec((wm + MEM_W, D_MODEL), lambda i: (0, 0)),
        ],
        out_specs=pl.BlockSpec((TM, D_MODEL), row),
        compiler_params=_cparams(("parallel",)),
        name="mix_out",
    )(h, o_main, q_mem, kb, vb, wo)


SH_PW = MAIN_W + LANES


ONE_LANE = LANES - 1


def _fox_aux_consts():
    pqt = np.zeros((MAIN_W, 3 * LANES), np.float32)
    pk = np.zeros((3 * LANES, MAIN_W), np.float32)
    for h in range(FOX_HEADS):
        for p in range(3):
            pqt[h * FOX_DH + p, p * LANES + h] = 1.0
            pqt[h * FOX_DH + 3 + p, ONE_LANE] = 1.0
            pk[ONE_LANE, h * FOX_DH + p] = 1.0
            pk[p * LANES + h, h * FOX_DH + 3 + p] = -1.0
    tril = np.tril(np.ones((TM, TM), np.float32))
    return jnp.asarray(tril, dtype=BF16), jnp.asarray(pqt, dtype=BF16), jnp.asarray(pk, dtype=BF16)


def _shared_kernel(h_ref, g_ref, w_ref, wvt_ref, bf_ref, tril_ref, pqt_ref, pk_ref,
                   k_ref, vt_ref, aqt_ref, ak_ref, carry_ref):
    @pl.when(pl.program_id(1) == 0)
    def _():
        carry_ref[...] = jnp.zeros_like(carry_ref)

    xn = _rmsnorm(h_ref[...], g_ref[...]).astype(BF16)
    k_ref[...] = _dot(xn, w_ref[:, 0:MAIN_W]).astype(BF16)
    vt_ref[0, 0] = _dot_nt(wvt_ref[...], xn).astype(BF16)
    log_f = _log_sigmoid(_dot(xn, w_ref[:, MAIN_W:SH_PW]) + bf_ref[...])
    d = _exact_sum3(_dot(tril_ref[...], _split3(log_f)), LANES) + carry_ref[...]
    carry_ref[...] = d[TM - 1:TM, :]
    d3 = _split3(d * LOG2E)
    lane = lax.broadcasted_iota(jnp.int32, d3.shape, 1)
    d3 = jnp.where(lane == ONE_LANE, jnp.ones_like(d3), d3)
    aqt_ref[0, 0] = _dot_nt(pqt_ref[...], d3).astype(BF16)
    ak_ref[...] = _dot(d3, pk_ref[...]).astype(BF16)


def _shared_kv(h, g, w, wvt, bf, batch, seq):
    n = h.shape[0]
    nt = seq // TM
    tril, pqt, pk = _fox_aux_consts()
    row = lambda b, t: (b * nt + t, 0)
    fixed = lambda b, t: (0, 0)
    out = jax.ShapeDtypeStruct((n, MAIN_W), BF16)
    out_t = jax.ShapeDtypeStruct((batch, nt, MAIN_W, TM), BF16)
    row_spec = pl.BlockSpec((TM, MAIN_W), row)
    slab_spec = pl.BlockSpec((1, 1, MAIN_W, TM), lambda b, t: (b, t, 0, 0))
    return pl.pallas_call(
        _shared_kernel,
        out_shape=(out, out_t, out_t, out),
        grid=(batch, nt),
        in_specs=[
            pl.BlockSpec((TM, D_MODEL), row),
            pl.BlockSpec((1, D_MODEL), fixed),
            pl.BlockSpec((D_MODEL, SH_PW), fixed),
            pl.BlockSpec((MAIN_W, D_MODEL), fixed),
            pl.BlockSpec((1, LANES), fixed),
            pl.BlockSpec((TM, TM), fixed),
            pl.BlockSpec((MAIN_W, 3 * LANES), fixed),
            pl.BlockSpec((3 * LANES, MAIN_W), fixed),
        ],
        out_specs=(row_spec, slab_spec, slab_spec, row_spec),
        scratch_shapes=[pltpu.VMEM((1, LANES), F32)],
        compiler_params=_cparams(("arbitrary", "arbitrary")),
        name="shared_kv",
    )(h, g, w, wvt, bf, tril, pqt, pk)


def _fox_proj_kernel(h_ref, g_ref, wqt_ref, wm_ref, qt_ref, qm_ref):
    xn = _rmsnorm(h_ref[...], g_ref[...]).astype(BF16)
    qt_ref[0] = (_dot_nt(wqt_ref[...], xn) * (FOX_DH ** -0.5 * LOG2E)).astype(BF16)
    qm_ref[...] = _dot(xn, wm_ref[...]).astype(BF16)


def _fox_proj(h, g, wqt, wm):
    n = h.shape[0]
    row = lambda i: (i, 0)
    fixed = lambda i: (0, 0)
    return pl.pallas_call(
        _fox_proj_kernel,
        out_shape=(jax.ShapeDtypeStruct((n // TM, MAIN_W, TM), BF16), jax.ShapeDtypeStruct((n, MEM_W), BF16)),
        grid=(n // TM,),
        in_specs=[
            pl.BlockSpec((TM, D_MODEL), row),
            pl.BlockSpec((1, D_MODEL), fixed),
            pl.BlockSpec((MAIN_W, D_MODEL), fixed),
            pl.BlockSpec((D_MODEL, MEM_W), fixed),
        ],
        out_specs=(pl.BlockSpec((1, MAIN_W, TM), lambda i: (i, 0, 0)), pl.BlockSpec((TM, MEM_W), row)),
        compiler_params=_cparams(("parallel",)),
        name="fox_proj",
    )(h, g, wqt, wm)


FOX_HPS = 3


FOX_SUB = MXU_N
FOX_NSUB = FOX_BK // FOX_SUB


def _fox_attn_kernel(qt_ref, aqt_ref, k_ref, ak_ref, vt_ref, o_ref):
    qi = pl.program_id(2)
    chains = [(hh, qs) for hh in range(FOX_HPS) for qs in range(FOX_BQ // FOX_SUB)]

    def head(hh):
        return slice(hh * FOX_DH, (hh + 1) * FOX_DH)

    def strip(s):
        return slice(s * FOX_SUB, (s + 1) * FOX_SUB)

    qq = [jnp.concatenate([qt_ref[0, 0, head(hh), strip(qs)], aqt_ref[0, 0, head(hh), strip(qs)]], axis=0)
          for hh, qs in chains]

    def step(kj, carry, diagonal):
        c0 = pl.multiple_of(kj * FOX_BK, FOX_BK)
        sts = {}
        for ks in range(FOX_NSUB):
            rows = pl.ds(c0 + ks * FOX_SUB, FOX_SUB)
            for ci, (hh, qs) in enumerate(chains):
                if diagonal and ks > qs:
                    continue
                kk = jnp.concatenate([k_ref[0, rows, head(hh)], ak_ref[0, rows, head(hh)]], axis=1)
                st = _dot(kk, qq[ci])
                if diagonal and ks == qs:
                    key = lax.broadcasted_iota(jnp.int32, st.shape, 0)
                    qry = lax.broadcasted_iota(jnp.int32, st.shape, 1)
                    st = jnp.where(key <= qry, st, -jnp.inf)
                sts[ks, ci] = st
        carry = list(carry)
        for ks in range(FOX_NSUB):
            for ci, (hh, qs) in enumerate(chains):
                if (ks, ci) not in sts:
                    continue
                m, acc = carry[ci]
                st = sts[ks, ci]
                m_new = jnp.maximum(m, jnp.max(st, axis=0, keepdims=True))
                alpha = jnp.exp2(m - m_new)
                p = jnp.exp2(st - m_new).astype(BF16)
                vt1 = jnp.concatenate([vt_ref[0, kj, head(hh), strip(ks)], ones], axis=0)
                acc = alpha * acc + _dot(vt1, p)
                carry[ci] = (m_new, acc)
        return tuple(carry)

    ones = jnp.ones((BF16_ROWS, FOX_SUB), BF16)
    init = tuple((jnp.full((1, FOX_SUB), -jnp.inf, F32), jnp.zeros((FOX_DH + BF16_ROWS, FOX_SUB), F32))
                 for _ in chains)
    carry = lax.fori_loop(0, qi, lambda kj, cr: step(kj, cr, False), init)
    carry = step(qi, carry, True)
    for ci, (hh, qs) in enumerate(chains):
        _, acc = carry[ci]
        o_ref[0, strip(qs), head(hh)] = (acc[:FOX_DH] / acc[FOX_DH:FOX_DH + 1]).T.astype(BF16)


def _fox_attn(qt, aqt, k, ak, vt, batch, seq):
    w = FOX_HPS * FOX_DH
    nq = seq // FOX_BQ
    qblk = pl.BlockSpec((1, 1, w, FOX_BQ), lambda b, g, i: (b, i, g, 0))
    kblk = pl.BlockSpec((1, seq, w), lambda b, g, i: (b, 0, g))
    vblk = pl.BlockSpec((1, seq // FOX_BK, w, FOX_BK), lambda b, g, i: (b, 0, g, 0))
    out = pl.pallas_call(
        _fox_attn_kernel,
        out_shape=jax.ShapeDtypeStruct((batch, seq, MAIN_W), BF16),
        grid=(batch, FOX_HEADS // FOX_HPS, nq),
        in_specs=[qblk, qblk, kblk, kblk, vblk],
        out_specs=pl.BlockSpec((1, FOX_BQ, w), lambda b, g, i: (b, i, g)),
        compiler_params=_cparams(("parallel", "parallel", "arbitrary")),
        name="fox_attn",
    )(qt.reshape(batch, nq, MAIN_W, FOX_BQ), aqt, k.reshape(batch, seq, MAIN_W), ak.reshape(batch, seq, MAIN_W), vt)
    return out.reshape(batch * seq, MAIN_W)


def _pad_heads_cols(w, heads, width, padded):
    lead = w.shape[:-1]
    w = w.reshape(lead + (heads, width))
    w = jnp.pad(w, [(0, 0)] * len(lead) + [(0, 0), (0, padded - width)])
    return w.reshape(lead + (heads * padded,))


def _pad_heads_rows(w, heads, width, padded):
    n = w.shape[-1]
    w = w.reshape(heads, width, n)
    w = jnp.pad(w, [(0, 0), (0, padded - width), (0, 0)])
    return w.reshape(heads * padded, n)


def _gla_weights(w_in, w_gate_up, b_gate, g_head, w_out):
    dk = GLA_HEADS * GLA_DK_HEAD
    dv = GLA_HEADS * GLA_DV_HEAD
    o1, o2, o3, o4, o5 = dk, 2 * dk, 2 * dk + dv, 2 * dk + 2 * dv, 2 * dk + 2 * dv + GLA_RANK
    w = jnp.concatenate([
        _pad_heads_cols(w_in[:, :o1], GLA_HEADS, GLA_DK_HEAD, DKP),
        _pad_heads_cols(w_in[:, o1:o2], GLA_HEADS, GLA_DK_HEAD, DKP),
        _pad_heads_cols(w_in[:, o2:o3], GLA_HEADS, GLA_DV_HEAD, DVP),
        _pad_heads_cols(w_in[:, o3:o4], GLA_HEADS, GLA_DV_HEAD, DVP),
        w_in[:, o5:],
        jnp.pad(w_in[:, o4:o5], [(0, 0), (0, LANES - GLA_RANK)]),
    ], axis=1).astype(BF16)
    wup = jnp.pad(_pad_heads_cols(w_gate_up, GLA_HEADS, GLA_DK_HEAD, DKP), [(0, LANES - GLA_RANK), (0, 0)]).astype(BF16)
    bg = _pad_heads_cols(b_gate[None, :], GLA_HEADS, GLA_DK_HEAD, DKP)
    gh = jnp.pad(g_head[None, :], [(0, 0), (0, DVP - GLA_DV_HEAD)])
    wo = jnp.concatenate([_pad_heads_rows(w_out[:MAIN_W], GLA_HEADS, GLA_DV_HEAD, DVP), w_out[MAIN_W:]], axis=0).astype(BF16)
    return w, wup, bg, gh, wo


def kernel(x, mem, norm_ffn, w_ffn_in, w_ffn_out, norm_mix, norm_mem, w_mem_kv, w_out, w_in_a, w_gate_up, b_gate,
           norm_gla_head, w_in_b, norm_shared, w_kv_shared, w_fgate, b_fgate, norm_final):
    batch, seq, _ = x.shape
    h = x.reshape(batch * seq, D_MODEL)

    w1 = w_ffn_in.astype(BF16)
    w2 = w_ffn_out.astype(BF16)
    g_ffn = norm_ffn[:, :, None, :]
    wkt =jnp.swapaxes(w_mem_kv[:, :, :MEM_W], 1, 2).astype(BF16)
    wv = w_mem_kv[:, :, MEM_W:].astype(BF16)
    kb, vb = _mem_kv(mem, norm_mem[:, None, :], wkt, wv)

    k_sh = v_sh = aq_sh = ak_sh = None
    for l in range(DEPTH):
        if l == N_A:
            w_sh = jnp.concatenate([w_kv_shared[:, :MAIN_W], jnp.pad(w_fgate, [(0, 0), (0, LANES - FOX_HEADS)])],
                                   axis=1).astype(BF16)
            wvt = w_kv_shared[:, MAIN_W:].T.astype(BF16)
            bf = jnp.pad(b_fgate[None, :], [(0, 0), (0, LANES - FOX_HEADS)])
            k_sh, v_sh, aq_sh, ak_sh = _shared_kv(h, norm_shared[None, :], w_sh, wvt, bf, batch, seq)
        h = _ffn(h, g_ffn, w1, w2, l, 0)
        if l < N_A:
            w, wup, bg, gh, wo = _gla_weights(w_in_a[l], w_gate_up[l], b_gate[l], norm_gla_head[l], w_out[l])
            q, k, v, r, la, q_mem = _gla_proj(h, norm_mix[l][None, :], w, wup, bg)
            o_main = _gla(q, k, la, v, r, gh, batch, seq)
        else:
            wb = w_in_b[l - N_A]
            qt, q_mem = _fox_proj(h, norm_mix[l][None, :], wb[:, :MAIN_W].T.astype(BF16), wb[:, MAIN_W:].astype(BF16))
            o_main = _fox_attn(qt, aq_sh, k_sh, ak_sh, v_sh, batch, seq)
            wo = w_out[l].astype(BF16)
        h = _mix_out(h, o_main, q_mem, kb, vb, wo, l, seq)
        h = _ffn(h, g_ffn, w1, w2, l, 1, norm_final[None, :] if l == DEPTH - 1 else None)
    return h.reshape(batch, seq, D_MODEL)
```

```python
import functools

import numpy as np
import jax
import jax.numpy as jnp
from jax import lax
from jax.experimental import pallas as pl
from jax.experimental.pallas import tpu as pltpu

F32 = jnp.float32
BF16 = jnp.bfloat16

D_MODEL = 1024
DEPTH = 4
N_A = DEPTH // 2
MAIN_W = 768
MEM_W = 256
GLA_HEADS = 4
GLA_DK_HEAD = 96
GLA_DV_HEAD = 192
GLA_RANK = 16
GLA_TEMP = 16.0
FOX_HEADS = 6
FOX_DH = 128
MEM_HEADS = 4
MEM_DH = 64
MEM_LEN = 256
D_FF = 2816
EPS = 1e-6
LOG2E = 1.4426950408889634

LANES = 128
MXU_N = 256
BF16_ROWS = 16
VMEM_LIMIT = 56 * 1024 * 1024

DKP = LANES
DVP = MXU_N
GLA_QW = GLA_HEADS * DKP
GLA_VW = GLA_HEADS * DVP
GLA_CHUNK = MXU_N
GLA_LEVELS = 8

TM = 512
PROJ_TM = 1024
FOX_BQ = TM
FOX_BK = TM


def _cparams(sem):
    return pltpu.CompilerParams(dimension_semantics=sem, vmem_limit_bytes=VMEM_LIMIT)


def _rmsnorm(x, g):
    return x * lax.rsqrt(jnp.mean(x * x, axis=-1, keepdims=True) + EPS) * g


def _log_sigmoid(x):
    return jnp.minimum(x, 0.0) - jnp.log(1.0 + jnp.exp(-jnp.abs(x)))


def _sigmoid(x):
    return 1.0 / (1.0 + jnp.exp(-x))


def _split3(x):
    hi = x.astype(BF16)
    r1 = x - hi.astype(F32)
    mid = r1.astype(BF16)
    lo = (r1 - mid.astype(F32)).astype(BF16)
    return jnp.concatenate([hi, mid, lo], axis=1)


def _split2(x):
    hi = x.astype(BF16)
    lo = (x - hi.astype(F32)).astype(BF16)
    return jnp.concatenate([hi, lo], axis=1)


def _dot(a, b):
    return jnp.dot(a, b, preferred_element_type=F32)


def _dot_nt(a, b):
    return lax.dot_general(a, b, (((1,), (1,)), ((), ())), preferred_element_type=F32)


def _dot_tn(a, b):
    return lax.dot_general(a, b, (((0,), (0,)), ((), ())), preferred_element_type=F32)


def _exact_sum3(r, w):
    return r[:, :w] + r[:, w:2 * w] + r[:, 2 * w:]


FFN_BOUNDS = (0, 6 * MXU_N, D_FF)
FFN_TM = 1024


def _ffn_kernel(h_ref, g_ref, w1_ref, w2_ref, *rest):
    o_ref = rest[-1]
    h = h_ref[...]
    xn = _rmsnorm(h, g_ref[0, 0]).astype(BF16)
    y = None
    for f0, f1 in zip(FFN_BOUNDS[:-1], FFN_BOUNDS[1:]):
        a = _dot(xn, w1_ref[0, 0, :, f0:f1])
        c = _dot(xn, w1_ref[0, 0, :, D_FF + f0:D_FF + f1])
        act = (a * _sigmoid(a) * c).astype(BF16)
        part = _dot(act, w2_ref[0, 0, f0:f1, :])
        y = part if y is None else y + part
    out = h + 0.5 * y
    o_ref[...] = _rmsnorm(out, rest[0][...]) if len(rest) == 2 else out


def _ffn(h, g_all, w1_all, w2_all, layer, half, g_final=None):
    n = h.shape[0]
    pick = lambda i: (layer, half, 0, 0)
    in_specs = [
        pl.BlockSpec((FFN_TM, D_MODEL), lambda i: (i, 0)),
        pl.BlockSpec((1, 1, 1, D_MODEL), pick),
        pl.BlockSpec((1, 1, D_MODEL, 2 * D_FF), pick, pipeline_mode=pl.Buffered(1)),
        pl.BlockSpec((1, 1, D_FF, D_MODEL), pick, pipeline_mode=pl.Buffered(1)),
    ]
    args = [h, g_all, w1_all, w2_all]
    if g_final is not None:
        in_specs.append(pl.BlockSpec((1, D_MODEL), lambda i: (0, 0)))
        args.append(g_final)
    return pl.pallas_call(
        _ffn_kernel,
        out_shape=jax.ShapeDtypeStruct((n, D_MODEL), F32),
        grid=(n // FFN_TM,),
        in_specs=in_specs,
        out_specs=pl.BlockSpec((FFN_TM, D_MODEL), lambda i: (i, 0)),
        compiler_params=_cparams(("parallel",)),
        name="ffn",
    )(*args)


GLA_PW = 2 * GLA_QW + 2 * GLA_VW + MEM_W + LANES


def _gla_proj_kernel(h_ref, g_ref, w_ref, wup_ref, bg_ref, q_ref, k_ref, v_ref, r_ref, la_ref, qm_ref):
    xn = _rmsnorm(h_ref[...], g_ref[...]).astype(BF16)
    o = 0
    q_ref[...] = (_dot(xn, w_ref[:, o:o + GLA_QW]) * (GLA_DK_HEAD ** -0.5)).astype(BF16)
    o += GLA_QW
    k_ref[...] = _dot(xn, w_ref[:, o:o + GLA_QW]).astype(BF16)
    o += GLA_QW
    v_ref[...] = _dot(xn, w_ref[:, o:o + GLA_VW]).astype(BF16)
    o += GLA_VW
    r_ref[...] = _dot(xn, w_ref[:, o:o + GLA_VW]).astype(BF16)
    o += GLA_VW
    qm_ref[...] = _dot(xn, w_ref[:, o:o + MEM_W]).astype(BF16)
    o += MEM_W
    g_low = _dot(xn, w_ref[:, o:o + LANES]).astype(BF16)
    x = _dot(g_low, wup_ref[...]) + bg_ref[...]
    la_ref[...] = _log_sigmoid(x) * (LOG2E / GLA_TEMP)


def _gla_proj(h, g, w, wup, bg, layer, a):
    n = h.shape[0]
    row = lambda i: (i, 0)
    return pl.pallas_call(
        _gla_proj_kernel,
        out_shape=(
            jax.ShapeDtypeStruct((n, GLA_QW), BF16),
            jax.ShapeDtypeStruct((n, GLA_QW), BF16),
            jax.ShapeDtypeStruct((n, GLA_VW), BF16),
            jax.ShapeDtypeStruct((n, GLA_VW), BF16),
            jax.ShapeDtypeStruct((n, GLA_QW), F32),
            jax.ShapeDtypeStruct((n, MEM_W), BF16),
        ),
        grid=(n // PROJ_TM,),
        in_specs=[
            pl.BlockSpec((PROJ_TM, D_MODEL), row),
            pl.BlockSpec((None, 1, D_MODEL), lambda i: (layer, 0, 0)),
            pl.BlockSpec((None, D_MODEL, GLA_PW), lambda i: (a, 0, 0), pipeline_mode=pl.Buffered(1)),
            pl.BlockSpec((None, LANES, GLA_QW), lambda i: (a, 0, 0)),
            pl.BlockSpec((None, 1, GLA_QW), lambda i: (a, 0, 0)),
        ],
        out_specs=(
            pl.BlockSpec((PROJ_TM, GLA_QW), row),
            pl.BlockSpec((PROJ_TM, GLA_QW), row),
            pl.BlockSpec((PROJ_TM, GLA_VW), row),
            pl.BlockSpec((PROJ_TM, GLA_VW), row),
            pl.BlockSpec((PROJ_TM, GLA_QW), row),
            pl.BlockSpec((PROJ_TM, MEM_W), row),
        ),
        compiler_params=_cparams(("parallel",)),
        name="gla_proj",
    )(h, g, w, wup, bg)


GLA_TS = 512


GLA_UNROLL = 2


def _gla_consts():
    c = GLA_CHUNK
    i = np.arange(c)[:, None]
    j = np.arange(c)[None, :]
    tril = (j <= i)
    b_rows, masks = [], []
    for lv in range(GLA_LEVELS):
        hs = 1 << lv
        mid = (i // (2 * hs)) * (2 * hs) + hs
        lower = (i % (2 * hs)) >= hs
        b_rows.append(np.where(lower, (j >= mid) & (j <= i), (j > i) & (j < mid)))
        masks.append(((i // (2 * hs)) == (j // (2 * hs))) & lower & ((j % (2 * hs)) < hs))
    masks.append(i == j)
    asb = lambda x: jnp.asarray(x.astype(np.float32), dtype=BF16)
    return asb(tril), asb(np.concatenate(b_rows, axis=0)), asb(np.stack(masks, axis=0))


def _gla_kernel(q_ref, k_ref, la_ref, v_ref, r_ref, gh_ref, tril_ref, b_ref, m_ref, o_ref, st_ref):
    c = GLA_CHUNK

    @pl.when(pl.program_id(1) == 0)
    def _():
        st_ref[...] = jnp.zeros_like(st_ref)

    gh = gh_ref[...]

    def chunk(ci, carry):
        r0 = pl.multiple_of(ci * c, c)
        g = la_ref[pl.ds(r0, c), :]
        cum = _dot(tril_ref[...], _split2(g))
        cum = cum[:, :GLA_QW] + cum[:, GLA_QW:]
        lev = _dot(b_ref[...], g.astype(BF16))
        qa = q_ref[pl.ds(r0, c), :]
        ka = k_ref[pl.ds(r0, c), :]
        v = v_ref[pl.ds(r0, c), :]
        r = r_ref[pl.ds(r0, c), :].astype(F32)
        for h in range(GLA_HEADS):
            ks = slice(h * DKP, (h + 1) * DKP)
            vs = slice(h * DVP, (h + 1) * DVP)
            qb, kb, vh = qa[:, ks], ka[:, ks], v[:, vs]
            cum_h = cum[:, ks]
            last_h = cum_h[c - 1:c, :]
            st = st_ref[h]
            s = _dot_nt(qb, kb).astype(BF16) * m_ref[GLA_LEVELS]
            for lv in range(GLA_LEVELS):
                e = jnp.exp2(lev[lv * c:(lv + 1) * c, ks]).astype(BF16)
                s = s + _dot_nt(qb * e, kb * e).astype(BF16) * m_ref[lv]
            qg = (qb.astype(F32) * jnp.exp2(cum_h)).astype(BF16)
            o = _dot_nt(qg, st.astype(BF16)) + _dot(s, vh)
            kd = (kb.astype(F32) * jnp.exp2(last_h - cum_h)).astype(BF16)
            st_ref[h] = st * jnp.exp2(last_h) + _dot_tn(vh, kd)
            ms = jnp.sum(o * o, axis=-1, keepdims=True) * (1.0 / GLA_DV_HEAD)
            rh = r[:, vs]
            y = o * lax.rsqrt(ms + EPS) * gh * (rh * _sigmoid(rh))
            o_ref[pl.ds(r0, c), vs] = y.astype(BF16)
        return carry

    lax.fori_loop(0, GLA_TS // c, chunk, 0, unroll=GLA_UNROLL)


def _gla(q, k, la, v, r, gh, a, batch, seq):
    n = q.shape[0]
    nt = seq // GLA_TS
    tril, bmat, m = _gla_consts()
    row = lambda b, t: (b * nt + t, 0)
    fixed2 = lambda b, t: (0, 0)
    fixed3 = lambda b, t: (0, 0, 0)
    return pl.pallas_call(
        _gla_kernel,
        out_shape=jax.ShapeDtypeStruct((n, GLA_VW), BF16),
        grid=(batch, nt),
        in_specs=[
            pl.BlockSpec((GLA_TS, GLA_QW), row),
            pl.BlockSpec((GLA_TS, GLA_QW), row),
            pl.BlockSpec((GLA_TS, GLA_QW), row),
            pl.BlockSpec((GLA_TS, GLA_VW), row),
            pl.BlockSpec((GLA_TS, GLA_VW), row),
            pl.BlockSpec((None, 1, DVP), lambda b, t: (a, 0, 0)),
            pl.BlockSpec(tril.shape, fixed2),
            pl.BlockSpec(bmat.shape, fixed2),
            pl.BlockSpec(m.shape, fixed3),
        ],
        out_specs=pl.BlockSpec((GLA_TS, GLA_VW), row),
        scratch_shapes=[pltpu.VMEM((GLA_HEADS, DVP, DKP), F32)],
        compiler_params=_cparams(("arbitrary", "arbitrary")),
        name="gla",
    )(q, k, la, v, r, gh, tril, bmat, m)


MEM_BW = MEM_HEADS * MEM_LEN


def _mem_kv_kernel(mem_ref, g_ref, wkt_ref, wv_ref, kb_ref, vb_ref):
    d_row = lax.broadcasted_iota(jnp.int32, (MEM_W, MEM_LEN), 0) // MEM_DH
    d_col = lax.broadcasted_iota(jnp.int32, (MEM_LEN, MEM_W), 1) // MEM_DH
    for b in range(mem_ref.shape[0]):
        mn = _rmsnorm(mem_ref[b], g_ref[0]).astype(BF16)
        kt = _dot_nt(wkt_ref[0], mn) * (MEM_DH ** -0.5)
        v = _dot(mn, wv_ref[0])
        for hh in range(MEM_HEADS):
            kb_ref[0, b, :, hh * MEM_LEN:(hh + 1) * MEM_LEN] = jnp.where(d_row == hh, kt, 0.0).astype(BF16)
            vb_ref[0, b, hh * MEM_LEN:(hh + 1) * MEM_LEN, :] = jnp.where(d_col == hh, v, 0.0).astype(BF16)


def _mem_kv(mem, g, wkt, wv):
    b = mem.shape[0]
    return pl.pallas_call(
        _mem_kv_kernel,
        out_shape=(
            jax.ShapeDtypeStruct((DEPTH, b, MEM_W, MEM_BW), BF16),
            jax.ShapeDtypeStruct((DEPTH, b, MEM_BW, MEM_W), BF16),
        ),
        grid=(DEPTH,),
        in_specs=[
            pl.BlockSpec((b, MEM_LEN, D_MODEL), lambda l: (0, 0, 0)),
            pl.BlockSpec((1, 1, D_MODEL), lambda l: (l, 0, 0)),
            pl.BlockSpec((1, MEM_W, D_MODEL), lambda l: (l, 0, 0)),
            pl.BlockSpec((1, D_MODEL, MEM_W), lambda l: (l, 0, 0)),
        ],
        out_specs=(
            pl.BlockSpec((1, b, MEM_W, MEM_BW), lambda l: (l, 0, 0, 0)),
            pl.BlockSpec((1, b, MEM_BW, MEM_W), lambda l: (l, 0, 0, 0)),
        ),
        compiler_params=_cparams(("parallel",)),
        name="mem_kv",
    )(mem, g, wkt, wv)


def _mix_out_kernel(h_ref, om_ref, qm_ref, kb_ref, vb_ref, wo_ref, o_ref):
    lg = _dot(qm_ref[...], kb_ref[0, 0])
    ps = []
    for hh in range(MEM_HEADS):
        s = lg[:, hh * MEM_LEN:(hh + 1) * MEM_LEN]
        e = jnp.exp(s - jnp.max(s, axis=-1, keepdims=True))
        ps.append((e / jnp.sum(e, axis=-1, keepdims=True)).astype(BF16))
    o_mem = _dot(jnp.concatenate(ps, axis=1), vb_ref[0, 0])
    x = jnp.concatenate([om_ref[...], o_mem.astype(BF16)], axis=1)
    o_ref[...] = h_ref[...] + _dot(x, wo_ref[...])


def _mix_out(h, o_main, q_mem, kb, vb, wo, layer, j, seq):
    n = h.shape[0]
    wm = o_main.shape[1]
    per_b = seq // PROJ_TM
    row = lambda i: (i, 0)
    return pl.pallas_call(
        _mix_out_kernel,
        out_shape=jax.ShapeDtypeStruct((n, D_MODEL), F32),
        grid=(n // PROJ_TM,),
        in_specs=[
            pl.BlockSpec((PROJ_TM, D_MODEL), row),
            pl.BlockSpec((PROJ_TM, wm), row),
            pl.BlockSpec((PROJ_TM, MEM_W), row),
            pl.BlockSpec((1, 1, MEM_W, MEM_BW), lambda i: (layer, i // per_b, 0, 0)),
            pl.BlockSpec((1, 1, MEM_BW, MEM_W), lambda i: (layer, i // per_b, 0, 0)),
            pl.BlockSpec((None, wm + MEM_W, D_MODEL), lambda i: (j, 0, 0)),
        ],
        out_specs=pl.BlockSpec((PROJ_TM, D_MODEL), row),
        compiler_params=_cparams(("parallel",)),
        name="mix_out",
    )(h, o_main, q_mem, kb, vb, wo)


SH_PW = MAIN_W + LANES


ONE_LANE = LANES - 1


def _fox_aux_consts():
    pqt = np.zeros((MAIN_W, 3 * LANES), np.float32)
    pk = np.zeros((3 * LANES, MAIN_W), np.float32)
    for h in range(FOX_HEADS):
        for p in range(3):
            pqt[h * FOX_DH + p, p * LANES + h] = 1.0
            pqt[h * FOX_DH + 3 + p, ONE_LANE] = 1.0
            pk[ONE_LANE, h * FOX_DH + p] = 1.0
            pk[p * LANES + h, h * FOX_DH + 3 + p] = -1.0
    tril = np.tril(np.ones((TM, TM), np.float32))
    return jnp.asarray(tril, dtype=BF16), jnp.asarray(pqt, dtype=BF16), jnp.asarray(pk, dtype=BF16)


def _shared_kernel(h_ref, g_ref, w_ref, wvt_ref, bf_ref, tril_ref, pqt_ref, pk_ref,
                   k_ref, vt_ref, aqt_ref, ak_ref, carry_ref):
    @pl.when(pl.program_id(1) == 0)
    def _():
        carry_ref[...] = jnp.zeros_like(carry_ref)

    xn = _rmsnorm(h_ref[...], g_ref[...]).astype(BF16)
    k_ref[...] = _dot(xn, w_ref[:, 0:MAIN_W]).astype(BF16)
    vt_ref[0, 0] = _dot_nt(wvt_ref[...], xn).astype(BF16)
    log_f = _log_sigmoid(_dot(xn, w_ref[:, MAIN_W:SH_PW]) + bf_ref[...])
    d = _exact_sum3(_dot(tril_ref[...], _split3(log_f)), LANES) + carry_ref[...]
    carry_ref[...] = d[TM - 1:TM, :]
    d3 = _split3(d * LOG2E)
    lane = lax.broadcasted_iota(jnp.int32, d3.shape, 1)
    d3 = jnp.where(lane == ONE_LANE, jnp.ones_like(d3), d3)
    aqt_ref[0, 0] = _dot_nt(pqt_ref[...], d3).astype(BF16)
    ak_ref[...] = _dot(d3, pk_ref[...]).astype(BF16)


def _shared_kv(h, g, w, wvt, bf, batch, seq):
    n = h.shape[0]
    nt = seq // TM
    tril, pqt, pk = _fox_aux_consts()
    row = lambda b, t: (b * nt + t, 0)
    fixed = lambda b, t: (0, 0)
    out = jax.ShapeDtypeStruct((n, MAIN_W), BF16)
    out_t = jax.ShapeDtypeStruct((batch, nt, MAIN_W, TM), BF16)
    row_spec = pl.BlockSpec((TM, MAIN_W), row)
    slab_spec = pl.BlockSpec((1, 1, MAIN_W, TM), lambda b, t: (b, t, 0, 0))
    return pl.pallas_call(
        _shared_kernel,
        out_shape=(out, out_t, out_t, out),
        grid=(batch, nt),
        in_specs=[
            pl.BlockSpec((TM, D_MODEL), row),
            pl.BlockSpec((1, D_MODEL), fixed),
            pl.BlockSpec((D_MODEL, SH_PW), fixed),
            pl.BlockSpec((MAIN_W, D_MODEL), fixed),
            pl.BlockSpec((1, LANES), fixed),
            pl.BlockSpec((TM, TM), fixed),
            pl.BlockSpec((MAIN_W, 3 * LANES), fixed),
            pl.BlockSpec((3 * LANES, MAIN_W), fixed),
        ],
        out_specs=(row_spec, slab_spec, slab_spec, row_spec),
        scratch_shapes=[pltpu.VMEM((1, LANES), F32)],
        compiler_params=_cparams(("arbitrary", "arbitrary")),
        name="shared_kv",
    )(h, g, w, wvt, bf, tril, pqt, pk)


def _fox_proj_kernel(h_ref, g_ref, wqt_ref, wm_ref, qt_ref, qm_ref):
    xn = _rmsnorm(h_ref[...], g_ref[...]).astype(BF16)
    qt_ref[0] = (_dot_nt(wqt_ref[...], xn) * (FOX_DH ** -0.5 * LOG2E)).astype(BF16)
    qm_ref[...] = _dot(xn, wm_ref[...]).astype(BF16)


def _fox_proj(h, g, wqt, wm, layer, j):
    n = h.shape[0]
    row = lambda i: (i, 0)
    return pl.pallas_call(
        _fox_proj_kernel,
        out_shape=(jax.ShapeDtypeStruct((n // TM, MAIN_W, TM), BF16), jax.ShapeDtypeStruct((n, MEM_W), BF16)),
        grid=(n // TM,),
        in_specs=[
            pl.BlockSpec((TM, D_MODEL), row),
            pl.BlockSpec((None, 1, D_MODEL), lambda i: (layer, 0, 0)),
            pl.BlockSpec((None, MAIN_W, D_MODEL), lambda i: (j, 0, 0)),
            pl.BlockSpec((None, D_MODEL, MEM_W), lambda i: (j, 0, 0)),
        ],
        out_specs=(pl.BlockSpec((1, MAIN_W, TM), lambda i: (i, 0, 0)), pl.BlockSpec((TM, MEM_W), row)),
        compiler_params=_cparams(("parallel",)),
        name="fox_proj",
    )(h, g, wqt, wm)


FOX_HPS = 3


FOX_SUB = MXU_N
FOX_NSUB = FOX_BK // FOX_SUB


def _fox_attn_kernel(qt_ref, aqt_ref, k_ref, ak_ref, vt_ref, o_ref):
    qi = pl.program_id(2)
    chains = [(hh, qs) for hh in range(FOX_HPS) for qs in range(FOX_BQ // FOX_SUB)]

    def head(hh):
        return slice(hh * FOX_DH, (hh + 1) * FOX_DH)

    def strip(s):
        return slice(s * FOX_SUB, (s + 1) * FOX_SUB)

    qq = [jnp.concatenate([qt_ref[0, 0, head(hh), strip(qs)], aqt_ref[0, 0, head(hh), strip(qs)]], axis=0)
          for hh, qs in chains]

    def scores(kj, diagonal):
        c0 = pl.multiple_of(kj * FOX_BK, FOX_BK)
        sts = {}
        for ks in range(FOX_NSUB):
            rows = pl.ds(c0 + ks * FOX_SUB, FOX_SUB)
            for ci, (hh, qs) in enumerate(chains):
                if diagonal and ks > qs:
                    continue
                kk = jnp.concatenate([k_ref[0, rows, head(hh)], ak_ref[0, rows, head(hh)]], axis=1)
                st = _dot(kk, qq[ci])
                if diagonal and ks == qs:
                    key = lax.broadcasted_iota(jnp.int32, st.shape, 0)
                    qry = lax.broadcasted_iota(jnp.int32, st.shape, 1)
                    st = jnp.where(key <= qry, st, -jnp.inf)
                sts[ks, ci] = st
        return sts

    def update(kj, carry, sts):
        carry = list(carry)
        for ks in range(FOX_NSUB):
            for ci, (hh, qs) in enumerate(chains):
                if (ks, ci) not in sts:
                    continue
                m, acc = carry[ci]
                st = sts[ks, ci]
                m_new = jnp.maximum(m, jnp.max(st, axis=0, keepdims=True))
                alpha = jnp.exp2(m - m_new)
                p = jnp.exp2(st - m_new).astype(BF16)
                vt1 = jnp.concatenate([vt_ref[0, kj, head(hh), strip(ks)], ones], axis=0)
                acc = alpha * acc + _dot(vt1, p)
                carry[ci] = (m_new, acc)
        return tuple(carry)

    ones = jnp.ones((BF16_ROWS, FOX_SUB), BF16)
    init = tuple((jnp.full((1, FOX_SUB), -jnp.inf, F32), jnp.zeros((FOX_DH + BF16_ROWS, FOX_SUB), F32))
                 for _ in chains)

    carry = lax.fori_loop(0, qi, lambda kj, cr: update(kj, cr, scores(kj, False)), init)
    carry = update(qi, carry, scores(qi, True))
    for ci, (hh, qs) in enumerate(chains):
        _, acc = carry[ci]
        o_ref[0, strip(qs), head(hh)] = (acc[:FOX_DH] / acc[FOX_DH:FOX_DH + 1]).T.astype(BF16)


def _fox_attn(qt, aqt, k, ak, vt, batch, seq):
    w = FOX_HPS * FOX_DH
    nq = seq // FOX_BQ
    qblk = pl.BlockSpec((1, 1, w, FOX_BQ), lambda b, g, i: (b, i, g, 0))
    kblk = pl.BlockSpec((1, seq, w), lambda b, g, i: (b, 0, g))
    vblk = pl.BlockSpec((1, seq // FOX_BK, w, FOX_BK), lambda b, g, i: (b, 0, g, 0))
    out = pl.pallas_call(
        _fox_attn_kernel,
        out_shape=jax.ShapeDtypeStruct((batch, seq, MAIN_W), BF16),
        grid=(batch, FOX_HEADS // FOX_HPS, nq),
        in_specs=[qblk, qblk, kblk, kblk, vblk],
        out_specs=pl.BlockSpec((1, FOX_BQ, w), lambda b, g, i: (b, i, g)),
        compiler_params=_cparams(("parallel", "parallel", "arbitrary")),
        name="fox_attn",
    )(qt.reshape(batch, nq, MAIN_W, FOX_BQ), aqt, k.reshape(batch, seq, MAIN_W), ak.reshape(batch, seq, MAIN_W), vt)
    return out.reshape(batch * seq, MAIN_W)


def _pad_heads_cols(w, heads, width, padded):
    lead = w.shape[:-1]
    w = w.reshape(lead + (heads, width))
    w = jnp.pad(w, [(0, 0)] * len(lead) + [(0, 0), (0, padded - width)])
    return w.reshape(lead + (heads * padded,))


def _pad_heads_rows(w, heads, width, padded):
    layers, _, n = w.shape
    w = w.reshape(layers, heads, width, n)
    w = jnp.pad(w, [(0, 0), (0, 0), (0, padded - width), (0, 0)])
    return w.reshape(layers, heads * padded, n)


def _gla_weights(w_in, w_gate_up, b_gate, g_head, w_out):
    dk = GLA_HEADS * GLA_DK_HEAD
    dv = GLA_HEADS * GLA_DV_HEAD
    o1, o2, o3, o4, o5 = dk, 2 * dk, 2 * dk + dv, 2 * dk + 2 * dv, 2 * dk + 2 * dv + GLA_RANK
    w = jnp.concatenate([
        _pad_heads_cols(w_in[..., :o1], GLA_HEADS, GLA_DK_HEAD, DKP),
        _pad_heads_cols(w_in[..., o1:o2], GLA_HEADS, GLA_DK_HEAD, DKP),
        _pad_heads_cols(w_in[..., o2:o3], GLA_HEADS, GLA_DV_HEAD, DVP),
        _pad_heads_cols(w_in[..., o3:o4], GLA_HEADS, GLA_DV_HEAD, DVP),
        w_in[..., o5:],
        jnp.pad(w_in[..., o4:o5], [(0, 0), (0, 0), (0, LANES - GLA_RANK)]),
    ], axis=-1).astype(BF16)
    wup = jnp.pad(_pad_heads_cols(w_gate_up, GLA_HEADS, GLA_DK_HEAD, DKP),
                  [(0, 0), (0, LANES - GLA_RANK), (0, 0)]).astype(BF16)
    bg = _pad_heads_cols(b_gate[:, None, :], GLA_HEADS, GLA_DK_HEAD, DKP)
    gh = jnp.pad(g_head[:, None, :], [(0, 0), (0, 0), (0, DVP - GLA_DV_HEAD)])
    wo = jnp.concatenate([_pad_heads_rows(w_out[:, :MAIN_W], GLA_HEADS, GLA_DV_HEAD, DVP), w_out[:, MAIN_W:]],
                         axis=1).astype(BF16)
    return w, wup, bg, gh, wo


def kernel(x, mem, norm_ffn, w_ffn_in, w_ffn_out, norm_mix, norm_mem, w_mem_kv, w_out, w_in_a, w_gate_up, b_gate,
           norm_gla_head, w_in_b, norm_shared, w_kv_shared, w_fgate, b_fgate, norm_final):
    batch, seq, _ = x.shape
    h = x.reshape(batch * seq, D_MODEL)

    w1 = w_ffn_in.astype(BF16)
    w2 = w_ffn_out.astype(BF16)
    g_ffn = norm_ffn[:, :, None, :]
    g_mix = norm_mix[:, None, :]
    wkt = jnp.swapaxes(w_mem_kv[:, :, :MEM_W], 1, 2).astype(BF16)
    wv = w_mem_kv[:, :, MEM_W:].astype(BF16)
    kb, vb = _mem_kv(mem, norm_mem[:, None, :], wkt, wv)
    w_a, wup, bg, gh, wo_a = _gla_weights(w_in_a, w_gate_up, b_gate, norm_gla_head, w_out[:N_A])
    wqt_b = jnp.swapaxes(w_in_b[:, :, :MAIN_W], 1, 2).astype(BF16)
    wm_b = w_in_b[:, :, MAIN_W:].astype(BF16)
    wo_b = w_out[N_A:].astype(BF16)

    k_sh = v_sh = aq_sh = ak_sh = None
    for l in range(DEPTH):
        if l == N_A:
            w_sh = jnp.concatenate([w_kv_shared[:, :MAIN_W], jnp.pad(w_fgate, [(0, 0), (0, LANES - FOX_HEADS)])],
                                   axis=1).astype(BF16)
            wvt = w_kv_shared[:, MAIN_W:].T.astype(BF16)
            bf = jnp.pad(b_fgate[None, :], [(0, 0), (0, LANES - FOX_HEADS)])
            k_sh, v_sh, aq_sh, ak_sh = _shared_kv(h, norm_shared[None, :], w_sh, wvt, bf, batch, seq)
        h = _ffn(h, g_ffn, w1, w2, l, 0)
        if l < N_A:
            q, k, v, r, la, q_mem = _gla_proj(h, g_mix, w_a, wup, bg, l, l)
            o_main = _gla(q, k, la, v, r, gh, l, batch, seq)
            h = _mix_out(h, o_main, q_mem, kb, vb, wo_a, l, l, seq)
        else:
            qt, q_mem = _fox_proj(h, g_mix, wqt_b, wm_b, l, l - N_A)
            o_main = _fox_attn(qt, aq_sh, k_sh, ak_sh, v_sh, batch, seq)
            h = _mix_out(h, o_main, q_mem, kb, vb, wo_b, l, l - N_A, seq)
        h = _ffn(h, g_ffn, w1, w2, l, 1, norm_final[None, :] if l == DEPTH - 1 else None)
    return h.reshape(batch, seq, D_MODEL)
```

```python
import functools

import numpy as np
import jax
import jax.numpy as jnp
from jax import lax
from jax.experimental import pallas as pl
from jax.experimental.pallas import tpu as pltpu

F32 = jnp.float32
BF16 = jnp.bfloat16

D_MODEL = 1024
DEPTH = 4
N_A = DEPTH // 2
MAIN_W = 768
MEM_W = 256
GLA_HEADS = 4
GLA_DK_HEAD = 96
GLA_DV_HEAD = 192
GLA_RANK = 16
GLA_TEMP = 16.0
FOX_HEADS = 6
FOX_DH = 128
MEM_HEADS = 4
MEM_DH = 64
MEM_LEN = 256
D_FF = 2816
EPS = 1e-6
LOG2E = 1.4426950408889634

LANES = 128
MXU_N = 256
BF16_ROWS = 16
VMEM_LIMIT = 56 * 1024 * 1024

DKP = LANES
DVP = MXU_N
GLA_QW = GLA_HEADS * DKP
GLA_VW = GLA_HEADS * DVP
GLA_CHUNK = MXU_N
GLA_LEVELS = 8

TM = 512
PROJ_TM = 1024
FOX_BQ = TM
FOX_BK = TM


def _cparams(sem):
    return pltpu.CompilerParams(dimension_semantics=sem, vmem_limit_bytes=VMEM_LIMIT)


def _rmsnorm(x, g):
    return x * lax.rsqrt(jnp.mean(x * x, axis=-1, keepdims=True) + EPS) * g


def _log_sigmoid(x):
    return jnp.minimum(x, 0.0) - jnp.log(1.0 + jnp.exp(-jnp.abs(x)))


def _sigmoid(x):
    return 1.0 / (1.0 + jnp.exp(-x))


def _split3(x):
    hi = x.astype(BF16)
    r1 = x - hi.astype(F32)
    mid = r1.astype(BF16)
    lo = (r1 - mid.astype(F32)).astype(BF16)
    return jnp.concatenate([hi, mid, lo], axis=1)


def _split2(x):
    hi = x.astype(BF16)
    lo = (x - hi.astype(F32)).astype(BF16)
    return jnp.concatenate([hi, lo], axis=1)


def _dot(a, b):
    return jnp.dot(a, b, preferred_element_type=F32)


def _dot_nt(a, b):
    return lax.dot_general(a, b, (((1,), (1,)), ((), ())), preferred_element_type=F32)


def _dot_tn(a, b):
    return lax.dot_general(a, b, (((0,), (0,)), ((), ())), preferred_element_type=F32)


def _exact_sum3(r, w):
    return r[:, :w] + r[:, w:2 * w] + r[:, 2 * w:]


FFN_BOUNDS = (0, 6 * MXU_N, D_FF)
FFN_TM = 1024


def _ffn_kernel(h_ref, g_ref, w1_ref, w2_ref, *rest):
    o_ref = rest[-1]
    h = h_ref[...]
    xn = _rmsnorm(h, g_ref[0, 0]).astype(BF16)
    y = None
    for f0, f1 in zip(FFN_BOUNDS[:-1], FFN_BOUNDS[1:]):
        a = _dot(xn, w1_ref[0, 0, :, f0:f1])
        c = _dot(xn, w1_ref[0, 0, :, D_FF + f0:D_FF + f1])
        act = (a * _sigmoid(a) * c).astype(BF16)
        part = _dot(act, w2_ref[0, 0, f0:f1, :])
        y = part if y is None else y + part
    out = h + 0.5 * y
    o_ref[...] = _rmsnorm(out, rest[0][...]) if len(rest) == 2 else out


def _ffn(h, g_all, w1_all, w2_all, layer, half, g_final=None):
    n = h.shape[0]
    pick = lambda i: (layer, half, 0, 0)
    in_specs = [
        pl.BlockSpec((FFN_TM, D_MODEL), lambda i: (i, 0)),
        pl.BlockSpec((1, 1, 1, D_MODEL), pick),
        pl.BlockSpec((1, 1, D_MODEL, 2 * D_FF), pick, pipeline_mode=pl.Buffered(1)),
        pl.BlockSpec((1, 1, D_FF, D_MODEL), pick, pipeline_mode=pl.Buffered(1)),
    ]
    args = [h, g_all, w1_all, w2_all]
    if g_final is not None:
        in_specs.append(pl.BlockSpec((1, D_MODEL), lambda i: (0, 0)))
        args.append(g_final)
    return pl.pallas_call(
        _ffn_kernel,
        out_shape=jax.ShapeDtypeStruct((n, D_MODEL), F32),
        grid=(n // FFN_TM,),
        in_specs=in_specs,
        out_specs=pl.BlockSpec((FFN_TM, D_MODEL), lambda i: (i, 0)),
        compiler_params=_cparams(("parallel",)),
        name="ffn",
    )(*args)


GLA_PW = 2 * GLA_QW + 2 * GLA_VW + MEM_W + LANES


def _gla_proj_kernel(h_ref, g_ref, w_ref, wup_ref, bg_ref, q_ref, k_ref, v_ref, r_ref, la_ref, qm_ref):
    xn = _rmsnorm(h_ref[...], g_ref[...]).astype(BF16)
    o = 0
    q_ref[...] = (_dot(xn, w_ref[:, o:o + GLA_QW]) * (GLA_DK_HEAD ** -0.5)).astype(BF16)
    o += GLA_QW
    k_ref[...] = _dot(xn, w_ref[:, o:o + GLA_QW]).astype(BF16)
    o += GLA_QW
    v_ref[...] = _dot(xn, w_ref[:, o:o + GLA_VW]).astype(BF16)
    o += GLA_VW
    r_ref[...] = _dot(xn, w_ref[:, o:o + GLA_VW]).astype(BF16)
    o += GLA_VW
    qm_ref[...] = _dot(xn, w_ref[:, o:o + MEM_W]).astype(BF16)
    o += MEM_W
    g_low = _dot(xn, w_ref[:, o:o + LANES]).astype(BF16)
    x = _dot(g_low, wup_ref[...]) + bg_ref[...]
    la_ref[...] = _log_sigmoid(x) * (LOG2E / GLA_TEMP)


def _gla_proj(h, g, w, wup, bg, layer, a):
    n = h.shape[0]
    row = lambda i: (i, 0)
    return pl.pallas_call(
        _gla_proj_kernel,
        out_shape=(
            jax.ShapeDtypeStruct((n, GLA_QW), BF16),
            jax.ShapeDtypeStruct((n, GLA_QW), BF16),
            jax.ShapeDtypeStruct((n, GLA_VW), BF16),
            jax.ShapeDtypeStruct((n, GLA_VW), BF16),
            jax.ShapeDtypeStruct((n, GLA_QW), F32),
            jax.ShapeDtypeStruct((n, MEM_W), BF16),
        ),
        grid=(n // PROJ_TM,),
        in_specs=[
            pl.BlockSpec((PROJ_TM, D_MODEL), row),
            pl.BlockSpec((None, 1, D_MODEL), lambda i: (layer, 0, 0)),
            pl.BlockSpec((None, D_MODEL, GLA_PW), lambda i: (a, 0, 0), pipeline_mode=pl.Buffered(1)),
            pl.BlockSpec((None, LANES, GLA_QW), lambda i: (a, 0, 0)),
            pl.BlockSpec((None, 1, GLA_QW), lambda i: (a, 0, 0)),
        ],
        out_specs=(
            pl.BlockSpec((PROJ_TM, GLA_QW), row),
            pl.BlockSpec((PROJ_TM, GLA_QW), row),
            pl.BlockSpec((PROJ_TM, GLA_VW), row),
            pl.BlockSpec((PROJ_TM, GLA_VW), row),
            pl.BlockSpec((PROJ_TM, GLA_QW), row),
            pl.BlockSpec((PROJ_TM, MEM_W), row),
        ),
        compiler_params=_cparams(("parallel",)),
        name="gla_proj",
    )(h, g, w, wup, bg)


GLA_TS = 1024


GLA_UNROLL = 2


def _gla_consts():
    c = GLA_CHUNK
    i = np.arange(c)[:, None]
    j = np.arange(c)[None, :]
    tril = (j <= i)
    b_rows, masks = [], []
    for lv in range(GLA_LEVELS):
        hs = 1 << lv
        mid = (i // (2 * hs)) * (2 * hs) + hs
        lower = (i % (2 * hs)) >= hs
        b_rows.append(np.where(lower, (j >= mid) & (j <= i), (j > i) & (j < mid)))
        masks.append(((i // (2 * hs)) == (j // (2 * hs))) & lower & ((j % (2 * hs)) < hs))
    masks.append(i == j)
    asb = lambda x: jnp.asarray(x.astype(np.float32), dtype=BF16)
    return asb(tril), asb(np.concatenate(b_rows, axis=0)), asb(np.stack(masks, axis=0))


def _gla_kernel(q_ref, k_ref, la_ref, v_ref, r_ref, gh_ref, tril_ref, b_ref, m_ref, o_ref, st_ref):
    c = GLA_CHUNK

    @pl.when(pl.program_id(1) == 0)
    def _():
        st_ref[...] = jnp.zeros_like(st_ref)

    gh = gh_ref[...]

    def chunk(ci, carry):
        r0 = pl.multiple_of(ci * c, c)
        g = la_ref[pl.ds(r0, c), :]
        cum = _dot(tril_ref[...], _split2(g))
        cum = cum[:, :GLA_QW] + cum[:, GLA_QW:]
        lev = _dot(b_ref[...], g.astype(BF16))
        qa = q_ref[pl.ds(r0, c), :]
        ka = k_ref[pl.ds(r0, c), :]
        v = v_ref[pl.ds(r0, c), :]
        r = r_ref[pl.ds(r0, c), :].astype(F32)
        for h in range(GLA_HEADS):
            ks = slice(h * DKP, (h + 1) * DKP)
            vs = slice(h * DVP, (h + 1) * DVP)
            qb, kb, vh = qa[:, ks], ka[:, ks], v[:, vs]
            cum_h = cum[:, ks]
            last_h = cum_h[c - 1:c, :]
            st = st_ref[h]
            s = _dot_nt(qb, kb).astype(BF16) * m_ref[GLA_LEVELS]
            for lv in range(GLA_LEVELS):
                e = jnp.exp2(lev[lv * c:(lv + 1) * c, ks]).astype(BF16)
                s = s + _dot_nt(qb * e, kb * e).astype(BF16) * m_ref[lv]
            qg = (qb.astype(F32) * jnp.exp2(cum_h)).astype(BF16)
            o = _dot_nt(qg, st.astype(BF16)) + _dot(s, vh)
            kd = (kb.astype(F32) * jnp.exp2(last_h - cum_h)).astype(BF16)
            st_ref[h] = st * jnp.exp2(last_h) + _dot_tn(vh, kd)
            ms = jnp.sum(o * o, axis=-1, keepdims=True) * (1.0 / GLA_DV_HEAD)
            rh = r[:, vs]
            y = o * lax.rsqrt(ms + EPS) * gh * (rh * _sigmoid(rh))
            o_ref[pl.ds(r0, c), vs] = y.astype(BF16)
        return carry

    lax.fori_loop(0, GLA_TS // c, chunk, 0, unroll=GLA_UNROLL)


def _gla(q, k, la, v, r, gh, a, batch, seq):
    n = q.shape[0]
    nt = seq // GLA_TS
    tril, bmat, m = _gla_consts()
    row = lambda b, t: (b * nt + t, 0)
    fixed2 = lambda b, t: (0, 0)
    fixed3 = lambda b, t: (0, 0, 0)
    return pl.pallas_call(
        _gla_kernel,
        out_shape=jax.ShapeDtypeStruct((n, GLA_VW), BF16),
        grid=(batch, nt),
        in_specs=[
            pl.BlockSpec((GLA_TS, GLA_QW), row),
            pl.BlockSpec((GLA_TS, GLA_QW), row),
            pl.BlockSpec((GLA_TS, GLA_QW), row),
            pl.BlockSpec((GLA_TS, GLA_VW), row),
            pl.BlockSpec((GLA_TS, GLA_VW), row),
            pl.BlockSpec((None, 1, DVP), lambda b, t: (a, 0, 0)),
            pl.BlockSpec(tril.shape, fixed2),
            pl.BlockSpec(bmat.shape, fixed2),
            pl.BlockSpec(m.shape, fixed3),
        ],
        out_specs=pl.BlockSpec((GLA_TS, GLA_VW), row),
        scratch_shapes=[pltpu.VMEM((GLA_HEADS, DVP, DKP), F32)],
        compiler_params=_cparams(("arbitrary", "arbitrary")),
        name="gla",
    )(q, k, la, v, r, gh, tril, bmat, m)


MEM_BW = MEM_HEADS * MEM_LEN


def _mem_kv_kernel(mem_ref, g_ref, wkt_ref, wv_ref, kb_ref, vb_ref):
    d_row = lax.broadcasted_iota(jnp.int32, (MEM_W, MEM_LEN), 0) // MEM_DH
    d_col = lax.broadcasted_iota(jnp.int32, (MEM_LEN, MEM_W), 1) // MEM_DH
    for b in range(mem_ref.shape[0]):
        mn = _rmsnorm(mem_ref[b], g_ref[0]).astype(BF16)
        kt = _dot_nt(wkt_ref[0], mn) * (MEM_DH ** -0.5)
        v = _dot(mn, wv_ref[0])
        for hh in range(MEM_HEADS):
            kb_ref[0, b, :, hh * MEM_LEN:(hh + 1) * MEM_LEN] = jnp.where(d_row == hh, kt, 0.0).astype(BF16)
            vb_ref[0, b, hh * MEM_LEN:(hh + 1) * MEM_LEN, :] = jnp.where(d_col == hh, v, 0.0).astype(BF16)


def _mem_kv(mem, g, wkt, wv):
    b = mem.shape[0]
    return pl.pallas_call(
        _mem_kv_kernel,
        out_shape=(
            jax.ShapeDtypeStruct((DEPTH, b, MEM_W, MEM_BW), BF16),
            jax.ShapeDtypeStruct((DEPTH, b, MEM_BW, MEM_W), BF16),
        ),
        grid=(DEPTH,),
        in_specs=[
            pl.BlockSpec((b, MEM_LEN, D_MODEL), lambda l: (0, 0, 0)),
            pl.BlockSpec((1, 1, D_MODEL), lambda l: (l, 0, 0)),
            pl.BlockSpec((1, MEM_W, D_MODEL), lambda l: (l, 0, 0)),
            pl.BlockSpec((1, D_MODEL, MEM_W), lambda l: (l, 0, 0)),
        ],
        out_specs=(
            pl.BlockSpec((1, b, MEM_W, MEM_BW), lambda l: (l, 0, 0, 0)),
            pl.BlockSpec((1, b, MEM_BW, MEM_W), lambda l: (l, 0, 0, 0)),
        ),
        compiler_params=_cparams(("parallel",)),
        name="mem_kv",
    )(mem, g, wkt, wv)


def _mix_out_kernel(h_ref, om_ref, qm_ref, kb_ref, vb_ref, wo_ref, o_ref):
    lg = _dot(qm_ref[...], kb_ref[0, 0])
    ps = []
    for hh in range(MEM_HEADS):
        s = lg[:, hh * MEM_LEN:(hh + 1) * MEM_LEN]
        e = jnp.exp(s - jnp.max(s, axis=-1, keepdims=True))
        ps.append((e / jnp.sum(e, axis=-1, keepdims=True)).astype(BF16))
    o_mem = _dot(jnp.concatenate(ps, axis=1), vb_ref[0, 0])
    x = jnp.concatenate([om_ref[...], o_mem.astype(BF16)], axis=1)
    o_ref[...] = h_ref[...] + _dot(x, wo_ref[...])


def _mix_out(h, o_main, q_mem, kb, vb, wo, layer, j, seq):
    n = h.shape[0]
    wm = o_main.shape[1]
    per_b = seq // PROJ_TM
    row = lambda i: (i, 0)
    return pl.pallas_call(
        _mix_out_kernel,
        out_shape=jax.ShapeDtypeStruct((n, D_MODEL), F32),
        grid=(n // PROJ_TM,),
        in_specs=[
            pl.BlockSpec((PROJ_TM, D_MODEL), row),
            pl.BlockSpec((PROJ_TM, wm), row),
            pl.BlockSpec((PROJ_TM, MEM_W), row),
            pl.BlockSpec((1, 1, MEM_W, MEM_BW), lambda i: (layer, i // per_b, 0, 0)),
            pl.BlockSpec((1, 1, MEM_BW, MEM_W), lambda i: (layer, i // per_b, 0, 0)),
            pl.BlockSpec((None, wm + MEM_W, D_MODEL), lambda i: (j, 0, 0)),
        ],
        out_specs=pl.BlockSpec((PROJ_TM, D_MODEL), row),
        compiler_params=_cparams(("parallel",)),
        name="mix_out",
    )(h, o_main, q_mem, kb, vb, wo)


SH_PW = MAIN_W + LANES


ONE_LANE = LANES - 1
AUX_SLOT = BF16_ROWS


def _fox_aux_consts():
    pqt = np.zeros((LANES, 3 * LANES), np.float32)
    pk = np.zeros((3 * LANES, LANES), np.float32)
    for h in range(FOX_HEADS):
        for p in range(3):
            pqt[h * AUX_SLOT + p, p * LANES + h] = 1.0
            pqt[h * AUX_SLOT + 3 + p, ONE_LANE] = 1.0
            pk[ONE_LANE, h * AUX_SLOT + p] = 1.0
            pk[p * LANES + h, h * AUX_SLOT + 3 + p] = -1.0
    tril = np.tril(np.ones((TM, TM), np.float32))
    return jnp.asarray(tril, dtype=BF16), jnp.asarray(pqt, dtype=BF16), jnp.asarray(pk, dtype=BF16)


def _shared_kernel(h_ref, g_ref, w_ref, wvt_ref, bf_ref, tril_ref, pqt_ref, pk_ref,
                   k_ref, vt_ref, aqt_ref, ak_ref, carry_ref):
    @pl.when(pl.program_id(1) == 0)
    def _():
        carry_ref[...] = jnp.zeros_like(carry_ref)

    xn = _rmsnorm(h_ref[...], g_ref[...]).astype(BF16)
    k_ref[...] = _dot(xn, w_ref[:, 0:MAIN_W]).astype(BF16)
    vt_ref[0, 0] = _dot_nt(wvt_ref[...], xn).astype(BF16)
    log_f = _log_sigmoid(_dot(xn, w_ref[:, MAIN_W:SH_PW]) + bf_ref[...])
    d = _exact_sum3(_dot(tril_ref[...], _split3(log_f)), LANES) + carry_ref[...]
    carry_ref[...] = d[TM - 1:TM, :]
    d3 = _split3(d * LOG2E)
    lane = lax.broadcasted_iota(jnp.int32, d3.shape, 1)
    d3 = jnp.where(lane == ONE_LANE, jnp.ones_like(d3), d3)
    aqt_ref[0, 0] = _dot_nt(pqt_ref[...], d3).astype(BF16)
    ak_ref[...] = _dot(d3, pk_ref[...]).astype(BF16)


def _shared_kv(h, g, w, wvt, bf, batch, seq):
    n = h.shape[0]
    nt = seq // TM
    tril, pqt, pk = _fox_aux_consts()
    row = lambda b, t: (b * nt + t, 0)
    fixed = lambda b, t: (0, 0)
    slab = lambda b, t: (b, t, 0, 0)
    return pl.pallas_call(
        _shared_kernel,
        out_shape=(
            jax.ShapeDtypeStruct((n, MAIN_W), BF16),
            jax.ShapeDtypeStruct((batch, nt, MAIN_W, TM), BF16),
            jax.ShapeDtypeStruct((batch, nt, LANES, TM), BF16),
            jax.ShapeDtypeStruct((n, LANES), BF16),
        ),
        grid=(batch, nt),
        in_specs=[
            pl.BlockSpec((TM, D_MODEL), row),
            pl.BlockSpec((1, D_MODEL), fixed),
            pl.BlockSpec((D_MODEL, SH_PW), fixed),
            pl.BlockSpec((MAIN_W, D_MODEL), fixed),
            pl.BlockSpec((1, LANES), fixed),
            pl.BlockSpec((TM, TM), fixed),
            pl.BlockSpec((LANES, 3 * LANES), fixed),
            pl.BlockSpec((3 * LANES, LANES), fixed),
        ],
        out_specs=(
            pl.BlockSpec((TM, MAIN_W), row),
            pl.BlockSpec((1, 1, MAIN_W, TM), slab),
            pl.BlockSpec((1, 1, LANES, TM), slab),
            pl.BlockSpec((TM, LANES), row),
        ),
        scratch_shapes=[pltpu.VMEM((1, LANES), F32)],
        compiler_params=_cparams(("arbitrary", "arbitrary")),
        name="shared_kv",
    )(h, g, w, wvt, bf, tril, pqt, pk)


def _fox_proj_kernel(h_ref, g_ref, wqt_ref, wm_ref, qt_ref, qm_ref):
    xn = _rmsnorm(h_ref[...], g_ref[...]).astype(BF16)
    qt = (_dot_nt(wqt_ref[...], xn) * (FOX_DH ** -0.5 * LOG2E)).astype(BF16)
    for s in range(PROJ_TM // TM):
        qt_ref[s] = qt[:, s * TM:(s + 1) * TM]
    qm_ref[...] = _dot(xn, wm_ref[...]).astype(BF16)


def _fox_proj(h, g, wqt, wm, layer, j):
    n = h.shape[0]
    row = lambda i: (i, 0)
    return pl.pallas_call(
        _fox_proj_kernel,
        out_shape=(jax.ShapeDtypeStruct((n // TM, MAIN_W, TM), BF16), jax.ShapeDtypeStruct((n, MEM_W), BF16)),
        grid=(n // PROJ_TM,),
        in_specs=[
            pl.BlockSpec((PROJ_TM, D_MODEL), row),
            pl.BlockSpec((None, 1, D_MODEL), lambda i: (layer, 0, 0)),
            pl.BlockSpec((None, MAIN_W, D_MODEL), lambda i: (j, 0, 0)),
            pl.BlockSpec((None, D_MODEL, MEM_W), lambda i: (j, 0, 0)),
        ],
        out_specs=(pl.BlockSpec((PROJ_TM // TM, MAIN_W, TM), lambda i: (i, 0, 0)),
                   pl.BlockSpec((PROJ_TM, MEM_W), row)),
        compiler_params=_cparams(("parallel",)),
        name="fox_proj",
    )(h, g, wqt, wm)


FOX_HPS = 3


FOX_SUB = MXU_N
FOX_NSUB = FOX_BK // FOX_SUB


def _fox_attn_kernel(qt_ref, aqt_ref, k_ref, ak_ref, vt_ref, o_ref):
    qi = pl.program_id(2)
    chains = [(hh, qs) for hh in range(FOX_HPS) for qs in range(FOX_BQ // FOX_SUB)]

    def head(hh):
        return slice(hh * FOX_DH, (hh + 1) * FOX_DH)

    def strip(s):
        return slice(s * FOX_SUB, (s + 1) * FOX_SUB)

    slot = lax.broadcasted_iota(jnp.int32, (LANES, FOX_SUB), 0) // AUX_SLOT
    first_head = pl.program_id(1) * FOX_HPS
    qq = [jnp.concatenate([qt_ref[0, 0, head(hh), strip(qs)],
                           jnp.where(slot == first_head + hh, aqt_ref[0, 0, :, strip(qs)], jnp.zeros((), BF16))],
                          axis=0)
          for hh, qs in chains]

    def scores(kj, diagonal):
        c0 = pl.multiple_of(kj * FOX_BK, FOX_BK)
        sts = {}
        for ks in range(FOX_NSUB):
            rows = pl.ds(c0 + ks * FOX_SUB, FOX_SUB)
            for ci, (hh, qs) in enumerate(chains):
                if diagonal and ks > qs:
                    continue
                kk = jnp.concatenate([k_ref[0, rows, head(hh)], ak_ref[0, rows, :]], axis=1)
                st = _dot(kk, qq[ci])
                if diagonal and ks == qs:
                    key = lax.broadcasted_iota(jnp.int32, st.shape, 0)
                    qry = lax.broadcasted_iota(jnp.int32, st.shape, 1)
                    st = jnp.where(key <= qry, st, -jnp.inf)
                sts[ks, ci] = st
        return sts

    def update(kj, carry, sts):
        carry = list(carry)
        for ks in range(FOX_NSUB):
            for ci, (hh, qs) in enumerate(chains):
                if (ks, ci) not in sts:
                    continue
                m, acc = carry[ci]
                st = sts[ks, ci]
                m_new = jnp.maximum(m, jnp.max(st, axis=0, keepdims=True))
                alpha = jnp.exp2(m - m_new)
                p = jnp.exp2(st - m_new).astype(BF16)
                vt1 = jnp.concatenate([vt_ref[0, kj, head(hh), strip(ks)], ones], axis=0)
                acc = alpha * acc + _dot(vt1, p)
                carry[ci] = (m_new, acc)
        return tuple(carry)

    ones = jnp.ones((BF16_ROWS, FOX_SUB), BF16)
    init = tuple((jnp.full((1, FOX_SUB), -jnp.inf, F32), jnp.zeros((FOX_DH + BF16_ROWS, FOX_SUB), F32))
                 for _ in chains)

    carry = lax.fori_loop(0, qi, lambda kj, cr: update(kj, cr, scores(kj, False)), init)
    carry = update(qi, carry, scores(qi, True))
    for ci, (hh, qs) in enumerate(chains):
        _, acc = carry[ci]
        o_ref[0, strip(qs), head(hh)] = (acc[:FOX_DH] / acc[FOX_DH:FOX_DH + 1]).T.astype(BF16)


def _fox_attn(qt, aqt, k, ak, vt, batch, seq):
    w = FOX_HPS * FOX_DH
    nq = seq // FOX_BQ
    qblk = pl.BlockSpec((1, 1, w, FOX_BQ), lambda b, g, i: (b, i, g, 0))
    kblk = pl.BlockSpec((1, seq, w), lambda b, g, i: (b, 0, g))
    vblk = pl.BlockSpec((1, seq // FOX_BK, w, FOX_BK), lambda b, g, i: (b, 0, g, 0))
    aqblk = pl.BlockSpec((1, 1, LANES, FOX_BQ), lambda b, g, i: (b, i, 0, 0))
    akblk = pl.BlockSpec((1, seq, LANES), lambda b, g, i: (b, 0, 0))
    out = pl.pallas_call(
        _fox_attn_kernel,
        out_shape=jax.ShapeDtypeStruct((batch, seq, MAIN_W), BF16),
        grid=(batch, FOX_HEADS // FOX_HPS, nq),
        in_specs=[qblk, aqblk, kblk, akblk, vblk],
        out_specs=pl.BlockSpec((1, FOX_BQ, w), lambda b, g, i: (b, i, g)),
        compiler_params=_cparams(("parallel", "parallel", "arbitrary")),
        name="fox_attn",
    )(qt.reshape(batch, nq, MAIN_W, FOX_BQ), aqt, k.reshape(batch, seq, MAIN_W), ak.reshape(batch, seq, LANES), vt)
    return out.reshape(batch * seq, MAIN_W)


def _pad_heads_cols(w, heads, width, padded):
    lead = w.shape[:-1]
    w = w.reshape(lead + (heads, width))
    w = jnp.pad(w, [(0, 0)] * len(lead) + [(0, 0), (0, padded - width)])
    return w.reshape(lead + (heads * padded,))


def _pad_heads_rows(w, heads, width, padded):
    layers, _, n = w.shape
    w = w.reshape(layers, heads, width, n)
    w = jnp.pad(w, [(0, 0), (0, 0), (0, padded - width), (0, 0)])
    return w.reshape(layers, heads * padded, n)


def _gla_weights(w_in, w_gate_up, b_gate, g_head, w_out):
    dk = GLA_HEADS * GLA_DK_HEAD
    dv = GLA_HEADS * GLA_DV_HEAD
    o1, o2, o3, o4, o5 = dk, 2 * dk, 2 * dk + dv, 2 * dk + 2 * dv, 2 * dk + 2 * dv + GLA_RANK
    w = jnp.concatenate([
        _pad_heads_cols(w_in[..., :o1], GLA_HEADS, GLA_DK_HEAD, DKP),
        _pad_heads_cols(w_in[..., o1:o2], GLA_HEADS, GLA_DK_HEAD, DKP),
        _pad_heads_cols(w_in[..., o2:o3], GLA_HEADS, GLA_DV_HEAD, DVP),
        _pad_heads_cols(w_in[..., o3:o4], GLA_HEADS, GLA_DV_HEAD, DVP),
        w_in[..., o5:],
        jnp.pad(w_in[..., o4:o5], [(0, 0), (0, 0), (0, LANES - GLA_RANK)]),
    ], axis=-1).astype(BF16)
    wup = jnp.pad(_pad_heads_cols(w_gate_up, GLA_HEADS, GLA_DK_HEAD, DKP),
                  [(0, 0), (0, LANES - GLA_RANK), (0, 0)]).astype(BF16)
    bg = _pad_heads_cols(b_gate[:, None, :], GLA_HEADS, GLA_DK_HEAD, DKP)
    gh = jnp.pad(g_head[:, None, :], [(0, 0), (0, 0), (0, DVP - GLA_DV_HEAD)])
    wo = jnp.concatenate([_pad_heads_rows(w_out[:, :MAIN_W], GLA_HEADS, GLA_DV_HEAD, DVP), w_out[:, MAIN_W:]],
                         axis=1).astype(BF16)
    return w, wup, bg, gh, wo


def kernel(x, mem, norm_ffn, w_ffn_in, w_ffn_out, norm_mix, norm_mem, w_mem_kv, w_out, w_in_a, w_gate_up, b_gate,
           norm_gla_head, w_in_b, norm_shared, w_kv_shared, w_fgate, b_fgate, norm_final):
    batch, seq, _ = x.shape
    h = x.reshape(batch * seq, D_MODEL)

    w1 = w_ffn_in.astype(BF16)
    w2 = w_ffn_out.astype(BF16)
    g_ffn = norm_ffn[:, :, None, :]
    g_mix = norm_mix[:, None, :]
    wkt = jnp.swapaxes(w_mem_kv[:, :, :MEM_W], 1, 2).astype(BF16)
    wv = w_mem_kv[:, :, MEM_W:].astype(BF16)
    kb, vb = _mem_kv(mem, norm_mem[:, None, :], wkt, wv)
    w_a, wup, bg, gh, wo_a = _gla_weights(w_in_a, w_gate_up, b_gate, norm_gla_head, w_out[:N_A])
    wqt_b = jnp.swapaxes(w_in_b[:, :, :MAIN_W], 1, 2).astype(BF16)
    wm_b = w_in_b[:, :, MAIN_W:].astype(BF16)
    wo_b = w_out[N_A:].astype(BF16)

    k_sh = v_sh = aq_sh = ak_sh = None
    for l in range(DEPTH):
        if l == N_A:
            w_sh = jnp.concatenate([w_kv_shared[:, :MAIN_W], jnp.pad(w_fgate, [(0, 0), (0, LANES - FOX_HEADS)])],
                                   axis=1).astype(BF16)
            wvt = w_kv_shared[:, MAIN_W:].T.astype(BF16)
            bf = jnp.pad(b_fgate[None, :], [(0, 0), (0, LANES - FOX_HEADS)])
            k_sh, v_sh, aq_sh, ak_sh = _shared_kv(h, norm_shared[None, :], w_sh, wvt, bf, batch, seq)
        h = _ffn(h, g_ffn, w1, w2, l, 0)
        if l < N_A:
            q, k, v, r, la, q_mem = _gla_proj(h, g_mix, w_a, wup, bg, l, l)
            o_main = _gla(q, k, la, v, r, gh, l, batch, seq)
            h = _mix_out(h, o_main, q_mem, kb, vb, wo_a, l, l, seq)
        else:
            qt, q_mem = _fox_proj(h, g_mix, wqt_b, wm_b, l, l - N_A)
            o_main = _fox_attn(qt, aq_sh, k_sh, ak_sh, v_sh, batch, seq)
            h = _mix_out(h, o_main, q_mem, kb, vb, wo_b, l, l - N_A, seq)
        h = _ffn(h, g_ffn, w1, w2, l, 1, norm_final[None, :] if l == DEPTH - 1 else None)
    return h.reshape(batch, seq, D_MODEL)
```

```python
import functools

import numpy as np
import jax
import jax.numpy as jnp
from jax import lax
from jax.experimental import pallas as pl
from jax.experimental.pallas import tpu as pltpu

F32 = jnp.float32
BF16 = jnp.bfloat16

D_MODEL = 1024
DEPTH = 4
N_A = DEPTH // 2
MAIN_W = 768
MEM_W = 256
GLA_HEADS = 4
GLA_DK_HEAD = 96
GLA_DV_HEAD = 192
GLA_RANK = 16
GLA_TEMP = 16.0
FOX_HEADS = 6
FOX_DH = 128
MEM_HEADS = 4
MEM_DH = 64
MEM_LEN = 256
D_FF = 2816
EPS = 1e-6
LOG2E = 1.4426950408889634

LANES = 128
MXU_N = 256
BF16_ROWS = 16
VMEM_LIMIT = 56 * 1024 * 1024

DKP = LANES
DVP = MXU_N
GLA_QW = GLA_HEADS * DKP
GLA_VW = GLA_HEADS * DVP
GLA_CHUNK = MXU_N
GLA_LEVELS = 8

TM = 512
PROJ_TM = 1024
FOX_BQ = TM
FOX_BK = TM


def _cparams(sem):
    return pltpu.CompilerParams(dimension_semantics=sem, vmem_limit_bytes=VMEM_LIMIT)


def _rmsnorm(x, g):
    return x * lax.rsqrt(jnp.mean(x * x, axis=-1, keepdims=True) + EPS) * g


def _log_sigmoid(x):
    return jnp.minimum(x, 0.0) - jnp.log(1.0 + jnp.exp(-jnp.abs(x)))


def _sigmoid(x):
    return 1.0 / (1.0 + jnp.exp(-x))


def _split3(x):
    hi = x.astype(BF16)
    r1 = x - hi.astype(F32)
    mid = r1.astype(BF16)
    lo = (r1 - mid.astype(F32)).astype(BF16)
    return jnp.concatenate([hi, mid, lo], axis=1)


def _split2(x):
    hi = x.astype(BF16)
    lo = (x - hi.astype(F32)).astype(BF16)
    return jnp.concatenate([hi, lo], axis=1)


def _dot(a, b):
    return jnp.dot(a, b, preferred_element_type=F32)


def _dot_nt(a, b):
    return lax.dot_general(a, b, (((1,), (1,)), ((), ())), preferred_element_type=F32)


def _dot_tn(a, b):
    return lax.dot_general(a, b, (((0,), (0,)), ((), ())), preferred_element_type=F32)


def _exact_sum3(r, w):
    return r[:, :w] + r[:, w:2 * w] + r[:, 2 * w:]


FFN_BOUNDS = (0, 6 * MXU_N, D_FF)
FFN_TM = 1024


def _ffn_kernel(h_ref, g_ref, w1_ref, w2_ref, *rest):
    o_ref = rest[-1]
    h = h_ref[...]
    xn = _rmsnorm(h, g_ref[0, 0]).astype(BF16)
    y = None
    for f0, f1 in zip(FFN_BOUNDS[:-1], FFN_BOUNDS[1:]):
        a = _dot(xn, w1_ref[0, 0, :, f0:f1])
        c = _dot(xn, w1_ref[0, 0, :, D_FF + f0:D_FF + f1])
        act = (a * _sigmoid(a) * c).astype(BF16)
        part = _dot(act, w2_ref[0, 0, f0:f1, :])
        y = part if y is None else y + part
    out = h + 0.5 * y
    o_ref[...] = _rmsnorm(out, rest[0][...]) if len(rest) == 2 else out


def _ffn(h, g_all, w1_all, w2_all, layer, half, g_final=None):
    n = h.shape[0]
    pick = lambda i: (layer, half, 0, 0)
    in_specs = [
        pl.BlockSpec((FFN_TM, D_MODEL), lambda i: (i, 0)),
        pl.BlockSpec((1, 1, 1, D_MODEL), pick),
        pl.BlockSpec((1, 1, D_MODEL, 2 * D_FF), pick, pipeline_mode=pl.Buffered(1)),
        pl.BlockSpec((1, 1, D_FF, D_MODEL), pick, pipeline_mode=pl.Buffered(1)),
    ]
    args = [h, g_all, w1_all, w2_all]
    if g_final is not None:
        in_specs.append(pl.BlockSpec((1, D_MODEL), lambda i: (0, 0)))
        args.append(g_final)
    return pl.pallas_call(
        _ffn_kernel,
        out_shape=jax.ShapeDtypeStruct((n, D_MODEL), F32),
        grid=(n // FFN_TM,),
        in_specs=in_specs,
        out_specs=pl.BlockSpec((FFN_TM, D_MODEL), lambda i: (i, 0)),
        compiler_params=_cparams(("parallel",)),
        name="ffn",
    )(*args)


GLA_PW = 2 * GLA_QW + 2 * GLA_VW + MEM_W + LANES


def _gla_proj_kernel(h_ref, g_ref, w_ref, wup_ref, bg_ref, q_ref, k_ref, v_ref, r_ref, la_ref, qm_ref):
    xn = _rmsnorm(h_ref[...], g_ref[...]).astype(BF16)
    oq, ok, ov = 0, GLA_QW, 2 * GLA_QW
    orr, om, og = ov + GLA_VW, ov + 2 * GLA_VW, ov + 2 * GLA_VW + MEM_W
    g_low = _dot(xn, w_ref[:, og:og + LANES]).astype(BF16)
    q_ref[...] = (_dot(xn, w_ref[:, oq:oq + GLA_QW]) * (GLA_DK_HEAD ** -0.5)).astype(BF16)
    x = _dot(g_low, wup_ref[...]) + bg_ref[...]
    k_ref[...] = _dot(xn, w_ref[:, ok:ok + GLA_QW]).astype(BF16)
    la_ref[...] = _log_sigmoid(x) * (LOG2E / GLA_TEMP)
    v_ref[...] = _dot(xn, w_ref[:, ov:ov + GLA_VW]).astype(BF16)
    r_ref[...] = _dot(xn, w_ref[:, orr:orr + GLA_VW]).astype(BF16)
    qm_ref[...] = _dot(xn, w_ref[:, om:om + MEM_W]).astype(BF16)


def _gla_proj(h, g, w, wup, bg, layer, a):
    n = h.shape[0]
    row = lambda i: (i, 0)
    return pl.pallas_call(
        _gla_proj_kernel,
        out_shape=(
            jax.ShapeDtypeStruct((n, GLA_QW), BF16),
            jax.ShapeDtypeStruct((n, GLA_QW), BF16),
            jax.ShapeDtypeStruct((n, GLA_VW), BF16),
            jax.ShapeDtypeStruct((n, GLA_VW), BF16),
            jax.ShapeDtypeStruct((n, GLA_QW), F32),
            jax.ShapeDtypeStruct((n, MEM_W), BF16),
        ),
        grid=(n // PROJ_TM,),
        in_specs=[
            pl.BlockSpec((PROJ_TM, D_MODEL), row),
            pl.BlockSpec((None, 1, D_MODEL), lambda i: (layer, 0, 0)),
            pl.BlockSpec((None, D_MODEL, GLA_PW), lambda i: (a, 0, 0), pipeline_mode=pl.Buffered(1)),
            pl.BlockSpec((None, LANES, GLA_QW), lambda i: (a, 0, 0)),
            pl.BlockSpec((None, 1, GLA_QW), lambda i: (a, 0, 0)),
        ],
        out_specs=(
            pl.BlockSpec((PROJ_TM, GLA_QW), row),
            pl.BlockSpec((PROJ_TM, GLA_QW), row),
            pl.BlockSpec((PROJ_TM, GLA_VW), row),
            pl.BlockSpec((PROJ_TM, GLA_VW), row),
            pl.BlockSpec((PROJ_TM, GLA_QW), row),
            pl.BlockSpec((PROJ_TM, MEM_W), row),
        ),
        compiler_params=_cparams(("parallel",)),
        name="gla_proj",
    )(h, g, w, wup, bg)


GLA_TS = 1024


GLA_UNROLL = 2


def _gla_consts():
    c = GLA_CHUNK
    i = np.arange(c)[:, None]
    j = np.arange(c)[None, :]
    tril = (j <= i)
    b_rows, masks = [], []
    for lv in range(GLA_LEVELS):
        hs = 1 << lv
        mid = (i // (2 * hs)) * (2 * hs) + hs
        lower = (i % (2 * hs)) >= hs
        b_rows.append(np.where(lower, (j >= mid) & (j <= i), (j > i) & (j < mid)))
        masks.append(((i // (2 * hs)) == (j // (2 * hs))) & lower & ((j % (2 * hs)) < hs))
    masks.append(i == j)
    asb = lambda x: jnp.asarray(x.astype(np.float32), dtype=BF16)
    return asb(tril), asb(np.concatenate(b_rows, axis=0)), asb(np.stack(masks, axis=0))


def _gla_kernel(q_ref, k_ref, la_ref, v_ref, r_ref, gh_ref, tril_ref, b_ref, m_ref, o_ref, st_ref):
    c = GLA_CHUNK

    @pl.when(pl.program_id(1) == 0)
    def _():
        st_ref[...] = jnp.zeros_like(st_ref)

    gh = gh_ref[...]

    heads = [(slice(h * DKP, (h + 1) * DKP), slice(h * DVP, (h + 1) * DVP)) for h in range(GLA_HEADS)]

    def group(gi, carry):
        offs = [pl.multiple_of((gi * GLA_UNROLL + u) * c, c) for u in range(GLA_UNROLL)]
        decay = []
        for r0 in offs:
            g = la_ref[pl.ds(r0, c), :]
            cum = _dot(tril_ref[...], _split2(g))
            cum = cum[:, :GLA_QW] + cum[:, GLA_QW:]
            decay.append((cum, _dot(b_ref[...], g.astype(BF16))))
        intra = []
        for r0, (cum, lev) in zip(offs, decay):
            qa = q_ref[pl.ds(r0, c), :]
            ka = k_ref[pl.ds(r0, c), :]
            v = v_ref[pl.ds(r0, c), :]
            per_head = []
            for ks, vs in heads:
                qb, kb, vh = qa[:, ks], ka[:, ks], v[:, vs]
                cum_h = cum[:, ks]
                last_h = cum_h[c - 1:c, :]
                s = _dot_nt(qb, kb).astype(BF16) * m_ref[GLA_LEVELS]
                for lv in range(GLA_LEVELS):
                    e = jnp.exp2(lev[lv * c:(lv + 1) * c, ks]).astype(BF16)
                    s = s + _dot_nt(qb * e, kb * e).astype(BF16) * m_ref[lv]
                kd = (kb.astype(F32) * jnp.exp2(last_h - cum_h)).astype(BF16)
                qg = (qb.astype(F32) * jnp.exp2(cum_h)).astype(BF16)
                per_head.append((qg, s, _dot_tn(vh, kd), jnp.exp2(last_h)))
            intra.append(per_head)
        for r0, per_head in zip(offs, intra):
            v = v_ref[pl.ds(r0, c), :]
            r = r_ref[pl.ds(r0, c), :].astype(F32)
            for h, (ks, vs) in enumerate(heads):
                qg, s, upd, last_decay = per_head[h]
                st = st_ref[h]
                o = _dot_nt(qg, st.astype(BF16)) + _dot(s, v[:, vs])
                st_ref[h] = st * last_decay + upd
                ms = jnp.sum(o * o, axis=-1, keepdims=True) * (1.0 / GLA_DV_HEAD)
                rh = r[:, vs]
                y = o * lax.rsqrt(ms + EPS) * gh * (rh * _sigmoid(rh))
                o_ref[pl.ds(r0, c), vs] = y.astype(BF16)
        return carry

    lax.fori_loop(0, GLA_TS // (c * GLA_UNROLL), group, 0)


def _gla(q, k, la, v, r, gh, a, batch, seq):
    n = q.shape[0]
    nt = seq // GLA_TS
    tril, bmat, m = _gla_consts()
    row = lambda b, t: (b * nt + t, 0)
    fixed2 = lambda b, t: (0, 0)
    fixed3 = lambda b, t: (0, 0, 0)
    return pl.pallas_call(
        _gla_kernel,
        out_shape=jax.ShapeDtypeStruct((n, GLA_VW), BF16),
        grid=(batch, nt),
        in_specs=[
            pl.BlockSpec((GLA_TS, GLA_QW), row),
            pl.BlockSpec((GLA_TS, GLA_QW), row),
            pl.BlockSpec((GLA_TS, GLA_QW), row),
            pl.BlockSpec((GLA_TS, GLA_VW), row),
            pl.BlockSpec((GLA_TS, GLA_VW), row),
            pl.BlockSpec((None, 1, DVP), lambda b, t: (a, 0, 0)),
            pl.BlockSpec(tril.shape, fixed2),
            pl.BlockSpec(bmat.shape, fixed2),
            pl.BlockSpec(m.shape, fixed3),
        ],
        out_specs=pl.BlockSpec((GLA_TS, GLA_VW), row),
        scratch_shapes=[pltpu.VMEM((GLA_HEADS, DVP, DKP), F32)],
        compiler_params=_cparams(("arbitrary", "arbitrary")),
        name="gla",
    )(q, k, la, v, r, gh, tril, bmat, m)


MEM_BW = MEM_HEADS * MEM_LEN


def _mem_kv_kernel(mem_ref, g_ref, wkt_ref, wv_ref, kb_ref, vb_ref):
    d_row = lax.broadcasted_iota(jnp.int32, (MEM_W, MEM_LEN), 0) // MEM_DH
    d_col = lax.broadcasted_iota(jnp.int32, (MEM_LEN, MEM_W), 1) // MEM_DH
    for b in range(mem_ref.shape[0]):
        mn = _rmsnorm(mem_ref[b], g_ref[0]).astype(BF16)
        kt = _dot_nt(wkt_ref[0], mn) * (MEM_DH ** -0.5)
        v = _dot(mn, wv_ref[0])
        for hh in range(MEM_HEADS):
            kb_ref[0, b, :, hh * MEM_LEN:(hh + 1) * MEM_LEN] = jnp.where(d_row == hh, kt, 0.0).astype(BF16)
            vb_ref[0, b, hh * MEM_LEN:(hh + 1) * MEM_LEN, :] = jnp.where(d_col == hh, v, 0.0).astype(BF16)


def _mem_kv(mem, g, wkt, wv):
    b = mem.shape[0]
    return pl.pallas_call(
        _mem_kv_kernel,
        out_shape=(
            jax.ShapeDtypeStruct((DEPTH, b, MEM_W, MEM_BW), BF16),
            jax.ShapeDtypeStruct((DEPTH, b, MEM_BW, MEM_W), BF16),
        ),
        grid=(DEPTH,),
        in_specs=[
            pl.BlockSpec((b, MEM_LEN, D_MODEL), lambda l: (0, 0, 0)),
            pl.BlockSpec((1, 1, D_MODEL), lambda l: (l, 0, 0)),
            pl.BlockSpec((1, MEM_W, D_MODEL), lambda l: (l, 0, 0)),
            pl.BlockSpec((1, D_MODEL, MEM_W), lambda l: (l, 0, 0)),
        ],
        out_specs=(
            pl.BlockSpec((1, b, MEM_W, MEM_BW), lambda l: (l, 0, 0, 0)),
            pl.BlockSpec((1, b, MEM_BW, MEM_W), lambda l: (l, 0, 0, 0)),
        ),
        compiler_params=_cparams(("parallel",)),
        name="mem_kv",
    )(mem, g, wkt, wv)


def _mix_out_kernel(h_ref, om_ref, qm_ref, kb_ref, vb_ref, wo_ref, o_ref):
    lg = _dot(qm_ref[...], kb_ref[0, 0])
    ps = []
    for hh in range(MEM_HEADS):
        s = lg[:, hh * MEM_LEN:(hh + 1) * MEM_LEN]
        e = jnp.exp(s - jnp.max(s, axis=-1, keepdims=True))
        ps.append((e / jnp.sum(e, axis=-1, keepdims=True)).astype(BF16))
    o_mem = _dot(jnp.concatenate(ps, axis=1), vb_ref[0, 0])
    wm = om_ref.shape[1]
    y = _dot(om_ref[...], wo_ref[0:wm, :]) + _dot(o_mem.astype(BF16), wo_ref[wm:wm + MEM_W, :])
    o_ref[...] = h_ref[...] + y


def _mix_out(h, o_main, q_mem, kb, vb, wo, layer, j, seq):
    n = h.shape[0]
    wm = o_main.shape[1]
    per_b = seq // PROJ_TM
    row = lambda i: (i, 0)
    return pl.pallas_call(
        _mix_out_kernel,
        out_shape=jax.ShapeDtypeStruct((n, D_MODEL), F32),
        grid=(n // PROJ_TM,),
        in_specs=[
            pl.BlockSpec((PROJ_TM, D_MODEL), row),
            pl.BlockSpec((PROJ_TM, wm), row),
            pl.BlockSpec((PROJ_TM, MEM_W), row),
            pl.BlockSpec((1, 1, MEM_W, MEM_BW), lambda i: (layer, i // per_b, 0, 0)),
            pl.BlockSpec((1, 1, MEM_BW, MEM_W), lambda i: (layer, i // per_b, 0, 0)),
            pl.BlockSpec((None, wm + MEM_W, D_MODEL), lambda i: (j, 0, 0)),
        ],
        out_specs=pl.BlockSpec((PROJ_TM, D_MODEL), row),
        compiler_params=_cparams(("parallel",)),
        name="mix_out",
    )(h, o_main, q_mem, kb, vb, wo)


SH_PW = MAIN_W + LANES


ONE_LANE = LANES - 1
AUX_SLOT = BF16_ROWS


def _fox_aux_consts():
    pqt = np.zeros((LANES, 3 * LANES), np.float32)
    pk = np.zeros((3 * LANES, LANES), np.float32)
    for h in range(FOX_HEADS):
        for p in range(3):
            pqt[h * AUX_SLOT + p, p * LANES + h] = 1.0
            pqt[h * AUX_SLOT + 3 + p, ONE_LANE] = 1.0
            pk[ONE_LANE, h * AUX_SLOT + p] = 1.0
            pk[p * LANES + h, h * AUX_SLOT + 3 + p] = -1.0
    tril = np.tril(np.ones((TM, TM), np.float32))
    return jnp.asarray(tril, dtype=BF16), jnp.asarray(pqt, dtype=BF16), jnp.asarray(pk, dtype=BF16)


def _shared_kernel(h_ref, g_ref, w_ref, wvt_ref, bf_ref, tril_ref, pqt_ref, pk_ref,
                   k_ref, vt_ref, aqt_ref, ak_ref, carry_ref):
    @pl.when(pl.program_id(1) == 0)
    def _():
        carry_ref[...] = jnp.zeros_like(carry_ref)

    xn = _rmsnorm(h_ref[...], g_ref[...]).astype(BF16)
    log_f = _log_sigmoid(_dot(xn, w_ref[:, MAIN_W:SH_PW]) + bf_ref[...])
    k_ref[...] = _dot(xn, w_ref[:, 0:MAIN_W]).astype(BF16)
    d = _exact_sum3(_dot(tril_ref[...], _split3(log_f)), LANES) + carry_ref[...]
    carry_ref[...] = d[TM - 1:TM, :]
    d3 = _split3(d * LOG2E)
    lane = lax.broadcasted_iota(jnp.int32, d3.shape, 1)
    d3 = jnp.where(lane == ONE_LANE, jnp.ones_like(d3), d3)
    vt_ref[0, 0] = _dot_nt(wvt_ref[...], xn).astype(BF16)
    aqt_ref[0, 0] = _dot_nt(pqt_ref[...], d3).astype(BF16)
    ak_ref[...] = _dot(d3, pk_ref[...]).astype(BF16)


def _shared_kv(h, g, w, wvt, bf, batch, seq):
    n = h.shape[0]
    nt = seq // TM
    tril, pqt, pk = _fox_aux_consts()
    row = lambda b, t: (b * nt + t, 0)
    fixed = lambda b, t: (0, 0)
    slab = lambda b, t: (b, t, 0, 0)
    return pl.pallas_call(
        _shared_kernel,
        out_shape=(
            jax.ShapeDtypeStruct((n, MAIN_W), BF16),
            jax.ShapeDtypeStruct((batch, nt, MAIN_W, TM), BF16),
            jax.ShapeDtypeStruct((batch, nt, LANES, TM), BF16),
            jax.ShapeDtypeStruct((n, LANES), BF16),
        ),
        grid=(batch, nt),
        in_specs=[
            pl.BlockSpec((TM, D_MODEL), row),
            pl.BlockSpec((1, D_MODEL), fixed),
            pl.BlockSpec((D_MODEL, SH_PW), fixed),
            pl.BlockSpec((MAIN_W, D_MODEL), fixed),
            pl.BlockSpec((1, LANES), fixed),
            pl.BlockSpec((TM, TM), fixed),
            pl.BlockSpec((LANES, 3 * LANES), fixed),
            pl.BlockSpec((3 * LANES, LANES), fixed),
        ],
        out_specs=(
            pl.BlockSpec((TM, MAIN_W), row),
            pl.BlockSpec((1, 1, MAIN_W, TM), slab),
            pl.BlockSpec((1, 1, LANES, TM), slab),
            pl.BlockSpec((TM, LANES), row),
        ),
        scratch_shapes=[pltpu.VMEM((1, LANES), F32)],
        compiler_params=_cparams(("arbitrary", "arbitrary")),
        name="shared_kv",
    )(h, g, w, wvt, bf, tril, pqt, pk)


def _fox_proj_kernel(h_ref, g_ref, wqt_ref, wm_ref, qt_ref, qm_ref):
    xn = _rmsnorm(h_ref[...], g_ref[...]).astype(BF16)
    qt = (_dot_nt(wqt_ref[...], xn) * (FOX_DH ** -0.5 * LOG2E)).astype(BF16)
    for s in range(PROJ_TM // TM):
        qt_ref[s] = qt[:, s * TM:(s + 1) * TM]
    qm_ref[...] = _dot(xn, wm_ref[...]).astype(BF16)


def _fox_proj(h, g, wqt, wm, layer, j):
    n = h.shape[0]
    row = lambda i: (i, 0)
    return pl.pallas_call(
        _fox_proj_kernel,
        out_shape=(jax.ShapeDtypeStruct((n // TM, MAIN_W, TM), BF16), jax.ShapeDtypeStruct((n, MEM_W), BF16)),
        grid=(n // PROJ_TM,),
        in_specs=[
            pl.BlockSpec((PROJ_TM, D_MODEL), row),
            pl.BlockSpec((None, 1, D_MODEL), lambda i: (layer, 0, 0)),
            pl.BlockSpec((None, MAIN_W, D_MODEL), lambda i: (j, 0, 0)),
            pl.BlockSpec((None, D_MODEL, MEM_W), lambda i: (j, 0, 0)),
        ],
        out_specs=(pl.BlockSpec((PROJ_TM // TM, MAIN_W, TM), lambda i: (i, 0, 0)),
                   pl.BlockSpec((PROJ_TM, MEM_W), row)),
        compiler_params=_cparams(("parallel",)),
        name="fox_proj",
    )(h, g, wqt, wm)


FOX_HPS = 3


FOX_SUB = MXU_N
FOX_NSUB = FOX_BK // FOX_SUB


def _fox_attn_kernel(qt_ref, aqt_ref, k_ref, ak_ref, vt_ref, o_ref):
    qi = pl.program_id(2)
    chains = [(hh, qs) for hh in range(FOX_HPS) for qs in range(FOX_BQ // FOX_SUB)]

    def head(hh):
        return slice(hh * FOX_DH, (hh + 1) * FOX_DH)

    def strip(s):
        return slice(s * FOX_SUB, (s + 1) * FOX_SUB)

    slot = lax.broadcasted_iota(jnp.int32, (LANES, FOX_SUB), 0) // AUX_SLOT
    first_head = pl.program_id(1) * FOX_HPS
    qq = [jnp.concatenate([qt_ref[0, 0, head(hh), strip(qs)],
                           jnp.where(slot == first_head + hh, aqt_ref[0, 0, :, strip(qs)], jnp.zeros((), BF16))],
                          axis=0)
          for hh, qs in chains]

    def scores(kj, diagonal):
        c0 = pl.multiple_of(kj * FOX_BK, FOX_BK)
        sts = {}
        for ks in range(FOX_NSUB):
            rows = pl.ds(c0 + ks * FOX_SUB, FOX_SUB)
            for ci, (hh, qs) in enumerate(chains):
                if diagonal and ks > qs:
                    continue
                kk = jnp.concatenate([k_ref[0, rows, head(hh)], ak_ref[0, rows, :]], axis=1)
                st = _dot(kk, qq[ci])
                if diagonal and ks == qs:
                    key = lax.broadcasted_iota(jnp.int32, st.shape, 0)
                    qry = lax.broadcasted_iota(jnp.int32, st.shape, 1)
                    st = jnp.where(key <= qry, st, -jnp.inf)
                sts[ks, ci] = st
        return sts

    def update(kj, carry, sts):
        carry = list(carry)
        for ks in range(FOX_NSUB):
            for ci, (hh, qs) in enumerate(chains):
                if (ks, ci) not in sts:
                    continue
                m, acc = carry[ci]
                st = sts[ks, ci]
                m_new = jnp.maximum(m, jnp.max(st, axis=0, keepdims=True))
                alpha = jnp.exp2(m - m_new)
                p = jnp.exp2(st - m_new).astype(BF16)
                vt1 = jnp.concatenate([vt_ref[0, kj, head(hh), strip(ks)], ones], axis=0)
                acc = alpha * acc + _dot(vt1, p)
                carry[ci] = (m_new, acc)
        return tuple(carry)

    ones = jnp.ones((BF16_ROWS, FOX_SUB), BF16)
    init = tuple((jnp.full((1, FOX_SUB), -jnp.inf, F32), jnp.zeros((FOX_DH + BF16_ROWS, FOX_SUB), F32))
                 for _ in chains)

    carry = lax.fori_loop(0, qi, lambda kj, cr: update(kj, cr, scores(kj, False)), init)
    carry = update(qi, carry, scores(qi, True))
    for ci, (hh, qs) in enumerate(chains):
        _, acc = carry[ci]
        o_ref[0, strip(qs), head(hh)] = (acc[:FOX_DH] / acc[FOX_DH:FOX_DH + 1]).T.astype(BF16)


def _fox_attn(qt, aqt, k, ak, vt, batch, seq):
    w = FOX_HPS * FOX_DH
    nq = seq // FOX_BQ
    qblk = pl.BlockSpec((1, 1, w, FOX_BQ), lambda b, g, i: (b, i, g, 0))
    kblk = pl.BlockSpec((1, seq, w), lambda b, g, i: (b, 0, g))
    vblk = pl.BlockSpec((1, seq // FOX_BK, w, FOX_BK), lambda b, g, i: (b, 0, g, 0))
    aqblk = pl.BlockSpec((1, 1, LANES, FOX_BQ), lambda b, g, i: (b, i, 0, 0))
    akblk = pl.BlockSpec((1, seq, LANES), lambda b, g, i: (b, 0, 0))
    out = pl.pallas_call(
        _fox_attn_kernel,
        out_shape=jax.ShapeDtypeStruct((batch, seq, MAIN_W), BF16),
        grid=(batch, FOX_HEADS // FOX_HPS, nq),
        in_specs=[qblk, aqblk, kblk, akblk, vblk],
        out_specs=pl.BlockSpec((1, FOX_BQ, w), lambda b, g, i: (b, i, g)),
        compiler_params=_cparams(("parallel", "parallel", "arbitrary")),
        name="fox_attn",
    )(qt.reshape(batch, nq, MAIN_W, FOX_BQ), aqt, k.reshape(batch, seq, MAIN_W), ak.reshape(batch, seq, LANES), vt)
    return out.reshape(batch * seq, MAIN_W)


def _pad_heads_cols(w, heads, width, padded):
    lead = w.shape[:-1]
    w = w.reshape(lead + (heads, width))
    w = jnp.pad(w, [(0, 0)] * len(lead) + [(0, 0), (0, padded - width)])
    return w.reshape(lead + (heads * padded,))


def _pad_heads_rows(w, heads, width, padded):
    layers, _, n = w.shape
    w = w.reshape(layers, heads, width, n)
    w = jnp.pad(w, [(0, 0), (0, 0), (0, padded - width), (0, 0)])
    return w.reshape(layers, heads * padded, n)


def _gla_weights(w_in, w_gate_up, b_gate, g_head, w_out):
    dk = GLA_HEADS * GLA_DK_HEAD
    dv = GLA_HEADS * GLA_DV_HEAD
    o1, o2, o3, o4, o5 = dk, 2 * dk, 2 * dk + dv, 2 * dk + 2 * dv, 2 * dk + 2 * dv + GLA_RANK
    w = jnp.concatenate([
        _pad_heads_cols(w_in[..., :o1], GLA_HEADS, GLA_DK_HEAD, DKP),
        _pad_heads_cols(w_in[..., o1:o2], GLA_HEADS, GLA_DK_HEAD, DKP),
        _pad_heads_cols(w_in[..., o2:o3], GLA_HEADS, GLA_DV_HEAD, DVP),
        _pad_heads_cols(w_in[..., o3:o4], GLA_HEADS, GLA_DV_HEAD, DVP),
        w_in[..., o5:],
        jnp.pad(w_in[..., o4:o5], [(0, 0), (0, 0), (0, LANES - GLA_RANK)]),
    ], axis=-1).astype(BF16)
    wup = jnp.pad(_pad_heads_cols(w_gate_up, GLA_HEADS, GLA_DK_HEAD, DKP),
                  [(0, 0), (0, LANES - GLA_RANK), (0, 0)]).astype(BF16)
    bg = _pad_heads_cols(b_gate[:, None, :], GLA_HEADS, GLA_DK_HEAD, DKP)
    gh = jnp.pad(g_head[:, None, :], [(0, 0), (0, 0), (0, DVP - GLA_DV_HEAD)])
    wo = jnp.concatenate([_pad_heads_rows(w_out[:, :MAIN_W], GLA_HEADS, GLA_DV_HEAD, DVP), w_out[:, MAIN_W:]],
                         axis=1).astype(BF16)
    return w, wup, bg, gh, wo


def kernel(x, mem, norm_ffn, w_ffn_in, w_ffn_out, norm_mix, norm_mem, w_mem_kv, w_out, w_in_a, w_gate_up, b_gate,
           norm_gla_head, w_in_b, norm_shared, w_kv_shared, w_fgate, b_fgate, norm_final):
    batch, seq, _ = x.shape
    h = x.reshape(batch * seq, D_MODEL)

    w1 = w_ffn_in.astype(BF16)
    w2 = w_ffn_out.astype(BF16)
    g_ffn = norm_ffn[:, :, None, :]
    g_mix = norm_mix[:, None, :]
    wkt = jnp.swapaxes(w_mem_kv[:, :, :MEM_W], 1, 2).astype(BF16)
    wv = w_mem_kv[:, :, MEM_W:].astype(BF16)
    kb, vb = _mem_kv(mem, norm_mem[:, None, :], wkt, wv)
    w_a, wup, bg, gh, wo_a = _gla_weights(w_in_a, w_gate_up, b_gate, norm_gla_head, w_out[:N_A])
    wqt_b = jnp.swapaxes(w_in_b[:, :, :MAIN_W], 1, 2).astype(BF16)
    wm_b = w_in_b[:, :, MAIN_W:].astype(BF16)
    wo_b = w_out[N_A:].astype(BF16)

    k_sh = v_sh = aq_sh = ak_sh = None
    for l in range(DEPTH):
        if l == N_A:
            w_sh = jnp.concatenate([w_kv_shared[:, :MAIN_W], jnp.pad(w_fgate, [(0, 0), (0, LANES - FOX_HEADS)])],
                                   axis=1).astype(BF16)
            wvt = w_kv_shared[:, MAIN_W:].T.astype(BF16)
            bf = jnp.pad(b_fgate[None, :], [(0, 0), (0, LANES - FOX_HEADS)])
            k_sh, v_sh, aq_sh, ak_sh = _shared_kv(h, norm_shared[None, :], w_sh, wvt, bf, batch, seq)
        h = _ffn(h, g_ffn, w1, w2, l, 0)
        if l < N_A:
            q, k, v, r, la, q_mem = _gla_proj(h, g_mix, w_a, wup, bg, l, l)
            o_main = _gla(q, k, la, v, r, gh, l, batch, seq)
            h = _mix_out(h, o_main, q_mem, kb, vb, wo_a, l, l, seq)
        else:
            qt, q_mem = _fox_proj(h, g_mix, wqt_b, wm_b, l, l - N_A)
            o_main = _fox_attn(qt, aq_sh, k_sh, ak_sh, v_sh, batch, seq)
            h = _mix_out(h, o_main, q_mem, kb, vb, wo_b, l, l - N_A, seq)
        h = _ffn(h, g_ffn, w1, w2, l, 1, norm_final[None, :] if l == DEPTH - 1 else None)
    return h.reshape(batch, seq, D_MODEL)
```

```python
import numpy as np
import jax
import jax.numpy as jnp
from jax import lax
from jax.experimental import pallas as pl
from jax.experimental.pallas import tpu as pltpu

F32 = jnp.float32
BF16 = jnp.bfloat16

D_MODEL = 1024
DEPTH = 4
N_A = DEPTH // 2
MAIN_W = 768
MEM_W = 256
GLA_HEADS = 4
GLA_DK_HEAD = 96
GLA_DV_HEAD = 192
GLA_RANK = 16
GLA_TEMP = 16.0
FOX_HEADS = 6
FOX_DH = 128
MEM_HEADS = 4
MEM_DH = 64
MEM_LEN = 256
D_FF = 2816
EPS = 1e-6
LOG2E = 1.4426950408889634

LANES = 128
MXU_N = 256
BF16_ROWS = 16
VMEM_LIMIT = 56 * 1024 * 1024

DKP = LANES
DVP = MXU_N
GLA_QW = GLA_HEADS * DKP
GLA_VW = GLA_HEADS * DVP
GLA_CHUNK = MXU_N
GLA_LEVELS = 8
GLA_MM_LEVELS = 3

TM = 512
PROJ_TM = 1024
FOX_BQ = TM
FOX_BK = TM


def _cparams(sem):
    return pltpu.CompilerParams(dimension_semantics=sem, vmem_limit_bytes=VMEM_LIMIT)


def _rmsnorm(x, g):
    return x * lax.rsqrt(jnp.mean(x * x, axis=-1, keepdims=True) + EPS) * g


def _log_sigmoid(x):
    return jnp.minimum(x, 0.0) - jnp.log(1.0 + jnp.exp(-jnp.abs(x)))


def _sigmoid(x):
    return 1.0 / (1.0 + jnp.exp(-x))


def _split3(x):
    hi = x.astype(BF16)
    r1 = x - hi.astype(F32)
    mid = r1.astype(BF16)
    lo = (r1 - mid.astype(F32)).astype(BF16)
    return jnp.concatenate([hi, mid, lo], axis=1)


def _split2(x):
    hi = x.astype(BF16)
    lo = (x - hi.astype(F32)).astype(BF16)
    return jnp.concatenate([hi, lo], axis=1)


def _dot(a, b):
    return jnp.dot(a, b, preferred_element_type=F32)


def _dot_nt(a, b):
    return lax.dot_general(a, b, (((1,), (1,)), ((), ())), preferred_element_type=F32)


def _dot_tn(a, b):
    return lax.dot_general(a, b, (((0,), (0,)), ((), ())), preferred_element_type=F32)


def _exact_sum3(r, w):
    return r[:, :w] + r[:, w:2 * w] + r[:, 2 * w:]


FFN_BOUNDS = (0, 6 * MXU_N, D_FF)
FFN_TM = 1024


def _ffn_kernel(h_ref, g_ref, w1_ref, w2_ref, *rest):
    o_ref = rest[-1]
    h = h_ref[...]
    xn = _rmsnorm(h, g_ref[0, 0]).astype(BF16)
    y = None
    for f0, f1 in zip(FFN_BOUNDS[:-1], FFN_BOUNDS[1:]):
        a = _dot(xn, w1_ref[0, 0, :, f0:f1])
        c = _dot(xn, w1_ref[0, 0, :, D_FF + f0:D_FF + f1])
        act = (a * _sigmoid(a) * c).astype(BF16)
        part = _dot(act, w2_ref[0, 0, f0:f1, :])
        y = part if y is None else y + part
    out = h + 0.5 * y
    o_ref[...] = _rmsnorm(out, rest[0][...]) if len(rest) == 2 else out


def _ffn(h, g_all, w1_all, w2_all, layer, half, g_final=None):
    n = h.shape[0]
    pick = lambda i: (layer, half, 0, 0)
    in_specs = [
        pl.BlockSpec((FFN_TM, D_MODEL), lambda i: (i, 0)),
        pl.BlockSpec((1, 1, 1, D_MODEL), pick),
        pl.BlockSpec((1, 1, D_MODEL, 2 * D_FF), pick, pipeline_mode=pl.Buffered(1)),
        pl.BlockSpec((1, 1, D_FF, D_MODEL), pick, pipeline_mode=pl.Buffered(1)),
    ]
    args = [h, g_all, w1_all, w2_all]
    if g_final is not None:
        in_specs.append(pl.BlockSpec((1, D_MODEL), lambda i: (0, 0)))
        args.append(g_final)
    return pl.pallas_call(
        _ffn_kernel,
        out_shape=jax.ShapeDtypeStruct((n, D_MODEL), F32),
        grid=(n // FFN_TM,),
        in_specs=in_specs,
        out_specs=pl.BlockSpec((FFN_TM, D_MODEL), lambda i: (i, 0)),
        compiler_params=_cparams(("parallel",)),
        name="ffn",
    )(*args)


GLA_PW = 2 * GLA_QW + 2 * GLA_VW + MEM_W + LANES


def _gla_proj_kernel(h_ref, g_ref, w_ref, wup_ref, bg_ref, q_ref, k_ref, v_ref, r_ref, la_ref, qm_ref):
    xn = _rmsnorm(h_ref[...], g_ref[...]).astype(BF16)
    oq, ok, ov = 0, GLA_QW, 2 * GLA_QW
    orr, om, og = ov + GLA_VW, ov + 2 * GLA_VW, ov + 2 * GLA_VW + MEM_W
    g_low = _dot(xn, w_ref[:, og:og + LANES]).astype(BF16)
    q_ref[...] = (_dot(xn, w_ref[:, oq:oq + GLA_QW]) * (GLA_DK_HEAD ** -0.5)).astype(BF16)
    x = _dot(g_low, wup_ref[...]) + bg_ref[...]
    k_ref[...] = _dot(xn, w_ref[:, ok:ok + GLA_QW]).astype(BF16)
    la_ref[...] = _log_sigmoid(x) * (LOG2E / GLA_TEMP)
    v_ref[...] = _dot(xn, w_ref[:, ov:ov + GLA_VW]).astype(BF16)
    r_ref[...] = _dot(xn, w_ref[:, orr:orr + GLA_VW]).astype(BF16)
    qm_ref[...] = _dot(xn, w_ref[:, om:om + MEM_W]).astype(BF16)


def _gla_proj(h, g, w, wup, bg, layer, a):
    n = h.shape[0]
    row = lambda i: (i, 0)
    return pl.pallas_call(
        _gla_proj_kernel,
        out_shape=(
            jax.ShapeDtypeStruct((n, GLA_QW), BF16),
            jax.ShapeDtypeStruct((n, GLA_QW), BF16),
            jax.ShapeDtypeStruct((n, GLA_VW), BF16),
            jax.ShapeDtypeStruct((n, GLA_VW), BF16),
            jax.ShapeDtypeStruct((n, GLA_QW), F32),
            jax.ShapeDtypeStruct((n, MEM_W), BF16),
        ),
        grid=(n // PROJ_TM,),
        in_specs=[
            pl.BlockSpec((PROJ_TM, D_MODEL), row),
            pl.BlockSpec((None, 1, D_MODEL), lambda i: (layer, 0, 0)),
            pl.BlockSpec((None, D_MODEL, GLA_PW), lambda i: (a, 0, 0), pipeline_mode=pl.Buffered(1)),
            pl.BlockSpec((None, LANES, GLA_QW), lambda i: (a, 0, 0)),
            pl.BlockSpec((None, 1, GLA_QW), lambda i: (a, 0, 0)),
        ],
        out_specs=(
            pl.BlockSpec((PROJ_TM, GLA_QW), row),
            pl.BlockSpec((PROJ_TM, GLA_QW), row),
            pl.BlockSpec((PROJ_TM, GLA_VW), row),
            pl.BlockSpec((PROJ_TM, GLA_VW), row),
            pl.BlockSpec((PROJ_TM, GLA_QW), row),
            pl.BlockSpec((PROJ_TM, MEM_W), row),
        ),
        compiler_params=_cparams(("parallel",)),
        name="gla_proj",
    )(h, g, w, wup, bg)


GLA_TS = 1024
GLA_UNROLL = 2


def _gla_consts():
    c = GLA_CHUNK
    i = np.arange(c)[:, None]
    j = np.arange(c)[None, :]
    tril = (j <= i)
    b_rows, masks = [], []
    for lv in range(GLA_LEVELS):
        hs = 1 << lv
        mid = (i // (2 * hs)) * (2 * hs) + hs
        lower = (i % (2 * hs)) >= hs
        b_rows.append(np.where(lower, (j >= mid) & (j <= i), (j > i) & (j < mid)))
        masks.append(((i // (2 * hs)) == (j // (2 * hs))) & lower & ((j % (2 * hs)) < hs))
    masks.append(i == j)
    asb = lambda x: jnp.asarray(x.astype(np.float32), dtype=BF16)
    return asb(tril), asb(np.concatenate(b_rows[:GLA_MM_LEVELS], axis=0)), asb(np.stack(masks, axis=0))


def _gla_kernel(q_ref, k_ref, la_ref, v_ref, r_ref, gh_ref, tril_ref, b_ref, m_ref, o_ref, st_ref):
    c = GLA_CHUNK

    @pl.when(pl.program_id(1) == 0)
    def _():
        st_ref[...] = jnp.zeros_like(st_ref)

    gh = gh_ref[...]

    heads = [(slice(h * DKP, (h + 1) * DKP), slice(h * DVP, (h + 1) * DVP)) for h in range(GLA_HEADS)]

    def decays(r0):
        g = la_ref[pl.ds(r0, c), :]
        cum = _dot(tril_ref[...], _split2(g))
        cum = cum[:, :GLA_QW] + cum[:, GLA_QW:]
        small = _dot(b_ref[...], g.astype(BF16))
        lev = [small[lv * c:(lv + 1) * c] for lv in range(GLA_MM_LEVELS)]
        for lv in range(GLA_MM_LEVELS, GLA_LEVELS):
            hs = 1 << lv
            parts = []
            for mid in range(hs, c, 2 * hs):
                parts += [cum[mid - 1:mid, :] - cum[mid - hs:mid, :], cum[mid:mid + hs, :] - cum[mid - 1:mid, :]]
            lev.append(jnp.concatenate(parts, axis=0))
        return cum, lev

    def intra(r0, cum, lev):
        qa = q_ref[pl.ds(r0, c), :]
        ka = k_ref[pl.ds(r0, c), :]
        v = v_ref[pl.ds(r0, c), :]
        per_head = []
        for ks, vs in heads:
            qb, kb, vh = qa[:, ks], ka[:, ks], v[:, vs]
            cum_h = cum[:, ks]
            last_h = cum_h[c - 1:c, :]
            s = _dot_nt(qb, kb).astype(BF16) * m_ref[GLA_LEVELS]
            for lv in range(GLA_LEVELS):
                e = jnp.exp2(lev[lv][:, ks]).astype(BF16)
                s = s + _dot_nt(qb * e, kb * e).astype(BF16) * m_ref[lv]
            kd = (kb.astype(F32) * jnp.exp2(last_h - cum_h)).astype(BF16)
            qg = (qb.astype(F32) * jnp.exp2(cum_h)).astype(BF16)
            per_head.append((qg, s, _dot_tn(vh, kd), jnp.exp2(last_h)))
        return per_head

    def recur(r0, per_head):
        v = v_ref[pl.ds(r0, c), :]
        r = r_ref[pl.ds(r0, c), :].astype(F32)
        for h, (ks, vs) in enumerate(heads):
            qg, s, upd, last_decay = per_head[h]
            st = st_ref[h]
            o = _dot_nt(qg, st.astype(BF16)) + _dot(s, v[:, vs])
            st_ref[h] = st * last_decay + upd
            ms = jnp.sum(o * o, axis=-1, keepdims=True) * (1.0 / GLA_DV_HEAD)
            rh = r[:, vs]
            y = o * lax.rsqrt(ms + EPS) * gh * (rh * _sigmoid(rh))
            o_ref[pl.ds(r0, c), vs] = y.astype(BF16)

    def group(gi, carry):
        offs = [pl.multiple_of((gi * GLA_UNROLL + u) * c, c) for u in range(GLA_UNROLL)]
        dec = [decays(r0) for r0 in offs]
        cur = [intra(r0, *d) for r0, d in zip(offs, dec)]
        for r0, per_head in zip(offs, cur):
            recur(r0, per_head)
        return carry

    lax.fori_loop(0, GLA_TS // (c * GLA_UNROLL), group, 0)


def _gla(q, k, la, v, r, gh, a, batch, seq):
    n = q.shape[0]
    nt = seq // GLA_TS
    tril, bmat, m = _gla_consts()
    row = lambda b, t: (b * nt + t, 0)
    fixed2 = lambda b, t: (0, 0)
    fixed3 = lambda b, t: (0, 0, 0)
    return pl.pallas_call(
        _gla_kernel,
        out_shape=jax.ShapeDtypeStruct((n, GLA_VW), BF16),
        grid=(batch, nt),
        in_specs=[
            pl.BlockSpec((GLA_TS, GLA_QW), row),
            pl.BlockSpec((GLA_TS, GLA_QW), row),
            pl.BlockSpec((GLA_TS, GLA_QW), row),
            pl.BlockSpec((GLA_TS, GLA_VW), row),
            pl.BlockSpec((GLA_TS, GLA_VW), row),
            pl.BlockSpec((None, 1, DVP), lambda b, t: (a, 0, 0)),
            pl.BlockSpec(tril.shape, fixed2),
            pl.BlockSpec(bmat.shape, fixed2),
            pl.BlockSpec(m.shape, fixed3),
        ],
        out_specs=pl.BlockSpec((GLA_TS, GLA_VW), row),
        scratch_shapes=[pltpu.VMEM((GLA_HEADS, DVP, DKP), F32)],
        compiler_params=_cparams(("arbitrary", "arbitrary")),
        name="gla",
    )(q, k, la, v, r, gh, tril, bmat, m)


MEM_BW = MEM_HEADS * MEM_LEN


def _mem_kv_kernel(mem_ref, g_ref, wkt_ref, wv_ref, kb_ref, vb_ref):
    d_row = lax.broadcasted_iota(jnp.int32, (MEM_W, MEM_LEN), 0) // MEM_DH
    d_col = lax.broadcasted_iota(jnp.int32, (MEM_LEN, MEM_W), 1) // MEM_DH
    for b in range(mem_ref.shape[0]):
        mn = _rmsnorm(mem_ref[b], g_ref[0]).astype(BF16)
        kt = _dot_nt(wkt_ref[0], mn) * (MEM_DH ** -0.5)
        v = _dot(mn, wv_ref[0])
        for hh in range(MEM_HEADS):
            kb_ref[0, b, :, hh * MEM_LEN:(hh + 1) * MEM_LEN] = jnp.where(d_row == hh, kt, 0.0).astype(BF16)
            vb_ref[0, b, hh * MEM_LEN:(hh + 1) * MEM_LEN, :] = jnp.where(d_col == hh, v, 0.0).astype(BF16)


def _mem_kv(mem, g, wkt, wv):
    b = mem.shape[0]
    return pl.pallas_call(
        _mem_kv_kernel,
        out_shape=(
            jax.ShapeDtypeStruct((DEPTH, b, MEM_W, MEM_BW), BF16),
            jax.ShapeDtypeStruct((DEPTH, b, MEM_BW, MEM_W), BF16),
        ),
        grid=(DEPTH,),
        in_specs=[
            pl.BlockSpec((b, MEM_LEN, D_MODEL), lambda l: (0, 0, 0)),
            pl.BlockSpec((1, 1, D_MODEL), lambda l: (l, 0, 0)),
            pl.BlockSpec((1, MEM_W, D_MODEL), lambda l: (l, 0, 0)),
            pl.BlockSpec((1, D_MODEL, MEM_W), lambda l: (l, 0, 0)),
        ],
        out_specs=(
            pl.BlockSpec((1, b, MEM_W, MEM_BW), lambda l: (l, 0, 0, 0)),
            pl.BlockSpec((1, b, MEM_BW, MEM_W), lambda l: (l, 0, 0, 0)),
        ),
        compiler_params=_cparams(("parallel",)),
        name="mem_kv",
    )(mem, g, wkt, wv)


def _mix_out_kernel(h_ref, om_ref, qm_ref, kb_ref, vb_ref, wo_ref, o_ref):
    lg = _dot(qm_ref[...], kb_ref[0, 0])
    ps = []
    for hh in range(MEM_HEADS):
        s = lg[:, hh * MEM_LEN:(hh + 1) * MEM_LEN]
        e = jnp.exp(s - jnp.max(s, axis=-1, keepdims=True))
        ps.append((e / jnp.sum(e, axis=-1, keepdims=True)).astype(BF16))
    o_mem = _dot(jnp.concatenate(ps, axis=1), vb_ref[0, 0])
    wm = om_ref.shape[1]
    y = _dot(om_ref[...], wo_ref[0:wm, :]) + _dot(o_mem.astype(BF16), wo_ref[wm:wm + MEM_W, :])
    o_ref[...] = h_ref[...] + y


def _mix_out(h, o_main, q_mem, kb, vb, wo, layer, j, seq):
    n = h.shape[0]
    wm = o_main.shape[1]
    per_b = seq // PROJ_TM
    row = lambda i: (i, 0)
    return pl.pallas_call(
        _mix_out_kernel,
        out_shape=jax.ShapeDtypeStruct((n, D_MODEL), F32),
        grid=(n // PROJ_TM,),
        in_specs=[
            pl.BlockSpec((PROJ_TM, D_MODEL), row),
            pl.BlockSpec((PROJ_TM, wm), row),
            pl.BlockSpec((PROJ_TM, MEM_W), row),
            pl.BlockSpec((1, 1, MEM_W, MEM_BW), lambda i: (layer, i // per_b, 0, 0)),
            pl.BlockSpec((1, 1, MEM_BW, MEM_W), lambda i: (layer, i // per_b, 0, 0)),
            pl.BlockSpec((None, wm + MEM_W, D_MODEL), lambda i: (j, 0, 0)),
        ],
        out_specs=pl.BlockSpec((PROJ_TM, D_MODEL), row),
        compiler_params=_cparams(("parallel",)),
        name="mix_out",
    )(h, o_main, q_mem, kb, vb, wo)


SH_PW = MAIN_W + LANES


ONE_LANE = LANES - 1
AUX_SLOT = BF16_ROWS


def _fox_aux_consts():
    pqt = np.zeros((LANES, 3 * LANES), np.float32)
    pk = np.zeros((3 * LANES, LANES), np.float32)
    for h in range(FOX_HEADS):
        for p in range(3):
            pqt[h * AUX_SLOT + p, p * LANES + h] = 1.0
            pqt[h * AUX_SLOT + 3 + p, ONE_LANE] = 1.0
            pk[ONE_LANE, h * AUX_SLOT + p] = 1.0
            pk[p * LANES + h, h * AUX_SLOT + 3 + p] = -1.0
    tril = np.tril(np.ones((TM, TM), np.float32))
    return jnp.asarray(tril, dtype=BF16), jnp.asarray(pqt, dtype=BF16), jnp.asarray(pk, dtype=BF16)


def _shared_kernel(h_ref, g_ref, w_ref, wvt_ref, bf_ref, tril_ref, pqt_ref, pk_ref,
                   k_ref, vt_ref, aqt_ref, ak_ref, carry_ref):
    @pl.when(pl.program_id(1) == 0)
    def _():
        carry_ref[...] = jnp.zeros_like(carry_ref)

    xn = _rmsnorm(h_ref[...], g_ref[...]).astype(BF16)
    log_f = _log_sigmoid(_dot(xn, w_ref[:, MAIN_W:SH_PW]) + bf_ref[...])
    k_ref[...] = _dot(xn, w_ref[:, 0:MAIN_W]).astype(BF16)
    d = _exact_sum3(_dot(tril_ref[...], _split3(log_f)), LANES) + carry_ref[...]
    carry_ref[...] = d[TM - 1:TM, :]
    d3 = _split3(d * LOG2E)
    lane = lax.broadcasted_iota(jnp.int32, d3.shape, 1)
    d3 = jnp.where(lane == ONE_LANE, jnp.ones_like(d3), d3)
    vt_ref[0, 0] = _dot_nt(wvt_ref[...], xn).astype(BF16)
    aqt_ref[0, 0] = _dot_nt(pqt_ref[...], d3).astype(BF16)
    ak_ref[...] = _dot(d3, pk_ref[...]).astype(BF16)


def _shared_kv(h, g, w, wvt, bf, batch, seq):
    n = h.shape[0]
    nt = seq // TM
    tril, pqt, pk = _fox_aux_consts()
    row = lambda b, t: (b * nt + t, 0)
    fixed = lambda b, t: (0, 0)
    slab = lambda b, t: (b, t, 0, 0)
    return pl.pallas_call(
        _shared_kernel,
        out_shape=(
            jax.ShapeDtypeStruct((n, MAIN_W), BF16),
            jax.ShapeDtypeStruct((batch, nt, MAIN_W, TM), BF16),
            jax.ShapeDtypeStruct((batch, nt, LANES, TM), BF16),
            jax.ShapeDtypeStruct((n, LANES), BF16),
        ),
        grid=(batch, nt),
        in_specs=[
            pl.BlockSpec((TM, D_MODEL), row),
            pl.BlockSpec((1, D_MODEL), fixed),
            pl.BlockSpec((D_MODEL, SH_PW), fixed),
            pl.BlockSpec((MAIN_W, D_MODEL), fixed),
            pl.BlockSpec((1, LANES), fixed),
            pl.BlockSpec((TM, TM), fixed),
            pl.BlockSpec((LANES, 3 * LANES), fixed),
            pl.BlockSpec((3 * LANES, LANES), fixed),
        ],
        out_specs=(
            pl.BlockSpec((TM, MAIN_W), row),
            pl.BlockSpec((1, 1, MAIN_W, TM), slab),
            pl.BlockSpec((1, 1, LANES, TM), slab),
            pl.BlockSpec((TM, LANES), row),
        ),
        scratch_shapes=[pltpu.VMEM((1, LANES), F32)],
        compiler_params=_cparams(("arbitrary", "arbitrary")),
        name="shared_kv",
    )(h, g, w, wvt, bf, tril, pqt, pk)


def _fox_proj_kernel(h_ref, g_ref, wqt_ref, wm_ref, qt_ref, qm_ref):
    xn = _rmsnorm(h_ref[...], g_ref[...]).astype(BF16)
    qt = (_dot_nt(wqt_ref[...], xn) * (FOX_DH ** -0.5 * LOG2E)).astype(BF16)
    for s in range(PROJ_TM // TM):
        qt_ref[s] = qt[:, s * TM:(s + 1) * TM]
    qm_ref[...] = _dot(xn, wm_ref[...]).astype(BF16)


def _fox_proj(h, g, wqt, wm, layer, j):
    n = h.shape[0]
    row = lambda i: (i, 0)
    return pl.pallas_call(
        _fox_proj_kernel,
        out_shape=(jax.ShapeDtypeStruct((n // TM, MAIN_W, TM), BF16), jax.ShapeDtypeStruct((n, MEM_W), BF16)),
        grid=(n // PROJ_TM,),
        in_specs=[
            pl.BlockSpec((PROJ_TM, D_MODEL), row),
            pl.BlockSpec((None, 1, D_MODEL), lambda i: (layer, 0, 0)),
            pl.BlockSpec((None, MAIN_W, D_MODEL), lambda i: (j, 0, 0)),
            pl.BlockSpec((None, D_MODEL, MEM_W), lambda i: (j, 0, 0)),
        ],
        out_specs=(pl.BlockSpec((PROJ_TM // TM, MAIN_W, TM), lambda i: (i, 0, 0)),
                   pl.BlockSpec((PROJ_TM, MEM_W), row)),
        compiler_params=_cparams(("parallel",)),
        name="fox_proj",
    )(h, g, wqt, wm)


FOX_HPS = 3


FOX_SUB = MXU_N
FOX_NSUB = FOX_BK // FOX_SUB


def _fox_attn_kernel(qt_ref, aqt_ref, k_ref, ak_ref, vt_ref, o_ref):
    qi = pl.program_id(2)
    chains = [(hh, qs) for hh in range(FOX_HPS) for qs in range(FOX_BQ // FOX_SUB)]

    def head(hh):
        return slice(hh * FOX_DH, (hh + 1) * FOX_DH)

    def strip(s):
        return slice(s * FOX_SUB, (s + 1) * FOX_SUB)

    slot = lax.broadcasted_iota(jnp.int32, (LANES, FOX_SUB), 0) // AUX_SLOT
    first_head = pl.program_id(1) * FOX_HPS
    qq = [jnp.concatenate([qt_ref[0, 0, head(hh), strip(qs)],
                           jnp.where(slot == first_head + hh, aqt_ref[0, 0, :, strip(qs)], jnp.zeros((), BF16))],
                          axis=0)
          for hh, qs in chains]

    def scores(kj, diagonal):
        c0 = pl.multiple_of(kj * FOX_BK, FOX_BK)
        sts = {}
        for ks in range(FOX_NSUB):
            rows = pl.ds(c0 + ks * FOX_SUB, FOX_SUB)
            ak = ak_ref[0, rows, :]
            kk = [jnp.concatenate([k_ref[0, rows, head(hh)], ak], axis=1) for hh in range(FOX_HPS)]
            for ci, (hh, qs) in enumerate(chains):
                if diagonal and ks > qs:
                    continue
                st = _dot(kk[hh], qq[ci])
                if diagonal and ks == qs:
                    key = lax.broadcasted_iota(jnp.int32, st.shape, 0)
                    qry = lax.broadcasted_iota(jnp.int32, st.shape, 1)
                    st = jnp.where(key <= qry, st, -jnp.inf)
                sts[ks, ci] = st
        return sts

    def update(kj, carry, sts):
        carry = list(carry)
        for ks in range(FOX_NSUB):
            vt1 = [jnp.concatenate([vt_ref[0, kj, head(hh), strip(ks)], ones], axis=0) for hh in range(FOX_HPS)]
            for ci, (hh, qs) in enumerate(chains):
                if (ks, ci) not in sts:
                    continue
                m, acc = carry[ci]
                st = sts[ks, ci]
                m_new = jnp.maximum(m, jnp.max(st, axis=0, keepdims=True))
                alpha = jnp.exp2(m - m_new)
                p = jnp.exp2(st - m_new).astype(BF16)
                acc = alpha * acc + _dot(vt1[hh], p)
                carry[ci] = (m_new, acc)
        return tuple(carry)

    ones = jnp.ones((BF16_ROWS, FOX_SUB), BF16)
    init = tuple((jnp.full((1, FOX_SUB), -jnp.inf, F32), jnp.zeros((FOX_DH + BF16_ROWS, FOX_SUB), F32))
                 for _ in chains)

    carry = lax.fori_loop(0, qi, lambda kj, cr: update(kj, cr, scores(kj, False)), init)
    carry = update(qi, carry, scores(qi, True))
    for ci, (hh, qs) in enumerate(chains):
        _, acc = carry[ci]
        o_ref[0, strip(qs), head(hh)] = (acc[:FOX_DH] / acc[FOX_DH:FOX_DH + 1]).T.astype(BF16)


def _fox_attn(qt, aqt, k, ak, vt, batch, seq):
    w = FOX_HPS * FOX_DH
    nq = seq // FOX_BQ
    qblk = pl.BlockSpec((1, 1, w, FOX_BQ), lambda b, g, i: (b, i, g, 0))
    kblk = pl.BlockSpec((1, seq, w), lambda b, g, i: (b, 0, g))
    vblk = pl.BlockSpec((1, seq // FOX_BK, w, FOX_BK), lambda b, g, i: (b, 0, g, 0))
    aqblk = pl.BlockSpec((1, 1, LANES, FOX_BQ), lambda b, g, i: (b, i, 0, 0))
    akblk = pl.BlockSpec((1, seq, LANES), lambda b, g, i: (b, 0, 0))
    out = pl.pallas_call(
        _fox_attn_kernel,
        out_shape=jax.ShapeDtypeStruct((batch, seq, MAIN_W), BF16),
        grid=(batch, FOX_HEADS // FOX_HPS, nq),
        in_specs=[qblk, aqblk, kblk, akblk, vblk],
        out_specs=pl.BlockSpec((1, FOX_BQ, w), lambda b, g, i: (b, i, g)),
        compiler_params=_cparams(("parallel", "parallel", "arbitrary")),
        name="fox_attn",
    )(qt.reshape(batch, nq, MAIN_W, FOX_BQ), aqt, k.reshape(batch, seq, MAIN_W), ak.reshape(batch, seq, LANES), vt)
    return out.reshape(batch * seq, MAIN_W)


def _pad_heads_cols(w, heads, width, padded):
    lead = w.shape[:-1]
    w = w.reshape(lead + (heads, width))
    w = jnp.pad(w, [(0, 0)] * len(lead) + [(0, 0), (0, padded - width)])
    return w.reshape(lead + (heads * padded,))


def _pad_heads_rows(w, heads, width, padded):
    layers, _, n = w.shape
    w = w.reshape(layers, heads, width, n)
    w = jnp.pad(w, [(0, 0), (0, 0), (0, padded - width), (0, 0)])
    return w.reshape(layers, heads * padded, n)


def _gla_weights(w_in, w_gate_up, b_gate, g_head, w_out):
    dk = GLA_HEADS * GLA_DK_HEAD
    dv = GLA_HEADS * GLA_DV_HEAD
    o1, o2, o3, o4, o5 = dk, 2 * dk, 2 * dk + dv, 2 * dk + 2 * dv, 2 * dk + 2 * dv + GLA_RANK
    w = jnp.concatenate([
        _pad_heads_cols(w_in[..., :o1], GLA_HEADS, GLA_DK_HEAD, DKP),
        _pad_heads_cols(w_in[..., o1:o2], GLA_HEADS, GLA_DK_HEAD, DKP),
        _pad_heads_cols(w_in[..., o2:o3], GLA_HEADS, GLA_DV_HEAD, DVP),
        _pad_heads_cols(w_in[..., o3:o4], GLA_HEADS, GLA_DV_HEAD, DVP),
        w_in[..., o5:],
        jnp.pad(w_in[..., o4:o5], [(0, 0), (0, 0), (0, LANES - GLA_RANK)]),
    ], axis=-1).astype(BF16)
    wup = jnp.pad(_pad_heads_cols(w_gate_up, GLA_HEADS, GLA_DK_HEAD, DKP),
                  [(0, 0), (0, LANES - GLA_RANK), (0, 0)]).astype(BF16)
    bg = _pad_heads_cols(b_gate[:, None, :], GLA_HEADS, GLA_DK_HEAD, DKP)
    gh = jnp.pad(g_head[:, None, :], [(0, 0), (0, 0), (0, DVP - GLA_DV_HEAD)])
    wo = jnp.concatenate([_pad_heads_rows(w_out[:, :MAIN_W], GLA_HEADS, GLA_DV_HEAD, DVP), w_out[:, MAIN_W:]],
                         axis=1).astype(BF16)
    return w, wup, bg, gh, wo


def kernel(x, mem, norm_ffn, w_ffn_in, w_ffn_out, norm_mix, norm_mem, w_mem_kv, w_out, w_in_a, w_gate_up, b_gate,
           norm_gla_head, w_in_b, norm_shared, w_kv_shared, w_fgate, b_fgate, norm_final):
    batch, seq, _ = x.shape
    h = x.reshape(batch * seq, D_MODEL)

    w1 = w_ffn_in.astype(BF16)
    w2 = w_ffn_out.astype(BF16)
    g_ffn = norm_ffn[:, :, None, :]
    g_mix = norm_mix[:, None, :]
    wkt = jnp.swapaxes(w_mem_kv[:, :, :MEM_W], 1, 2).astype(BF16)
    wv = w_mem_kv[:, :, MEM_W:].astype(BF16)
    kb, vb = _mem_kv(mem, norm_mem[:, None, :], wkt, wv)
    w_a, wup, bg, gh, wo_a = _gla_weights(w_in_a, w_gate_up, b_gate, norm_gla_head, w_out[:N_A])
    wqt_b = jnp.swapaxes(w_in_b[:, :, :MAIN_W], 1, 2).astype(BF16)
    wm_b = w_in_b[:, :, MAIN_W:].astype(BF16)
    wo_b = w_out[N_A:].astype(BF16)

    k_sh = v_sh = aq_sh = ak_sh = None
    for l in range(DEPTH):
        if l == N_A:
            w_sh = jnp.concatenate([w_kv_shared[:, :MAIN_W], jnp.pad(w_fgate, [(0, 0), (0, LANES - FOX_HEADS)])],
                                   axis=1).astype(BF16)
            wvt = w_kv_shared[:, MAIN_W:].T.astype(BF16)
            bf = jnp.pad(b_fgate[None, :], [(0, 0), (0, LANES - FOX_HEADS)])
            k_sh, v_sh, aq_sh, ak_sh = _shared_kv(h, norm_shared[None, :], w_sh, wvt, bf, batch, seq)
        h = _ffn(h, g_ffn, w1, w2, l, 0)
        if l < N_A:
            q, k, v, r, la, q_mem = _gla_proj(h, g_mix, w_a, wup, bg, l, l)
            o_main = _gla(q, k, la, v, r, gh, l, batch, seq)
            h = _mix_out(h, o_main, q_mem, kb, vb, wo_a, l, l, seq)
        else:
            qt, q_mem = _fox_proj(h, g_mix, wqt_b, wm_b, l, l - N_A)
            o_main = _fox_attn(qt, aq_sh, k_sh, ak_sh, v_sh, batch, seq)
            h = _mix_out(h, o_main, q_mem, kb, vb, wo_b, l, l - N_A, seq)
        h = _ffn(h, g_ffn, w1, w2, l, 1, norm_final[None, :] if l == DEPTH - 1 else None)
    return h.reshape(batch, seq, D_MODEL)
```

```python
import numpy as np
import jax
import jax.numpy as jnp
from jax import lax
from jax.experimental import pallas as pl
from jax.experimental.pallas import tpu as pltpu

F32 = jnp.float32
BF16 = jnp.bfloat16

D_MODEL = 1024
DEPTH = 4
N_A = DEPTH // 2
MAIN_W = 768
MEM_W = 256
GLA_HEADS = 4
GLA_DK_HEAD = 96
GLA_DV_HEAD = 192
GLA_RANK = 16
GLA_TEMP = 16.0
FOX_HEADS = 6
FOX_DH = 128
MEM_HEADS = 4
MEM_DH = 64
MEM_LEN = 256
D_FF = 2816
EPS = 1e-6
LOG2E = 1.4426950408889634

LANES = 128
MXU_N = 256
BF16_ROWS = 16
VMEM_LIMIT = 56 * 1024 * 1024

DKP = LANES
DVP = MXU_N
GLA_QW = GLA_HEADS * DKP
GLA_VW = GLA_HEADS * DVP
GLA_CHUNK = MXU_N
GLA_LEVELS = 8
GLA_MM_LEVELS = 3

TM = 512
PROJ_TM = 1024
FOX_BQ = TM
FOX_BK = TM


def _cparams(sem):
    return pltpu.CompilerParams(dimension_semantics=sem, vmem_limit_bytes=VMEM_LIMIT)


def _rmsnorm(x, g):
    return x * lax.rsqrt(jnp.mean(x * x, axis=-1, keepdims=True) + EPS) * g


def _log_sigmoid(x):
    return jnp.minimum(x, 0.0) - jnp.log(1.0 + jnp.exp(-jnp.abs(x)))


def _sigmoid(x):
    return 1.0 / (1.0 + jnp.exp(-x))


def _split3(x):
    hi = x.astype(BF16)
    r1 = x - hi.astype(F32)
    mid = r1.astype(BF16)
    lo = (r1 - mid.astype(F32)).astype(BF16)
    return jnp.concatenate([hi, mid, lo], axis=1)


def _split2(x):
    hi = x.astype(BF16)
    lo = (x - hi.astype(F32)).astype(BF16)
    return jnp.concatenate([hi, lo], axis=1)


def _dot(a, b):
    return jnp.dot(a, b, preferred_element_type=F32)


def _dot_nt(a, b):
    return lax.dot_general(a, b, (((1,), (1,)), ((), ())), preferred_element_type=F32)


def _dot_tn(a, b):
    return lax.dot_general(a, b, (((0,), (0,)), ((), ())), preferred_element_type=F32)


def _exact_sum3(r, w):
    return r[:, :w] + r[:, w:2 * w] + r[:, 2 * w:]


FFN_BOUNDS = (0, 6 * MXU_N, D_FF)
FFN_TM = 1024


def _ffn_kernel(h_ref, g_ref, w1_ref, w2_ref, *rest):
    o_ref = rest[-1]
    h = h_ref[...]
    xn = _rmsnorm(h, g_ref[0, 0]).astype(BF16)
    y = None
    for f0, f1 in zip(FFN_BOUNDS[:-1], FFN_BOUNDS[1:]):
        a = _dot(xn, w1_ref[0, 0, :, f0:f1])
        c = _dot(xn, w1_ref[0, 0, :, D_FF + f0:D_FF + f1])
        act = (a * _sigmoid(a) * c).astype(BF16)
        part = _dot(act, w2_ref[0, 0, f0:f1, :])
        y = part if y is None else y + part
    out = h + 0.5 * y
    o_ref[...] = _rmsnorm(out, rest[0][...]) if len(rest) == 2 else out


def _ffn(h, g_all, w1_all, w2_all, layer, half, g_final=None):
    n = h.shape[0]
    pick = lambda i: (layer, half, 0, 0)
    in_specs = [
        pl.BlockSpec((FFN_TM, D_MODEL), lambda i: (i, 0)),
        pl.BlockSpec((1, 1, 1, D_MODEL), pick),
        pl.BlockSpec((1, 1, D_MODEL, 2 * D_FF), pick, pipeline_mode=pl.Buffered(1)),
        pl.BlockSpec((1, 1, D_FF, D_MODEL), pick, pipeline_mode=pl.Buffered(1)),
    ]
    args = [h, g_all, w1_all, w2_all]
    if g_final is not None:
        in_specs.append(pl.BlockSpec((1, D_MODEL), lambda i: (0, 0)))
        args.append(g_final)
    return pl.pallas_call(
        _ffn_kernel,
        out_shape=jax.ShapeDtypeStruct((n, D_MODEL), F32),
        grid=(n // FFN_TM,),
        in_specs=in_specs,
        out_specs=pl.BlockSpec((FFN_TM, D_MODEL), lambda i: (i, 0)),
        compiler_params=_cparams(("parallel",)),
        name="ffn",
    )(*args)


GLA_PW = 2 * GLA_QW + 2 * GLA_VW + MEM_W + LANES


def _gla_proj_kernel(h_ref, g_ref, w_ref, wup_ref, bg_ref, q_ref, k_ref, v_ref, r_ref, la_ref, qm_ref):
    xn = _rmsnorm(h_ref[...], g_ref[...]).astype(BF16)
    oq, ok, ov = 0, GLA_QW, 2 * GLA_QW
    orr, om, og = ov + GLA_VW, ov + 2 * GLA_VW, ov + 2 * GLA_VW + MEM_W
    g_low = _dot(xn, w_ref[:, og:og + LANES]).astype(BF16)
    q_ref[...] = (_dot(xn, w_ref[:, oq:oq + GLA_QW]) * (GLA_DK_HEAD ** -0.5)).astype(BF16)
    x = _dot(g_low, wup_ref[...]) + bg_ref[...]
    k_ref[...] = _dot(xn, w_ref[:, ok:ok + GLA_QW]).astype(BF16)
    la_ref[...] = _log_sigmoid(x) * (LOG2E / GLA_TEMP)
    v_ref[...] = _dot(xn, w_ref[:, ov:ov + GLA_VW]).astype(BF16)
    r_ref[...] = _dot(xn, w_ref[:, orr:orr + GLA_VW]).astype(BF16)
    qm_ref[...] = _dot(xn, w_ref[:, om:om + MEM_W]).astype(BF16)


def _gla_proj(h, g, w, wup, bg, layer, a):
    n = h.shape[0]
    row = lambda i: (i, 0)
    return pl.pallas_call(
        _gla_proj_kernel,
        out_shape=(
            jax.ShapeDtypeStruct((n, GLA_QW), BF16),
            jax.ShapeDtypeStruct((n, GLA_QW), BF16),
            jax.ShapeDtypeStruct((n, GLA_VW), BF16),
            jax.ShapeDtypeStruct((n, GLA_VW), BF16),
            jax.ShapeDtypeStruct((n, GLA_QW), F32),
            jax.ShapeDtypeStruct((n, MEM_W), BF16),
        ),
        grid=(n // PROJ_TM,),
        in_specs=[
            pl.BlockSpec((PROJ_TM, D_MODEL), row),
            pl.BlockSpec((None, 1, D_MODEL), lambda i: (layer, 0, 0)),
            pl.BlockSpec((None, D_MODEL, GLA_PW), lambda i: (a, 0, 0), pipeline_mode=pl.Buffered(1)),
            pl.BlockSpec((None, LANES, GLA_QW), lambda i: (a, 0, 0)),
            pl.BlockSpec((None, 1, GLA_QW), lambda i: (a, 0, 0)),
        ],
        out_specs=(
            pl.BlockSpec((PROJ_TM, GLA_QW), row),
            pl.BlockSpec((PROJ_TM, GLA_QW), row),
            pl.BlockSpec((PROJ_TM, GLA_VW), row),
            pl.BlockSpec((PROJ_TM, GLA_VW), row),
            pl.BlockSpec((PROJ_TM, GLA_QW), row),
            pl.BlockSpec((PROJ_TM, MEM_W), row),
        ),
        compiler_params=_cparams(("parallel",)),
        name="gla_proj",
    )(h, g, w, wup, bg)


GLA_TS = 1024
GLA_UNROLL = 2


def _gla_consts():
    c = GLA_CHUNK
    i = np.arange(c)[:, None]
    j = np.arange(c)[None, :]
    tril = (j <= i)
    b_rows, masks = [], []
    for lv in range(GLA_LEVELS):
        hs = 1 << lv
        mid = (i // (2 * hs)) * (2 * hs) + hs
        lower = (i % (2 * hs)) >= hs
        b_rows.append(np.where(lower, (j >= mid) & (j <= i), (j > i) & (j < mid)))
        masks.append(((i // (2 * hs)) == (j // (2 * hs))) & lower & ((j % (2 * hs)) < hs))
    masks.append(i == j)
    asb = lambda x: jnp.asarray(x.astype(np.float32), dtype=BF16)
    return asb(tril), asb(np.concatenate(b_rows[:GLA_MM_LEVELS], axis=0)), asb(np.stack(masks, axis=0))


def _gla_kernel(q_ref, k_ref, la_ref, v_ref, r_ref, gh_ref, tril_ref, b_ref, m_ref, o_ref, st_ref):
    c = GLA_CHUNK

    @pl.when(pl.program_id(1) == 0)
    def _():
        st_ref[...] = jnp.zeros_like(st_ref)

    gh = gh_ref[...]

    heads = [(slice(h * DKP, (h + 1) * DKP), slice(h * DVP, (h + 1) * DVP)) for h in range(GLA_HEADS)]

    def decays(r0):
        g = la_ref[pl.ds(r0, c), :]
        cum = _dot(tril_ref[...], _split2(g))
        cum = cum[:, :GLA_QW] + cum[:, GLA_QW:]
        small = _dot(b_ref[...], g.astype(BF16))
        lev = [small[lv * c:(lv + 1) * c] for lv in range(GLA_MM_LEVELS)]
        for lv in range(GLA_MM_LEVELS, GLA_LEVELS):
            hs = 1 << lv
            parts = []
            for mid in range(hs, c, 2 * hs):
                parts += [cum[mid - 1:mid, :] - cum[mid - hs:mid, :], cum[mid:mid + hs, :] - cum[mid - 1:mid, :]]
            lev.append(jnp.concatenate(parts, axis=0))
        return cum, lev

    def intra(r0, cum, lev):
        qa = q_ref[pl.ds(r0, c), :]
        ka = k_ref[pl.ds(r0, c), :]
        v = v_ref[pl.ds(r0, c), :]
        per_head = []
        for ks, vs in heads:
            qb, kb, vh = qa[:, ks], ka[:, ks], v[:, vs]
            cum_h = cum[:, ks]
            last_h = cum_h[c - 1:c, :]
            s = _dot_nt(qb, kb).astype(BF16) * m_ref[GLA_LEVELS]
            for lv in range(GLA_LEVELS):
                e = jnp.exp2(lev[lv][:, ks]).astype(BF16)
                s = s + _dot_nt(qb * e, kb * e).astype(BF16) * m_ref[lv]
            kd = (kb.astype(F32) * jnp.exp2(last_h - cum_h)).astype(BF16)
            qg = (qb.astype(F32) * jnp.exp2(cum_h)).astype(BF16)
            per_head.append((qg, s, _dot_tn(vh, kd), jnp.exp2(last_h)))
        return per_head

    def recur(r0, per_head):
        v = v_ref[pl.ds(r0, c), :]
        r = r_ref[pl.ds(r0, c), :].astype(F32)
        for h, (ks, vs) in enumerate(heads):
            qg, s, upd, last_decay = per_head[h]
            st = st_ref[h]
            o = _dot_nt(qg, st.astype(BF16)) + _dot(s, v[:, vs])
            st_ref[h] = st * last_decay + upd
            ms = jnp.sum(o * o, axis=-1, keepdims=True) * (1.0 / GLA_DV_HEAD)
            rh = r[:, vs]
            y = o * lax.rsqrt(ms + EPS) * gh * (rh * _sigmoid(rh))
            o_ref[pl.ds(r0, c), vs] = y.astype(BF16)

    def group(gi, carry):
        offs = [pl.multiple_of((gi * GLA_UNROLL + u) * c, c) for u in range(GLA_UNROLL)]
        dec = [decays(r0) for r0 in offs]
        cur = [intra(r0, *d) for r0, d in zip(offs, dec)]
        for r0, per_head in zip(offs, cur):
            recur(r0, per_head)
        return carry

    lax.fori_loop(0, GLA_TS // (c * GLA_UNROLL), group, 0)


def _gla(q, k, la, v, r, gh, a, batch, seq):
    n = q.shape[0]
    nt = seq // GLA_TS
    tril, bmat, m = _gla_consts()
    row = lambda b, t: (b * nt + t, 0)
    fixed2 = lambda b, t: (0, 0)
    fixed3 = lambda b, t: (0, 0, 0)
    return pl.pallas_call(
        _gla_kernel,
        out_shape=jax.ShapeDtypeStruct((n, GLA_VW), BF16),
        grid=(batch, nt),
        in_specs=[
            pl.BlockSpec((GLA_TS, GLA_QW), row),
            pl.BlockSpec((GLA_TS, GLA_QW), row),
            pl.BlockSpec((GLA_TS, GLA_QW), row),
            pl.BlockSpec((GLA_TS, GLA_VW), row),
            pl.BlockSpec((GLA_TS, GLA_VW), row),
            pl.BlockSpec((None, 1, DVP), lambda b, t: (a, 0, 0)),
            pl.BlockSpec(tril.shape, fixed2),
            pl.BlockSpec(bmat.shape, fixed2),
            pl.BlockSpec(m.shape, fixed3),
        ],
        out_specs=pl.BlockSpec((GLA_TS, GLA_VW), row),
        scratch_shapes=[pltpu.VMEM((GLA_HEADS, DVP, DKP), F32)],
        compiler_params=_cparams(("arbitrary", "arbitrary")),
        name="gla",
    )(q, k, la, v, r, gh, tril, bmat, m)


MEM_BW = MEM_HEADS * MEM_LEN


def _mem_kv_kernel(mem_ref, g_ref, wkt_ref, wv_ref, kb_ref, vb_ref):
    d_row = lax.broadcasted_iota(jnp.int32, (MEM_W, MEM_LEN), 0) // MEM_DH
    d_col = lax.broadcasted_iota(jnp.int32, (MEM_LEN, MEM_W), 1) // MEM_DH
    for b in range(mem_ref.shape[0]):
        mn = _rmsnorm(mem_ref[b], g_ref[0]).astype(BF16)
        kt = _dot_nt(wkt_ref[0], mn) * (MEM_DH ** -0.5)
        v = _dot(mn, wv_ref[0])
        for hh in range(MEM_HEADS):
            kb_ref[0, b, :, hh * MEM_LEN:(hh + 1) * MEM_LEN] = jnp.where(d_row == hh, kt, 0.0).astype(BF16)
            vb_ref[0, b, hh * MEM_LEN:(hh + 1) * MEM_LEN, :] = jnp.where(d_col == hh, v, 0.0).astype(BF16)


def _mem_kv(mem, g, wkt, wv):
    b = mem.shape[0]
    return pl.pallas_call(
        _mem_kv_kernel,
        out_shape=(
            jax.ShapeDtypeStruct((DEPTH, b, MEM_W, MEM_BW), BF16),
            jax.ShapeDtypeStruct((DEPTH, b, MEM_BW, MEM_W), BF16),
        ),
        grid=(DEPTH,),
        in_specs=[
            pl.BlockSpec((b, MEM_LEN, D_MODEL), lambda l: (0, 0, 0)),
            pl.BlockSpec((1, 1, D_MODEL), lambda l: (l, 0, 0)),
            pl.BlockSpec((1, MEM_W, D_MODEL), lambda l: (l, 0, 0)),
            pl.BlockSpec((1, D_MODEL, MEM_W), lambda l: (l, 0, 0)),
        ],
        out_specs=(
            pl.BlockSpec((1, b, MEM_W, MEM_BW), lambda l: (l, 0, 0, 0)),
            pl.BlockSpec((1, b, MEM_BW, MEM_W), lambda l: (l, 0, 0, 0)),
        ),
        compiler_params=_cparams(("parallel",)),
        name="mem_kv",
    )(mem, g, wkt, wv)


def _mix_out_kernel(h_ref, om_ref, qm_ref, kb_ref, vb_ref, wo_ref, o_ref):
    lg = _dot(qm_ref[...], kb_ref[0, 0])
    ps = []
    for hh in range(MEM_HEADS):
        s = lg[:, hh * MEM_LEN:(hh + 1) * MEM_LEN]
        e = jnp.exp(s - jnp.max(s, axis=-1, keepdims=True))
        ps.append((e / jnp.sum(e, axis=-1, keepdims=True)).astype(BF16))
    o_mem = _dot(jnp.concatenate(ps, axis=1), vb_ref[0, 0])
    wm = om_ref.shape[1]
    y = _dot(om_ref[...], wo_ref[0:wm, :]) + _dot(o_mem.astype(BF16), wo_ref[wm:wm + MEM_W, :])
    o_ref[...] = h_ref[...] + y


def _mix_out(h, o_main, q_mem, kb, vb, wo, layer, j, seq):
    n = h.shape[0]
    wm = o_main.shape[1]
    per_b = seq // PROJ_TM
    row = lambda i: (i, 0)
    return pl.pallas_call(
        _mix_out_kernel,
        out_shape=jax.ShapeDtypeStruct((n, D_MODEL), F32),
        grid=(n // PROJ_TM,),
        in_specs=[
            pl.BlockSpec((PROJ_TM, D_MODEL), row),
            pl.BlockSpec((PROJ_TM, wm), row),
            pl.BlockSpec((PROJ_TM, MEM_W), row),
            pl.BlockSpec((1, 1, MEM_W, MEM_BW), lambda i: (layer, i // per_b, 0, 0)),
            pl.BlockSpec((1, 1, MEM_BW, MEM_W), lambda i: (layer, i // per_b, 0, 0)),
            pl.BlockSpec((None, wm + MEM_W, D_MODEL), lambda i: (j, 0, 0)),
        ],
        out_specs=pl.BlockSpec((PROJ_TM, D_MODEL), row),
        compiler_params=_cparams(("parallel",)),
        name="mix_out",
    )(h, o_main, q_mem, kb, vb, wo)


SH_PW = MAIN_W + LANES


ONE_LANE = LANES - 1
AUX_SLOT = BF16_ROWS


def _fox_aux_consts():
    pqt = np.zeros((LANES, 3 * LANES), np.float32)
    pk = np.zeros((3 * LANES, LANES), np.float32)
    for h in range(FOX_HEADS):
        for p in range(3):
            pqt[h * AUX_SLOT + p, p * LANES + h] = 1.0
            pqt[h * AUX_SLOT + 3 + p, ONE_LANE] = 1.0
            pk[ONE_LANE, h * AUX_SLOT + p] = 1.0
            pk[p * LANES + h, h * AUX_SLOT + 3 + p] = -1.0
    tril = np.tril(np.ones((TM, TM), np.float32))
    return jnp.asarray(tril, dtype=BF16), jnp.asarray(pqt, dtype=BF16), jnp.asarray(pk, dtype=BF16)


def _shared_kernel(h_ref, g_ref, w_ref, wvt_ref, bf_ref, tril_ref, pqt_ref, pk_ref,
                   k_ref, vt_ref, aqt_ref, ak_ref, carry_ref):
    @pl.when(pl.program_id(1) == 0)
    def _():
        carry_ref[...] = jnp.zeros_like(carry_ref)

    xn = _rmsnorm(h_ref[...], g_ref[...]).astype(BF16)
    log_f = _log_sigmoid(_dot(xn, w_ref[:, MAIN_W:SH_PW]) + bf_ref[...])
    k_ref[...] = _dot(xn, w_ref[:, 0:MAIN_W]).astype(BF16)
    d = _exact_sum3(_dot(tril_ref[...], _split3(log_f)), LANES) + carry_ref[...]
    carry_ref[...] = d[TM - 1:TM, :]
    d3 = _split3(d * LOG2E)
    lane = lax.broadcasted_iota(jnp.int32, d3.shape, 1)
    d3 = jnp.where(lane == ONE_LANE, jnp.ones_like(d3), d3)
    vt_ref[0, 0] = _dot_nt(wvt_ref[...], xn).astype(BF16)
    aqt_ref[0, 0] = _dot_nt(pqt_ref[...], d3).astype(BF16)
    ak_ref[...] = _dot(d3, pk_ref[...]).astype(BF16)


def _shared_kv(h, g, w, wvt, bf, batch, seq):
    n = h.shape[0]
    nt = seq // TM
    tril, pqt, pk = _fox_aux_consts()
    row = lambda b, t: (b * nt + t, 0)
    fixed = lambda b, t: (0, 0)
    slab = lambda b, t: (b, t, 0, 0)
    return pl.pallas_call(
        _shared_kernel,
        out_shape=(
            jax.ShapeDtypeStruct((n, MAIN_W), BF16),
            jax.ShapeDtypeStruct((batch, nt, MAIN_W, TM), BF16),
            jax.ShapeDtypeStruct((batch, nt, LANES, TM), BF16),
            jax.ShapeDtypeStruct((n, LANES), BF16),
        ),
        grid=(batch, nt),
        in_specs=[
            pl.BlockSpec((TM, D_MODEL), row),
            pl.BlockSpec((1, D_MODEL), fixed),
            pl.BlockSpec((D_MODEL, SH_PW), fixed),
            pl.BlockSpec((MAIN_W, D_MODEL), fixed),
            pl.BlockSpec((1, LANES), fixed),
            pl.BlockSpec((TM, TM), fixed),
            pl.BlockSpec((LANES, 3 * LANES), fixed),
            pl.BlockSpec((3 * LANES, LANES), fixed),
        ],
        out_specs=(
            pl.BlockSpec((TM, MAIN_W), row),
            pl.BlockSpec((1, 1, MAIN_W, TM), slab),
            pl.BlockSpec((1, 1, LANES, TM), slab),
            pl.BlockSpec((TM, LANES), row),
        ),
        scratch_shapes=[pltpu.VMEM((1, LANES), F32)],
        compiler_params=_cparams(("arbitrary", "arbitrary")),
        name="shared_kv",
    )(h, g, w, wvt, bf, tril, pqt, pk)


def _fox_proj_kernel(h_ref, g_ref, wqt_ref, wm_ref, qt_ref, qm_ref):
    xn = _rmsnorm(h_ref[...], g_ref[...]).astype(BF16)
    qt = (_dot_nt(wqt_ref[...], xn) * (FOX_DH ** -0.5 * LOG2E)).astype(BF16)
    for s in range(PROJ_TM // TM):
        qt_ref[s] = qt[:, s * TM:(s + 1) * TM]
    qm_ref[...] = _dot(xn, wm_ref[...]).astype(BF16)


def _fox_proj(h, g, wqt, wm, layer, j):
    n = h.shape[0]
    row = lambda i: (i, 0)
    return pl.pallas_call(
        _fox_proj_kernel,
        out_shape=(jax.ShapeDtypeStruct((n // TM, MAIN_W, TM), BF16), jax.ShapeDtypeStruct((n, MEM_W), BF16)),
        grid=(n // PROJ_TM,),
        in_specs=[
            pl.BlockSpec((PROJ_TM, D_MODEL), row),
            pl.BlockSpec((None, 1, D_MODEL), lambda i: (layer, 0, 0)),
            pl.BlockSpec((None, MAIN_W, D_MODEL), lambda i: (j, 0, 0)),
            pl.BlockSpec((None, D_MODEL, MEM_W), lambda i: (j, 0, 0)),
        ],
        out_specs=(pl.BlockSpec((PROJ_TM // TM, MAIN_W, TM), lambda i: (i, 0, 0)),
                   pl.BlockSpec((PROJ_TM, MEM_W), row)),
        compiler_params=_cparams(("parallel",)),
        name="fox_proj",
    )(h, g, wqt, wm)


FOX_HPS = 3


FOX_SUB = MXU_N
FOX_NSUB = FOX_BK // FOX_SUB


def _fox_attn_kernel(qt_ref, aqt_ref, k_ref, ak_ref, vt_ref, o_ref):
    qi = pl.program_id(2)
    chains = [(hh, qs) for hh in range(FOX_HPS) for qs in range(FOX_BQ // FOX_SUB)]

    def head(hh):
        return slice(hh * FOX_DH, (hh + 1) * FOX_DH)

    def strip(s):
        return slice(s * FOX_SUB, (s + 1) * FOX_SUB)

    slot = lax.broadcasted_iota(jnp.int32, (LANES, FOX_SUB), 0) // AUX_SLOT
    first_head = pl.program_id(1) * FOX_HPS
    qq = [jnp.concatenate([qt_ref[0, 0, head(hh), strip(qs)],
                           jnp.where(slot == first_head + hh, aqt_ref[0, 0, :, strip(qs)], jnp.zeros((), BF16))],
                          axis=0)
          for hh, qs in chains]

    def scores(kj, diagonal):
        c0 = pl.multiple_of(kj * FOX_BK, FOX_BK)
        sts = {}
        for ks in range(FOX_NSUB):
            rows = pl.ds(c0 + ks * FOX_SUB, FOX_SUB)
            ak = ak_ref[0, rows, :]
            kk = [jnp.concatenate([k_ref[0, rows, head(hh)], ak], axis=1) for hh in range(FOX_HPS)]
            for ci, (hh, qs) in enumerate(chains):
                if diagonal and ks > qs:
                    continue
                st = _dot(kk[hh], qq[ci])
                if diagonal and ks == qs:
                    key = lax.broadcasted_iota(jnp.int32, st.shape, 0)
                    qry = lax.broadcasted_iota(jnp.int32, st.shape, 1)
                    st = jnp.where(key <= qry, st, -jnp.inf)
                sts[ks, ci] = st
        return sts

    def update(kj, carry, sts):
        carry = list(carry)
        for ks in range(FOX_NSUB):
            vt1 = [jnp.concatenate([vt_ref[0, kj, head(hh), strip(ks)], ones], axis=0) for hh in range(FOX_HPS)]
            for ci, (hh, qs) in enumerate(chains):
                if (ks, ci) not in sts:
                    continue
                m, acc = carry[ci]
                st = sts[ks, ci]
                m_new = jnp.maximum(m, jnp.max(st, axis=0, keepdims=True))
                alpha = jnp.exp2(m - m_new)
                p = jnp.exp2(st - m_new).astype(BF16)
                acc = alpha * acc + _dot(vt1[hh], p)
                carry[ci] = (m_new, acc)
        return tuple(carry)

    ones = jnp.ones((BF16_ROWS, FOX_SUB), BF16)
    init = tuple((jnp.full((1, FOX_SUB), -jnp.inf, F32), jnp.zeros((FOX_DH + BF16_ROWS, FOX_SUB), F32))
                 for _ in chains)

    def pair(i, carry):
        kj = 2 * i
        sts0, sts1 = scores(kj, False), scores(kj + 1, False)
        return update(kj + 1, update(kj, carry, sts0), sts1)

    carry = lax.fori_loop(0, qi // 2, pair, init)
    carry = lax.cond(qi % 2 == 1, lambda cr: update(qi - 1, cr, scores(qi - 1, False)), lambda cr: cr, carry)
    carry = update(qi, carry, scores(qi, True))
    for ci, (hh, qs) in enumerate(chains):
        _, acc = carry[ci]
        o_ref[0, strip(qs), head(hh)] = (acc[:FOX_DH] / acc[FOX_DH:FOX_DH + 1]).T.astype(BF16)


def _fox_attn(qt, aqt, k, ak, vt, batch, seq):
    w = FOX_HPS * FOX_DH
    nq = seq // FOX_BQ
    qblk = pl.BlockSpec((1, 1, w, FOX_BQ), lambda b, g, i: (b, i, g, 0))
    kblk = pl.BlockSpec((1, seq, w), lambda b, g, i: (b, 0, g))
    vblk = pl.BlockSpec((1, seq // FOX_BK, w, FOX_BK), lambda b, g, i: (b, 0, g, 0))
    aqblk = pl.BlockSpec((1, 1, LANES, FOX_BQ), lambda b, g, i: (b, i, 0, 0))
    akblk = pl.BlockSpec((1, seq, LANES), lambda b, g, i: (b, 0, 0))
    out = pl.pallas_call(
        _fox_attn_kernel,
        out_shape=jax.ShapeDtypeStruct((batch, seq, MAIN_W), BF16),
        grid=(batch, FOX_HEADS // FOX_HPS, nq),
        in_specs=[qblk, aqblk, kblk, akblk, vblk],
        out_specs=pl.BlockSpec((1, FOX_BQ, w), lambda b, g, i: (b, i, g)),
        compiler_params=_cparams(("parallel", "parallel", "arbitrary")),
        name="fox_attn",
    )(qt.reshape(batch, nq, MAIN_W, FOX_BQ), aqt, k.reshape(batch, seq, MAIN_W), ak.reshape(batch, seq, LANES), vt)
    return out.reshape(batch * seq, MAIN_W)


def _pad_heads_cols(w, heads, width, padded):
    lead = w.shape[:-1]
    w = w.reshape(lead + (heads, width))
    w = jnp.pad(w, [(0, 0)] * len(lead) + [(0, 0), (0, padded - width)])
    return w.reshape(lead + (heads * padded,))


def _pad_heads_rows(w, heads, width, padded):
    layers, _, n = w.shape
    w = w.reshape(layers, heads, width, n)
    w = jnp.pad(w, [(0, 0), (0, 0), (0, padded - width), (0, 0)])
    return w.reshape(layers, heads * padded, n)


def _gla_weights(w_in, w_gate_up, b_gate, g_head, w_out):
    dk = GLA_HEADS * GLA_DK_HEAD
    dv = GLA_HEADS * GLA_DV_HEAD
    o1, o2, o3, o4, o5 = dk, 2 * dk, 2 * dk + dv, 2 * dk + 2 * dv, 2 * dk + 2 * dv + GLA_RANK
    w = jnp.concatenate([
        _pad_heads_cols(w_in[..., :o1], GLA_HEADS, GLA_DK_HEAD, DKP),
        _pad_heads_cols(w_in[..., o1:o2], GLA_HEADS, GLA_DK_HEAD, DKP),
        _pad_heads_cols(w_in[..., o2:o3], GLA_HEADS, GLA_DV_HEAD, DVP),
        _pad_heads_cols(w_in[..., o3:o4], GLA_HEADS, GLA_DV_HEAD, DVP),
        w_in[..., o5:],
        jnp.pad(w_in[..., o4:o5], [(0, 0), (0, 0), (0, LANES - GLA_RANK)]),
    ], axis=-1).astype(BF16)
    wup = jnp.pad(_pad_heads_cols(w_gate_up, GLA_HEADS, GLA_DK_HEAD, DKP),
                  [(0, 0), (0, LANES - GLA_RANK), (0, 0)]).astype(BF16)
    bg = _pad_heads_cols(b_gate[:, None, :], GLA_HEADS, GLA_DK_HEAD, DKP)
    gh = jnp.pad(g_head[:, None, :], [(0, 0), (0, 0), (0, DVP - GLA_DV_HEAD)])
    wo = jnp.concatenate([_pad_heads_rows(w_out[:, :MAIN_W], GLA_HEADS, GLA_DV_HEAD, DVP), w_out[:, MAIN_W:]],
                         axis=1).astype(BF16)
    return w, wup, bg, gh, wo


def kernel(x, mem, norm_ffn, w_ffn_in, w_ffn_out, norm_mix, norm_mem, w_mem_kv, w_out, w_in_a, w_gate_up, b_gate,
           norm_gla_head, w_in_b, norm_shared, w_kv_shared, w_fgate, b_fgate, norm_final):
    batch, seq, _ = x.shape
    h = x.reshape(batch * seq, D_MODEL)

    w1 = w_ffn_in.astype(BF16)
    w2 = w_ffn_out.astype(BF16)
    g_ffn = norm_ffn[:, :, None, :]
    g_mix = norm_mix[:, None, :]
    wkt = jnp.swapaxes(w_mem_kv[:, :, :MEM_W], 1, 2).astype(BF16)
    wv = w_mem_kv[:, :, MEM_W:].astype(BF16)
    kb, vb = _mem_kv(mem, norm_mem[:, None, :], wkt, wv)
    w_a, wup, bg, gh, wo_a = _gla_weights(w_in_a, w_gate_up, b_gate, norm_gla_head, w_out[:N_A])
    wqt_b = jnp.swapaxes(w_in_b[:, :, :MAIN_W], 1, 2).astype(BF16)
    wm_b = w_in_b[:, :, MAIN_W:].astype(BF16)
    wo_b = w_out[N_A:].astype(BF16)

    k_sh = v_sh = aq_sh = ak_sh = None
    for l in range(DEPTH):
        if l == N_A:
            w_sh = jnp.concatenate([w_kv_shared[:, :MAIN_W], jnp.pad(w_fgate, [(0, 0), (0, LANES - FOX_HEADS)])],
                                   axis=1).astype(BF16)
            wvt = w_kv_shared[:, MAIN_W:].T.astype(BF16)
            bf = jnp.pad(b_fgate[None, :], [(0, 0), (0, LANES - FOX_HEADS)])
            k_sh, v_sh, aq_sh, ak_sh = _shared_kv(h, norm_shared[None, :], w_sh, wvt, bf, batch, seq)
        h = _ffn(h, g_ffn, w1, w2, l, 0)
        if l < N_A:
            q, k, v, r, la, q_mem = _gla_proj(h, g_mix, w_a, wup, bg, l, l)
            o_main = _gla(q, k, la, v, r, gh, l, batch, seq)
            h = _mix_out(h, o_main, q_mem, kb, vb, wo_a, l, l, seq)
        else:
            qt, q_mem = _fox_proj(h, g_mix, wqt_b, wm_b, l, l - N_A)
            o_main = _fox_attn(qt, aq_sh, k_sh, ak_sh, v_sh, batch, seq)
            h = _mix_out(h, o_main, q_mem, kb, vb, wo_b, l, l - N_A, seq)
        h = _ffn(h, g_ffn, w1, w2, l, 1, norm_final[None, :] if l == DEPTH - 1 else None)
    return h.reshape(batch, seq, D_MODEL)
```

```python
import numpy as np
import jax
import jax.numpy as jnp
from jax import lax
from jax.experimental import pallas as pl
from jax.experimental.pallas import tpu as pltpu

F32 = jnp.float32
BF16 = jnp.bfloat16

D_MODEL = 1024
DEPTH = 4
N_A = DEPTH // 2
MAIN_W = 768
MEM_W = 256
GLA_HEADS = 4
GLA_DK_HEAD = 96
GLA_DV_HEAD = 192
GLA_RANK = 16
GLA_TEMP = 16.0
FOX_HEADS = 6
FOX_DH = 128
MEM_HEADS = 4
MEM_DH = 64
MEM_LEN = 256
D_FF = 2816
EPS = 1e-6
LOG2E = 1.4426950408889634

LANES = 128
MXU_N = 256
BF16_ROWS = 16
VMEM_LIMIT = 56 * 1024 * 1024

DKP = LANES
DVP = MXU_N
GLA_QW = GLA_HEADS * DKP
GLA_VW = GLA_HEADS * DVP
GLA_CHUNK = MXU_N
GLA_LEVELS = 8
GLA_MM_LEVELS = 3

TM = 512
PROJ_TM = 1024
FOX_BQ = TM
FOX_BK = TM


def _cparams(sem):
    return pltpu.CompilerParams(dimension_semantics=sem, vmem_limit_bytes=VMEM_LIMIT)


def _rmsnorm(x, g):
    return x * lax.rsqrt(jnp.mean(x * x, axis=-1, keepdims=True) + EPS) * g


def _log_sigmoid(x):
    return jnp.minimum(x, 0.0) - jnp.log(1.0 + jnp.exp(-jnp.abs(x)))


def _sigmoid(x):
    return 1.0 / (1.0 + jnp.exp(-x))


def _split3(x):
    hi = x.astype(BF16)
    r1 = x - hi.astype(F32)
    mid = r1.astype(BF16)
    lo = (r1 - mid.astype(F32)).astype(BF16)
    return jnp.concatenate([hi, mid, lo], axis=1)


def _split2(x):
    hi = x.astype(BF16)
    lo = (x - hi.astype(F32)).astype(BF16)
    return jnp.concatenate([hi, lo], axis=1)


def _dot(a, b):
    return jnp.dot(a, b, preferred_element_type=F32)


def _dot_nt(a, b):
    return lax.dot_general(a, b, (((1,), (1,)), ((), ())), preferred_element_type=F32)


def _dot_tn(a, b):
    return lax.dot_general(a, b, (((0,), (0,)), ((), ())), preferred_element_type=F32)


def _exact_sum3(r, w):
    return r[:, :w] + r[:, w:2 * w] + r[:, 2 * w:]


FFN_BOUNDS = (0, 6 * MXU_N, D_FF)
FFN_TM = 1024


def _ffn_kernel(h_ref, g_ref, w1_ref, w2_ref, *rest):
    o_ref = rest[-1]
    h = h_ref[...]
    xn = _rmsnorm(h, g_ref[0, 0]).astype(BF16)
    y = None
    for f0, f1 in zip(FFN_BOUNDS[:-1], FFN_BOUNDS[1:]):
        a = _dot(xn, w1_ref[0, 0, :, f0:f1])
        c = _dot(xn, w1_ref[0, 0, :, D_FF + f0:D_FF + f1])
        act = (a * _sigmoid(a) * c).astype(BF16)
        part = _dot(act, w2_ref[0, 0, f0:f1, :])
        y = part if y is None else y + part
    out = h + 0.5 * y
    o_ref[...] = _rmsnorm(out, rest[0][...]) if len(rest) == 2 else out


def _ffn(h, g_all, w1_all, w2_all, layer, half, g_final=None):
    n = h.shape[0]
    pick = lambda i: (layer, half, 0, 0)
    in_specs = [
        pl.BlockSpec((FFN_TM, D_MODEL), lambda i: (i, 0)),
        pl.BlockSpec((1, 1, 1, D_MODEL), pick),
        pl.BlockSpec((1, 1, D_MODEL, 2 * D_FF), pick, pipeline_mode=pl.Buffered(1)),
        pl.BlockSpec((1, 1, D_FF, D_MODEL), pick, pipeline_mode=pl.Buffered(1)),
    ]
    args = [h, g_all, w1_all, w2_all]
    if g_final is not None:
        in_specs.append(pl.BlockSpec((1, D_MODEL), lambda i: (0, 0)))
        args.append(g_final)
    return pl.pallas_call(
        _ffn_kernel,
        out_shape=jax.ShapeDtypeStruct((n, D_MODEL), F32),
        grid=(n // FFN_TM,),
        in_specs=in_specs,
        out_specs=pl.BlockSpec((FFN_TM, D_MODEL), lambda i: (i, 0)),
        compiler_params=_cparams(("parallel",)),
        name="ffn",
    )(*args)


GLA_PW = 2 * GLA_QW + 2 * GLA_VW + MEM_W + LANES


def _gla_proj_kernel(h_ref, g_ref, w_ref, wup_ref, bg_ref, q_ref, k_ref, v_ref, r_ref, la_ref, qm_ref):
    xn = _rmsnorm(h_ref[...], g_ref[...]).astype(BF16)
    oq, ok, ov = 0, GLA_QW, 2 * GLA_QW
    orr, om, og = ov + GLA_VW, ov + 2 * GLA_VW, ov + 2 * GLA_VW + MEM_W
    g_low = _dot(xn, w_ref[:, og:og + LANES]).astype(BF16)
    q_ref[...] = (_dot(xn, w_ref[:, oq:oq + GLA_QW]) * (GLA_DK_HEAD ** -0.5)).astype(BF16)
    x = _dot(g_low, wup_ref[...]) + bg_ref[...]
    k_ref[...] = _dot(xn, w_ref[:, ok:ok + GLA_QW]).astype(BF16)
    la_ref[...] = _log_sigmoid(x) * (LOG2E / GLA_TEMP)
    v_ref[...] = _dot(xn, w_ref[:, ov:ov + GLA_VW]).astype(BF16)
    r_ref[...] = _dot(xn, w_ref[:, orr:orr + GLA_VW]).astype(BF16)
    qm_ref[...] = _dot(xn, w_ref[:, om:om + MEM_W]).astype(BF16)


def _gla_proj(h, g, w, wup, bg, layer, a):
    n = h.shape[0]
    row = lambda i: (i, 0)
    return pl.pallas_call(
        _gla_proj_kernel,
        out_shape=(
            jax.ShapeDtypeStruct((n, GLA_QW), BF16),
            jax.ShapeDtypeStruct((n, GLA_QW), BF16),
            jax.ShapeDtypeStruct((n, GLA_VW), BF16),
            jax.ShapeDtypeStruct((n, GLA_VW), BF16),
            jax.ShapeDtypeStruct((n, GLA_QW), F32),
            jax.ShapeDtypeStruct((n, MEM_W), BF16),
        ),
        grid=(n // PROJ_TM,),
        in_specs=[
            pl.BlockSpec((PROJ_TM, D_MODEL), row),
            pl.BlockSpec((None, 1, D_MODEL), lambda i: (layer, 0, 0)),
            pl.BlockSpec((None, D_MODEL, GLA_PW), lambda i: (a, 0, 0), pipeline_mode=pl.Buffered(1)),
            pl.BlockSpec((None, LANES, GLA_QW), lambda i: (a, 0, 0)),
            pl.BlockSpec((None, 1, GLA_QW), lambda i: (a, 0, 0)),
        ],
        out_specs=(
            pl.BlockSpec((PROJ_TM, GLA_QW), row),
            pl.BlockSpec((PROJ_TM, GLA_QW), row),
            pl.BlockSpec((PROJ_TM, GLA_VW), row),
            pl.BlockSpec((PROJ_TM, GLA_VW), row),
            pl.BlockSpec((PROJ_TM, GLA_QW), row),
            pl.BlockSpec((PROJ_TM, MEM_W), row),
        ),
        compiler_params=_cparams(("parallel",)),
        name="gla_proj",
    )(h, g, w, wup, bg)


GLA_TS = 1024
GLA_UNROLL = 2


def _gla_consts():
    c = GLA_CHUNK
    i = np.arange(c)[:, None]
    j = np.arange(c)[None, :]
    tril = (j <= i)
    b_rows, masks = [], []
    for lv in range(GLA_LEVELS):
        hs = 1 << lv
        mid = (i // (2 * hs)) * (2 * hs) + hs
        lower = (i % (2 * hs)) >= hs
        b_rows.append(np.where(lower, (j >= mid) & (j <= i), (j > i) & (j < mid)))
        masks.append(((i // (2 * hs)) == (j // (2 * hs))) & lower & ((j % (2 * hs)) < hs))
    masks.append(i == j)
    asb = lambda x: jnp.asarray(x.astype(np.float32), dtype=BF16)
    return asb(tril), asb(np.concatenate(b_rows[:GLA_MM_LEVELS], axis=0)), asb(np.stack(masks, axis=0))


def _gla_kernel(q_ref, k_ref, la_ref, v_ref, r_ref, gh_ref, tril_ref, b_ref, m_ref, o_ref, st_ref):
    c = GLA_CHUNK

    @pl.when(pl.program_id(1) == 0)
    def _():
        st_ref[...] = jnp.zeros_like(st_ref)

    gh = gh_ref[...]

    heads = [(slice(h * DKP, (h + 1) * DKP), slice(h * DVP, (h + 1) * DVP)) for h in range(GLA_HEADS)]

    def decays(r0):
        g = la_ref[pl.ds(r0, c), :]
        cum = _dot(tril_ref[...], _split2(g))
        cum = cum[:, :GLA_QW] + cum[:, GLA_QW:]
        small = _dot(b_ref[...], g.astype(BF16))
        lev = [small[lv * c:(lv + 1) * c] for lv in range(GLA_MM_LEVELS)]
        for lv in range(GLA_MM_LEVELS, GLA_LEVELS):
            hs = 1 << lv
            parts = []
            for mid in range(hs, c, 2 * hs):
                parts += [cum[mid - 1:mid, :] - cum[mid - hs:mid, :], cum[mid:mid + hs, :] - cum[mid - 1:mid, :]]
            lev.append(jnp.concatenate(parts, axis=0))
        return cum, lev

    def intra(r0, cum, lev):
        qa = q_ref[pl.ds(r0, c), :]
        ka = k_ref[pl.ds(r0, c), :]
        v = v_ref[pl.ds(r0, c), :]
        per_head = []
        for ks, vs in heads:
            qb, kb, vh = qa[:, ks], ka[:, ks], v[:, vs]
            cum_h = cum[:, ks]
            last_h = cum_h[c - 1:c, :]
            s = _dot_nt(qb, kb).astype(BF16) * m_ref[GLA_LEVELS]
            for lv in range(GLA_LEVELS):
                e = jnp.exp2(lev[lv][:, ks]).astype(BF16)
                s = s + _dot_nt(qb * e, kb * e).astype(BF16) * m_ref[lv]
            kd = (kb.astype(F32) * jnp.exp2(last_h - cum_h)).astype(BF16)
            qg = (qb.astype(F32) * jnp.exp2(cum_h)).astype(BF16)
            per_head.append((qg, s, _dot_tn(vh, kd), jnp.exp2(last_h)))
        return per_head

    def recur(r0, per_head):
        v = v_ref[pl.ds(r0, c), :]
        r = r_ref[pl.ds(r0, c), :].astype(F32)
        for h, (ks, vs) in enumerate(heads):
            qg, s, upd, last_decay = per_head[h]
            st = st_ref[h]
            o = _dot_nt(qg, st.astype(BF16)) + _dot(s, v[:, vs])
            st_ref[h] = st * last_decay + upd
            ms = jnp.sum(o * o, axis=-1, keepdims=True) * (1.0 / GLA_DV_HEAD)
            rh = r[:, vs]
            y = o * lax.rsqrt(ms + EPS) * gh * (rh * _sigmoid(rh))
            o_ref[pl.ds(r0, c), vs] = y.astype(BF16)

    def group(gi, carry):
        offs = [pl.multiple_of((gi * GLA_UNROLL + u) * c, c) for u in range(GLA_UNROLL)]
        dec = [decays(r0) for r0 in offs]
        cur = [intra(r0, *d) for r0, d in zip(offs, dec)]
        for r0, per_head in zip(offs, cur):
            recur(r0, per_head)
        return carry

    lax.fori_loop(0, GLA_TS // (c * GLA_UNROLL), group, 0)


def _gla(q, k, la, v, r, gh, a, batch, seq):
    n = q.shape[0]
    nt = seq // GLA_TS
    tril, bmat, m = _gla_consts()
    row = lambda b, t: (b * nt + t, 0)
    fixed2 = lambda b, t: (0, 0)
    fixed3 = lambda b, t: (0, 0, 0)
    return pl.pallas_call(
        _gla_kernel,
        out_shape=jax.ShapeDtypeStruct((n, GLA_VW), BF16),
        grid=(batch, nt),
        in_specs=[
            pl.BlockSpec((GLA_TS, GLA_QW), row),
            pl.BlockSpec((GLA_TS, GLA_QW), row),
            pl.BlockSpec((GLA_TS, GLA_QW), row),
            pl.BlockSpec((GLA_TS, GLA_VW), row),
            pl.BlockSpec((GLA_TS, GLA_VW), row),
            pl.BlockSpec((None, 1, DVP), lambda b, t: (a, 0, 0)),
            pl.BlockSpec(tril.shape, fixed2),
            pl.BlockSpec(bmat.shape, fixed2),
            pl.BlockSpec(m.shape, fixed3),
        ],
        out_specs=pl.BlockSpec((GLA_TS, GLA_VW), row),
        scratch_shapes=[pltpu.VMEM((GLA_HEADS, DVP, DKP), F32)],
        compiler_params=_cparams(("arbitrary", "arbitrary")),
        name="gla",
    )(q, k, la, v, r, gh, tril, bmat, m)


MEM_BW = MEM_HEADS * MEM_LEN


def _mem_kv_kernel(mem_ref, g_ref, wkt_ref, wv_ref, kb_ref, vb_ref):
    d_row = lax.broadcasted_iota(jnp.int32, (MEM_W, MEM_LEN), 0) // MEM_DH
    d_col = lax.broadcasted_iota(jnp.int32, (MEM_LEN, MEM_W), 1) // MEM_DH
    for b in range(mem_ref.shape[0]):
        mn = _rmsnorm(mem_ref[b], g_ref[0]).astype(BF16)
        kt = _dot_nt(wkt_ref[0], mn) * (MEM_DH ** -0.5)
        v = _dot(mn, wv_ref[0])
        for hh in range(MEM_HEADS):
            kb_ref[0, b, :, hh * MEM_LEN:(hh + 1) * MEM_LEN] = jnp.where(d_row == hh, kt, 0.0).astype(BF16)
            vb_ref[0, b, hh * MEM_LEN:(hh + 1) * MEM_LEN, :] = jnp.where(d_col == hh, v, 0.0).astype(BF16)


def _mem_kv(mem, g, wkt, wv):
    b = mem.shape[0]
    return pl.pallas_call(
        _mem_kv_kernel,
        out_shape=(
            jax.ShapeDtypeStruct((DEPTH, b, MEM_W, MEM_BW), BF16),
            jax.ShapeDtypeStruct((DEPTH, b, MEM_BW, MEM_W), BF16),
        ),
        grid=(DEPTH,),
        in_specs=[
            pl.BlockSpec((b, MEM_LEN, D_MODEL), lambda l: (0, 0, 0)),
            pl.BlockSpec((1, 1, D_MODEL), lambda l: (l, 0, 0)),
            pl.BlockSpec((1, MEM_W, D_MODEL), lambda l: (l, 0, 0)),
            pl.BlockSpec((1, D_MODEL, MEM_W), lambda l: (l, 0, 0)),
        ],
        out_specs=(
            pl.BlockSpec((1, b, MEM_W, MEM_BW), lambda l: (l, 0, 0, 0)),
            pl.BlockSpec((1, b, MEM_BW, MEM_W), lambda l: (l, 0, 0, 0)),
        ),
        compiler_params=_cparams(("parallel",)),
        name="mem_kv",
    )(mem, g, wkt, wv)


def _mix_out_kernel(h_ref, om_ref, qm_ref, kb_ref, vb_ref, wo_ref, o_ref):
    lg = _dot(qm_ref[...], kb_ref[0, 0])
    ps = []
    for hh in range(MEM_HEADS):
        s = lg[:, hh * MEM_LEN:(hh + 1) * MEM_LEN]
        e = jnp.exp(s - jnp.max(s, axis=-1, keepdims=True))
        ps.append((e / jnp.sum(e, axis=-1, keepdims=True)).astype(BF16))
    o_mem = _dot(jnp.concatenate(ps, axis=1), vb_ref[0, 0])
    wm = om_ref.shape[1]
    y = _dot(om_ref[...], wo_ref[0:wm, :]) + _dot(o_mem.astype(BF16), wo_ref[wm:wm + MEM_W, :])
    o_ref[...] = h_ref[...] + y


def _mix_out(h, o_main, q_mem, kb, vb, wo, layer, j, seq):
    n = h.shape[0]
    wm = o_main.shape[1]
    per_b = seq // PROJ_TM
    row = lambda i: (i, 0)
    return pl.pallas_call(
        _mix_out_kernel,
        out_shape=jax.ShapeDtypeStruct((n, D_MODEL), F32),
        grid=(n // PROJ_TM,),
        in_specs=[
            pl.BlockSpec((PROJ_TM, D_MODEL), row),
            pl.BlockSpec((PROJ_TM, wm), row),
            pl.BlockSpec((PROJ_TM, MEM_W), row),
            pl.BlockSpec((1, 1, MEM_W, MEM_BW), lambda i: (layer, i // per_b, 0, 0)),
            pl.BlockSpec((1, 1, MEM_BW, MEM_W), lambda i: (layer, i // per_b, 0, 0)),
            pl.BlockSpec((None, wm + MEM_W, D_MODEL), lambda i: (j, 0, 0)),
        ],
        out_specs=pl.BlockSpec((PROJ_TM, D_MODEL), row),
        compiler_params=_cparams(("parallel",)),
        name="mix_out",
    )(h, o_main, q_mem, kb, vb, wo)


SH_PW = MAIN_W + LANES


ONE_LANE = LANES - 1
AUX_SLOT = BF16_ROWS


def _fox_aux_consts():
    pqt = np.zeros((LANES, 3 * LANES), np.float32)
    pk = np.zeros((3 * LANES, LANES), np.float32)
    for h in range(FOX_HEADS):
        for p in range(3):
            pqt[h * AUX_SLOT + p, p * LANES + h] = 1.0
            pqt[h * AUX_SLOT + 3 + p, ONE_LANE] = 1.0
            pk[ONE_LANE, h * AUX_SLOT + p] = 1.0
            pk[p * LANES + h, h * AUX_SLOT + 3 + p] = -1.0
    tril = np.tril(np.ones((TM, TM), np.float32))
    return jnp.asarray(tril, dtype=BF16), jnp.asarray(pqt, dtype=BF16), jnp.asarray(pk, dtype=BF16)


def _shared_kernel(h_ref, g_ref, w_ref, wvt_ref, bf_ref, tril_ref, pqt_ref, pk_ref,
                   k_ref, vt_ref, aqt_ref, ak_ref, carry_ref):
    @pl.when(pl.program_id(1) == 0)
    def _():
        carry_ref[...] = jnp.zeros_like(carry_ref)

    xn = _rmsnorm(h_ref[...], g_ref[...]).astype(BF16)
    log_f = _log_sigmoid(_dot(xn, w_ref[:, MAIN_W:SH_PW]) + bf_ref[...])
    k_ref[...] = _dot(xn, w_ref[:, 0:MAIN_W]).astype(BF16)
    d = _exact_sum3(_dot(tril_ref[...], _split3(log_f)), LANES) + carry_ref[...]
    carry_ref[...] = d[TM - 1:TM, :]
    d3 = _split3(d * LOG2E)
    lane = lax.broadcasted_iota(jnp.int32, d3.shape, 1)
    d3 = jnp.where(lane == ONE_LANE, jnp.ones_like(d3), d3)
    vt_ref[0, 0] = _dot_nt(wvt_ref[...], xn).astype(BF16)
    aqt_ref[0, 0] = _dot_nt(pqt_ref[...], d3).astype(BF16)
    ak_ref[...] = _dot(d3, pk_ref[...]).astype(BF16)


def _shared_kv(h, g, w, wvt, bf, batch, seq):
    n = h.shape[0]
    nt = seq // TM
    tril, pqt, pk = _fox_aux_consts()
    row = lambda b, t: (b * nt + t, 0)
    fixed = lambda b, t: (0, 0)
    slab = lambda b, t: (b, t, 0, 0)
    return pl.pallas_call(
        _shared_kernel,
        out_shape=(
            jax.ShapeDtypeStruct((n, MAIN_W), BF16),
            jax.ShapeDtypeStruct((batch, nt, MAIN_W, TM), BF16),
            jax.ShapeDtypeStruct((batch, nt, LANES, TM), BF16),
            jax.ShapeDtypeStruct((n, LANES), BF16),
        ),
        grid=(batch, nt),
        in_specs=[
            pl.BlockSpec((TM, D_MODEL), row),
            pl.BlockSpec((1, D_MODEL), fixed),
            pl.BlockSpec((D_MODEL, SH_PW), fixed),
            pl.BlockSpec((MAIN_W, D_MODEL), fixed),
            pl.BlockSpec((1, LANES), fixed),
            pl.BlockSpec((TM, TM), fixed),
            pl.BlockSpec((LANES, 3 * LANES), fixed),
            pl.BlockSpec((3 * LANES, LANES), fixed),
        ],
        out_specs=(
            pl.BlockSpec((TM, MAIN_W), row),
            pl.BlockSpec((1, 1, MAIN_W, TM), slab),
            pl.BlockSpec((1, 1, LANES, TM), slab),
            pl.BlockSpec((TM, LANES), row),
        ),
        scratch_shapes=[pltpu.VMEM((1, LANES), F32)],
        compiler_params=_cparams(("arbitrary", "arbitrary")),
        name="shared_kv",
    )(h, g, w, wvt, bf, tril, pqt, pk)


def _fox_proj_kernel(h_ref, g_ref, wqt_ref, wm_ref, qt_ref, qm_ref):
    xn = _rmsnorm(h_ref[...], g_ref[...]).astype(BF16)
    qt = (_dot_nt(wqt_ref[...], xn) * (FOX_DH ** -0.5 * LOG2E)).astype(BF16)
    for s in range(PROJ_TM // TM):
        qt_ref[s] = qt[:, s * TM:(s + 1) * TM]
    qm_ref[...] = _dot(xn, wm_ref[...]).astype(BF16)


def _fox_proj(h, g, wqt, wm, layer, j):
    n = h.shape[0]
    row = lambda i: (i, 0)
    return pl.pallas_call(
        _fox_proj_kernel,
        out_shape=(jax.ShapeDtypeStruct((n // TM, MAIN_W, TM), BF16), jax.ShapeDtypeStruct((n, MEM_W), BF16)),
        grid=(n // PROJ_TM,),
        in_specs=[
            pl.BlockSpec((PROJ_TM, D_MODEL), row),
            pl.BlockSpec((None, 1, D_MODEL), lambda i: (layer, 0, 0)),
            pl.BlockSpec((None, MAIN_W, D_MODEL), lambda i: (j, 0, 0)),
            pl.BlockSpec((None, D_MODEL, MEM_W), lambda i: (j, 0, 0)),
        ],
        out_specs=(pl.BlockSpec((PROJ_TM // TM, MAIN_W, TM), lambda i: (i, 0, 0)),
                   pl.BlockSpec((PROJ_TM, MEM_W), row)),
        compiler_params=_cparams(("parallel",)),
        name="fox_proj",
    )(h, g, wqt, wm)


FOX_HPS = 3


FOX_SUB = MXU_N
FOX_NSUB = FOX_BK // FOX_SUB


def _fox_attn_kernel(qt_ref, aqt_ref, k_ref, ak_ref, vt_ref, o_ref):
    qi = pl.program_id(2)
    chains = [(hh, qs) for hh in range(FOX_HPS) for qs in range(FOX_BQ // FOX_SUB)]

    def head(hh):
        return slice(hh * FOX_DH, (hh + 1) * FOX_DH)

    def strip(s):
        return slice(s * FOX_SUB, (s + 1) * FOX_SUB)

    slot = lax.broadcasted_iota(jnp.int32, (LANES, FOX_SUB), 0) // AUX_SLOT
    first_head = pl.program_id(1) * FOX_HPS
    qq = [jnp.concatenate([qt_ref[0, 0, head(hh), strip(qs)],
                           jnp.where(slot == first_head + hh, aqt_ref[0, 0, :, strip(qs)], jnp.zeros((), BF16))],
                          axis=0)
          for hh, qs in chains]

    def scores(kj, diagonal):
        c0 = pl.multiple_of(kj * FOX_BK, FOX_BK)
        sts = {}
        for ks in range(FOX_NSUB):
            rows = pl.ds(c0 + ks * FOX_SUB, FOX_SUB)
            ak = ak_ref[0, rows, :]
            kk = [jnp.concatenate([k_ref[0, rows, head(hh)], ak], axis=1) for hh in range(FOX_HPS)]
            for ci, (hh, qs) in enumerate(chains):
                if diagonal and ks > qs:
                    continue
                st = _dot(kk[hh], qq[ci])
                if diagonal and ks == qs:
                    key = lax.broadcasted_iota(jnp.int32, st.shape, 0)
                    qry = lax.broadcasted_iota(jnp.int32, st.shape, 1)
                    st = jnp.where(key <= qry, st, -jnp.inf)
                sts[ks, ci] = st
        return sts

    def update(kj, carry, sts):
        carry = list(carry)
        for ks in range(FOX_NSUB):
            vt1 = [jnp.concatenate([vt_ref[0, kj, head(hh), strip(ks)], ones], axis=0) for hh in range(FOX_HPS)]
            for ci, (hh, qs) in enumerate(chains):
                if (ks, ci) not in sts:
                    continue
                m, acc = carry[ci]
                st = sts[ks, ci]
                m_new = jnp.maximum(m, jnp.max(st, axis=0, keepdims=True))
                alpha = jnp.exp2(m - m_new)
                p = jnp.exp2(st - m_new).astype(BF16)
                acc = alpha * acc + _dot(vt1[hh], p)
                carry[ci] = (m_new, acc)
        return tuple(carry)

    ones = jnp.ones((BF16_ROWS, FOX_SUB), BF16)
    init = tuple((jnp.full((1, FOX_SUB), -jnp.inf, F32), jnp.zeros((FOX_DH + BF16_ROWS, FOX_SUB), F32))
                 for _ in chains)

    def pair(i, carry):
        kj = 2 * i
        sts0, sts1 = scores(kj, False), scores(kj + 1, False)
        return update(kj + 1, update(kj, carry, sts0), sts1)

    def tail_with_leftover(carry):
        sts0, sts1 = scores(qi - 1, False), scores(qi, True)
        return update(qi, update(qi - 1, carry, sts0), sts1)

    carry = lax.fori_loop(0, qi // 2, pair, init)
    carry = lax.cond(qi % 2 == 1, tail_with_leftover, lambda cr: update(qi, cr, scores(qi, True)), carry)
    for ci, (hh, qs) in enumerate(chains):
        _, acc = carry[ci]
        o_ref[0, strip(qs), head(hh)] = (acc[:FOX_DH] / acc[FOX_DH:FOX_DH + 1]).T.astype(BF16)


def _fox_attn(qt, aqt, k, ak, vt, batch, seq):
    w = FOX_HPS * FOX_DH
    nq = seq // FOX_BQ
    qblk = pl.BlockSpec((1, 1, w, FOX_BQ), lambda b, g, i: (b, i, g, 0))
    kblk = pl.BlockSpec((1, seq, w), lambda b, g, i: (b, 0, g))
    vblk = pl.BlockSpec((1, seq // FOX_BK, w, FOX_BK), lambda b, g, i: (b, 0, g, 0))
    aqblk = pl.BlockSpec((1, 1, LANES, FOX_BQ), lambda b, g, i: (b, i, 0, 0))
    akblk = pl.BlockSpec((1, seq, LANES), lambda b, g, i: (b, 0, 0))
    out = pl.pallas_call(
        _fox_attn_kernel,
        out_shape=jax.ShapeDtypeStruct((batch, seq, MAIN_W), BF16),
        grid=(batch, FOX_HEADS // FOX_HPS, nq),
        in_specs=[qblk, aqblk, kblk, akblk, vblk],
        out_specs=pl.BlockSpec((1, FOX_BQ, w), lambda b, g, i: (b, i, g)),
        compiler_params=_cparams(("parallel", "parallel", "arbitrary")),
        name="fox_attn",
    )(qt.reshape(batch, nq, MAIN_W, FOX_BQ), aqt, k.reshape(batch, seq, MAIN_W), ak.reshape(batch, seq, LANES), vt)
    return out.reshape(batch * seq, MAIN_W)


def _pad_heads_cols(w, heads, width, padded):
    lead = w.shape[:-1]
    w = w.reshape(lead + (heads, width))
    w = jnp.pad(w, [(0, 0)] * len(lead) + [(0, 0), (0, padded - width)])
    return w.reshape(lead + (heads * padded,))


def _pad_heads_rows(w, heads, width, padded):
    layers, _, n = w.shape
    w = w.reshape(layers, heads, width, n)
    w = jnp.pad(w, [(0, 0), (0, 0), (0, padded - width), (0, 0)])
    return w.reshape(layers, heads * padded, n)


def _gla_weights(w_in, w_gate_up, b_gate, g_head, w_out):
    dk = GLA_HEADS * GLA_DK_HEAD
    dv = GLA_HEADS * GLA_DV_HEAD
    o1, o2, o3, o4, o5 = dk, 2 * dk, 2 * dk + dv, 2 * dk + 2 * dv, 2 * dk + 2 * dv + GLA_RANK
    w = jnp.concatenate([
        _pad_heads_cols(w_in[..., :o1], GLA_HEADS, GLA_DK_HEAD, DKP),
        _pad_heads_cols(w_in[..., o1:o2], GLA_HEADS, GLA_DK_HEAD, DKP),
        _pad_heads_cols(w_in[..., o2:o3], GLA_HEADS, GLA_DV_HEAD, DVP),
        _pad_heads_cols(w_in[..., o3:o4], GLA_HEADS, GLA_DV_HEAD, DVP),
        w_in[..., o5:],
        jnp.pad(w_in[..., o4:o5], [(0, 0), (0, 0), (0, LANES - GLA_RANK)]),
    ], axis=-1).astype(BF16)
    wup = jnp.pad(_pad_heads_cols(w_gate_up, GLA_HEADS, GLA_DK_HEAD, DKP),
                  [(0, 0), (0, LANES - GLA_RANK), (0, 0)]).astype(BF16)
    bg = _pad_heads_cols(b_gate[:, None, :], GLA_HEADS, GLA_DK_HEAD, DKP)
    gh = jnp.pad(g_head[:, None, :], [(0, 0), (0, 0), (0, DVP - GLA_DV_HEAD)])
    wo = jnp.concatenate([_pad_heads_rows(w_out[:, :MAIN_W], GLA_HEADS, GLA_DV_HEAD, DVP), w_out[:, MAIN_W:]],
                         axis=1).astype(BF16)
    return w, wup, bg, gh, wo


def kernel(x, mem, norm_ffn, w_ffn_in, w_ffn_out, norm_mix, norm_mem, w_mem_kv, w_out, w_in_a, w_gate_up, b_gate,
           norm_gla_head, w_in_b, norm_shared, w_kv_shared, w_fgate, b_fgate, norm_final):
    batch, seq, _ = x.shape
    h = x.reshape(batch * seq, D_MODEL)

    w1 = w_ffn_in.astype(BF16)
    w2 = w_ffn_out.astype(BF16)
    g_ffn = norm_ffn[:, :, None, :]
    g_mix = norm_mix[:, None, :]
    wkt = jnp.swapaxes(w_mem_kv[:, :, :MEM_W], 1, 2).astype(BF16)
    wv = w_mem_kv[:, :, MEM_W:].astype(BF16)
    kb, vb = _mem_kv(mem, norm_mem[:, None, :], wkt, wv)
    w_a, wup, bg, gh, wo_a = _gla_weights(w_in_a, w_gate_up, b_gate, norm_gla_head, w_out[:N_A])
    wqt_b = jnp.swapaxes(w_in_b[:, :, :MAIN_W], 1, 2).astype(BF16)
    wm_b = w_in_b[:, :, MAIN_W:].astype(BF16)
    wo_b = w_out[N_A:].astype(BF16)

    k_sh = v_sh = aq_sh = ak_sh = None
    for l in range(DEPTH):
        if l == N_A:
            w_sh = jnp.concatenate([w_kv_shared[:, :MAIN_W], jnp.pad(w_fgate, [(0, 0), (0, LANES - FOX_HEADS)])],
                                   axis=1).astype(BF16)
            wvt = w_kv_shared[:, MAIN_W:].T.astype(BF16)
            bf = jnp.pad(b_fgate[None, :], [(0, 0), (0, LANES - FOX_HEADS)])
            k_sh, v_sh, aq_sh, ak_sh = _shared_kv(h, norm_shared[None, :], w_sh, wvt, bf, batch, seq)
        h = _ffn(h, g_ffn, w1, w2, l, 0)
        if l < N_A:
            q, k, v, r, la, q_mem = _gla_proj(h, g_mix, w_a, wup, bg, l, l)
            o_main = _gla(q, k, la, v, r, gh, l, batch, seq)
            h = _mix_out(h, o_main, q_mem, kb, vb, wo_a, l, l, seq)
        else:
            qt, q_mem = _fox_proj(h, g_mix, wqt_b, wm_b, l, l - N_A)
            o_main = _fox_attn(qt, aq_sh, k_sh, ak_sh, v_sh, batch, seq)
            h = _mix_out(h, o_main, q_mem, kb, vb, wo_b, l, l - N_A, seq)
        h = _ffn(h, g_ffn, w1, w2, l, 1, norm_final[None, :] if l == DEPTH - 1 else None)
    return h.reshape(batch, seq, D_MODEL)
```

```python
import numpy as np
import jax
import jax.numpy as jnp
from jax import lax
from jax.experimental import pallas as pl
from jax.experimental.pallas import tpu as pltpu

F32 = jnp.float32
BF16 = jnp.bfloat16

D_MODEL = 1024
DEPTH = 4
N_A = DEPTH // 2
MAIN_W = 768
MEM_W = 256
GLA_HEADS = 4
GLA_DK_HEAD = 96
GLA_DV_HEAD = 192
GLA_RANK = 16
GLA_TEMP = 16.0
FOX_HEADS = 6
FOX_DH = 128
MEM_HEADS = 4
MEM_DH = 64
MEM_LEN = 256
D_FF = 2816
EPS = 1e-6
LOG2E = 1.4426950408889634

LANES = 128
MXU_N = 256
BF16_ROWS = 16
VMEM_LIMIT = 56 * 1024 * 1024

DKP = LANES
DVP = MXU_N
GLA_QW = GLA_HEADS * DKP
GLA_CHUNK = MXU_N
GLA_LEVELS = 8
GLA_MM_LEVELS = 3

TM = 512
PROJ_TM = 1024
FOX_BQ = TM
FOX_BK = TM


def _cparams(sem):
    return pltpu.CompilerParams(dimension_semantics=sem, vmem_limit_bytes=VMEM_LIMIT)


def _rmsnorm(x, g):
    return x * lax.rsqrt(jnp.mean(x * x, axis=-1, keepdims=True) + EPS) * g


def _log_sigmoid(x):
    return jnp.minimum(x, 0.0) - jnp.log(1.0 + jnp.exp(-jnp.abs(x)))


def _sigmoid(x):
    return 1.0 / (1.0 + jnp.exp(-x))


def _split3(x):
    hi = x.astype(BF16)
    r1 = x - hi.astype(F32)
    mid = r1.astype(BF16)
    lo = (r1 - mid.astype(F32)).astype(BF16)
    return jnp.concatenate([hi, mid, lo], axis=1)


def _split2(x):
    hi = x.astype(BF16)
    lo = (x - hi.astype(F32)).astype(BF16)
    return jnp.concatenate([hi, lo], axis=1)


def _dot(a, b):
    return jnp.dot(a, b, preferred_element_type=F32)


def _dot_nt(a, b):
    return lax.dot_general(a, b, (((1,), (1,)), ((), ())), preferred_element_type=F32)


def _dot_tn(a, b):
    return lax.dot_general(a, b, (((0,), (0,)), ((), ())), preferred_element_type=F32)


def _exact_sum3(r, w):
    return r[:, :w] + r[:, w:2 * w] + r[:, 2 * w:]


FFN_BOUNDS = (0, 6 * MXU_N, D_FF)
FFN_TM = 1024


def _ffn_kernel(h_ref, g_ref, w1_ref, w2_ref, *rest):
    o_ref = rest[-1]
    h = h_ref[...]
    xn = _rmsnorm(h, g_ref[0, 0]).astype(BF16)
    y = None
    for f0, f1 in zip(FFN_BOUNDS[:-1], FFN_BOUNDS[1:]):
        a = _dot(xn, w1_ref[0, 0, :, f0:f1])
        c = _dot(xn, w1_ref[0, 0, :, D_FF + f0:D_FF + f1])
        act = (a * _sigmoid(a) * c).astype(BF16)
        part = _dot(act, w2_ref[0, 0, f0:f1, :])
        y = part if y is None else y + part
    out = h + 0.5 * y
    o_ref[...] = _rmsnorm(out, rest[0][...]) if len(rest) == 2 else out


def _ffn(h, g_all, w1_all, w2_all, layer, half, g_final=None):
    n = h.shape[0]
    pick = lambda i: (layer, half, 0, 0)
    in_specs = [
        pl.BlockSpec((FFN_TM, D_MODEL), lambda i: (i, 0)),
        pl.BlockSpec((1, 1, 1, D_MODEL), pick),
        pl.BlockSpec((1, 1, D_MODEL, 2 * D_FF), pick, pipeline_mode=pl.Buffered(1)),
        pl.BlockSpec((1, 1, D_FF, D_MODEL), pick, pipeline_mode=pl.Buffered(1)),
    ]
    args = [h, g_all, w1_all, w2_all]
    if g_final is not None:
        in_specs.append(pl.BlockSpec((1, D_MODEL), lambda i: (0, 0)))
        args.append(g_final)
    return pl.pallas_call(
        _ffn_kernel,
        out_shape=jax.ShapeDtypeStruct((n, D_MODEL), F32),
        grid=(n // FFN_TM,),
        in_specs=in_specs,
        out_specs=pl.BlockSpec((FFN_TM, D_MODEL), lambda i: (i, 0)),
        compiler_params=_cparams(("parallel",)),
        name="ffn",
    )(*args)


GLA_VIN = GLA_HEADS * GLA_DV_HEAD
GLA_PW = 2 * GLA_QW + 2 * GLA_VIN + MEM_W + LANES


def _gla_proj_kernel(h_ref, g_ref, w_ref, wup_ref, bg_ref, q_ref, k_ref, v_ref, r_ref, la_ref, qm_ref):
    xn = _rmsnorm(h_ref[...], g_ref[...]).astype(BF16)
    oq, ok, ov = 0, GLA_QW, 2 * GLA_QW
    orr, om, og = ov + GLA_VIN, ov + 2 * GLA_VIN, ov + 2 * GLA_VIN + MEM_W
    g_low = _dot(xn, w_ref[:, og:og + LANES]).astype(BF16)
    q_ref[...] = (_dot(xn, w_ref[:, oq:oq + GLA_QW]) * (GLA_DK_HEAD ** -0.5)).astype(BF16)
    x = _dot(g_low, wup_ref[...]) + bg_ref[...]
    k_ref[...] = _dot(xn, w_ref[:, ok:ok + GLA_QW]).astype(BF16)
    la_ref[...] = _log_sigmoid(x) * (LOG2E / GLA_TEMP)
    v_ref[...] = _dot(xn, w_ref[:, ov:ov + GLA_VIN]).astype(BF16)
    r_ref[...] = _dot(xn, w_ref[:, orr:orr + GLA_VIN]).astype(BF16)
    qm_ref[...] = _dot(xn, w_ref[:, om:om + MEM_W]).astype(BF16)


def _gla_proj(h, g, w, wup, bg, layer, a):
    n = h.shape[0]
    row = lambda i: (i, 0)
    return pl.pallas_call(
        _gla_proj_kernel,
        out_shape=(
            jax.ShapeDtypeStruct((n, GLA_QW), BF16),
            jax.ShapeDtypeStruct((n, GLA_QW), BF16),
            jax.ShapeDtypeStruct((n, GLA_VIN), BF16),
            jax.ShapeDtypeStruct((n, GLA_VIN), BF16),
            jax.ShapeDtypeStruct((n, GLA_QW), F32),
            jax.ShapeDtypeStruct((n, MEM_W), BF16),
        ),
        grid=(n // PROJ_TM,),
        in_specs=[
            pl.BlockSpec((PROJ_TM, D_MODEL), row),
            pl.BlockSpec((None, 1, D_MODEL), lambda i: (layer, 0, 0)),
            pl.BlockSpec((None, D_MODEL, GLA_PW), lambda i: (a, 0, 0), pipeline_mode=pl.Buffered(1)),
            pl.BlockSpec((None, LANES, GLA_QW), lambda i: (a, 0, 0)),
            pl.BlockSpec((None, 1, GLA_QW), lambda i: (a, 0, 0)),
        ],
        out_specs=(
            pl.BlockSpec((PROJ_TM, GLA_QW), row),
            pl.BlockSpec((PROJ_TM, GLA_QW), row),
            pl.BlockSpec((PROJ_TM, GLA_VIN), row),
            pl.BlockSpec((PROJ_TM, GLA_VIN), row),
            pl.BlockSpec((PROJ_TM, GLA_QW), row),
            pl.BlockSpec((PROJ_TM, MEM_W), row),
        ),
        compiler_params=_cparams(("parallel",)),
        name="gla_proj",
    )(h, g, w, wup, bg)


GLA_TS = 1024
GLA_UNROLL = 2


def _gla_consts():
    c = GLA_CHUNK
    i = np.arange(c)[:, None]
    j = np.arange(c)[None, :]
    tril = (j <= i)
    b_rows, masks = [], []
    for lv in range(GLA_LEVELS):
        hs = 1 << lv
        mid = (i // (2 * hs)) * (2 * hs) + hs
        lower = (i % (2 * hs)) >= hs
        b_rows.append(np.where(lower, (j >= mid) & (j <= i), (j > i) & (j < mid)))
        masks.append(((i // (2 * hs)) == (j // (2 * hs))) & lower & ((j % (2 * hs)) < hs))
    masks.append(i == j)
    asb = lambda x: jnp.asarray(x.astype(np.float32), dtype=BF16)
    return asb(tril), asb(np.concatenate(b_rows[:GLA_MM_LEVELS], axis=0)), asb(np.stack(masks, axis=0))


def _gla_kernel(q_ref, k_ref, la_ref, v_ref, r_ref, gh_ref, tril_ref, b_ref, m_ref, o_ref, st_ref):
    c = GLA_CHUNK

    @pl.when(pl.program_id(1) == 0)
    def _():
        st_ref[...] = jnp.zeros_like(st_ref)

    gh = gh_ref[...]

    heads = [(slice(h * DKP, (h + 1) * DKP), slice(h * DVP, (h + 1) * DVP)) for h in range(GLA_HEADS)]

    def head_cols(x, h):
        piece = x[:, h * GLA_DV_HEAD:(h + 1) * GLA_DV_HEAD]
        return jnp.concatenate([piece, jnp.zeros((c, DVP - GLA_DV_HEAD), x.dtype)], axis=1)

    def decays(r0):
        g = la_ref[pl.ds(r0, c), :]
        cum = _dot(tril_ref[...], _split2(g))
        cum = cum[:, :GLA_QW] + cum[:, GLA_QW:]
        small = _dot(b_ref[...], g.astype(BF16))
        lev = [small[lv * c:(lv + 1) * c] for lv in range(GLA_MM_LEVELS)]
        for lv in range(GLA_MM_LEVELS, GLA_LEVELS):
            hs = 1 << lv
            parts = []
            for mid in range(hs, c, 2 * hs):
                parts += [cum[mid - 1:mid, :] - cum[mid - hs:mid, :], cum[mid:mid + hs, :] - cum[mid - 1:mid, :]]
            lev.append(jnp.concatenate(parts, axis=0))
        return cum, lev

    def intra(r0, cum, lev):
        qa = q_ref[pl.ds(r0, c), :]
        ka = k_ref[pl.ds(r0, c), :]
        v = v_ref[pl.ds(r0, c), :]
        per_head = []
        for h, (ks, vs) in enumerate(heads):
            qb, kb, vh = qa[:, ks], ka[:, ks], head_cols(v, h)
            cum_h = cum[:, ks]
            last_h = cum_h[c - 1:c, :]
            s = _dot_nt(qb, kb).astype(BF16) * m_ref[GLA_LEVELS]
            for lv in range(GLA_LEVELS):
                e = jnp.exp2(lev[lv][:, ks]).astype(BF16)
                s = s + _dot_nt(qb * e, kb * e).astype(BF16) * m_ref[lv]
            kd = (kb.astype(F32) * jnp.exp2(last_h - cum_h)).astype(BF16)
            qg = (qb.astype(F32) * jnp.exp2(cum_h)).astype(BF16)
            per_head.append((qg, s, _dot_tn(vh, kd), jnp.exp2(last_h)))
        return per_head

    def recur(r0, per_head):
        v = v_ref[pl.ds(r0, c), :]
        r = r_ref[pl.ds(r0, c), :]
        for h, (ks, vs) in enumerate(heads):
            qg, s, upd, last_decay = per_head[h]
            st = st_ref[h]
            o = _dot_nt(qg, st.astype(BF16)) + _dot(s, head_cols(v, h))
            st_ref[h] = st * last_decay + upd
            ms = jnp.sum(o * o, axis=-1, keepdims=True) * (1.0 / GLA_DV_HEAD)
            rh = head_cols(r, h).astype(F32)
            y = o * lax.rsqrt(ms + EPS) * gh * (rh * _sigmoid(rh))
            o_ref[pl.ds(r0, c), h * GLA_DV_HEAD:(h + 1) * GLA_DV_HEAD] = y[:, :GLA_DV_HEAD].astype(BF16)

    def group(gi, carry):
        offs = [pl.multiple_of((gi * GLA_UNROLL + u) * c, c) for u in range(GLA_UNROLL)]
        dec = [decays(r0) for r0 in offs]
        cur = [intra(r0, *d) for r0, d in zip(offs, dec)]
        for r0, per_head in zip(offs, cur):
            recur(r0, per_head)
        return carry

    lax.fori_loop(0, GLA_TS // (c * GLA_UNROLL), group, 0)


def _gla(q, k, la, v, r, gh, a, batch, seq):
    n = q.shape[0]
    nt = seq // GLA_TS
    tril, bmat, m = _gla_consts()
    row = lambda b, t: (b * nt + t, 0)
    fixed2 = lambda b, t: (0, 0)
    fixed3 = lambda b, t: (0, 0, 0)
    return pl.pallas_call(
        _gla_kernel,
        out_shape=jax.ShapeDtypeStruct((n, GLA_VIN), BF16),
        grid=(batch, nt),
        in_specs=[
            pl.BlockSpec((GLA_TS, GLA_QW), row),
            pl.BlockSpec((GLA_TS, GLA_QW), row),
            pl.BlockSpec((GLA_TS, GLA_QW), row),
            pl.BlockSpec((GLA_TS, GLA_VIN), row),
            pl.BlockSpec((GLA_TS, GLA_VIN), row),
            pl.BlockSpec((None, 1, DVP), lambda b, t: (a, 0, 0)),
            pl.BlockSpec(tril.shape, fixed2),
            pl.BlockSpec(bmat.shape, fixed2),
            pl.BlockSpec(m.shape, fixed3),
        ],
        out_specs=pl.BlockSpec((GLA_TS, GLA_VIN), row),
        scratch_shapes=[pltpu.VMEM((GLA_HEADS, DVP, DKP), F32)],
        compiler_params=_cparams(("arbitrary", "arbitrary")),
        name="gla",
    )(q, k, la, v, r, gh, tril, bmat, m)


MEM_BW = MEM_HEADS * MEM_LEN


def _mem_kv_kernel(mem_ref, g_ref, wkt_ref, wv_ref, kb_ref, vb_ref):
    d_row = lax.broadcasted_iota(jnp.int32, (MEM_W, MEM_LEN), 0) // MEM_DH
    d_col = lax.broadcasted_iota(jnp.int32, (MEM_LEN, MEM_W), 1) // MEM_DH
    for b in range(mem_ref.shape[0]):
        mn = _rmsnorm(mem_ref[b], g_ref[0]).astype(BF16)
        kt = _dot_nt(wkt_ref[0], mn) * (MEM_DH ** -0.5)
        v = _dot(mn, wv_ref[0])
        for hh in range(MEM_HEADS):
            kb_ref[0, b, :, hh * MEM_LEN:(hh + 1) * MEM_LEN] = jnp.where(d_row == hh, kt, 0.0).astype(BF16)
            vb_ref[0, b, hh * MEM_LEN:(hh + 1) * MEM_LEN, :] = jnp.where(d_col == hh, v, 0.0).astype(BF16)


def _mem_kv(mem, g, wkt, wv):
    b = mem.shape[0]
    return pl.pallas_call(
        _mem_kv_kernel,
        out_shape=(
            jax.ShapeDtypeStruct((DEPTH, b, MEM_W, MEM_BW), BF16),
            jax.ShapeDtypeStruct((DEPTH, b, MEM_BW, MEM_W), BF16),
        ),
        grid=(DEPTH,),
        in_specs=[
            pl.BlockSpec((b, MEM_LEN, D_MODEL), lambda l: (0, 0, 0)),
            pl.BlockSpec((1, 1, D_MODEL), lambda l: (l, 0, 0)),
            pl.BlockSpec((1, MEM_W, D_MODEL), lambda l: (l, 0, 0)),
            pl.BlockSpec((1, D_MODEL, MEM_W), lambda l: (l, 0, 0)),
        ],
        out_specs=(
            pl.BlockSpec((1, b, MEM_W, MEM_BW), lambda l: (l, 0, 0, 0)),
            pl.BlockSpec((1, b, MEM_BW, MEM_W), lambda l: (l, 0, 0, 0)),
        ),
        compiler_params=_cparams(("parallel",)),
        name="mem_kv",
    )(mem, g, wkt, wv)


def _mix_out_kernel(h_ref, om_ref, qm_ref, kb_ref, vb_ref, wo_ref, o_ref):
    lg = _dot(qm_ref[...], kb_ref[0, 0])
    ps = []
    for hh in range(MEM_HEADS):
        s = lg[:, hh * MEM_LEN:(hh + 1) * MEM_LEN]
        e = jnp.exp(s - jnp.max(s, axis=-1, keepdims=True))
        ps.append((e / jnp.sum(e, axis=-1, keepdims=True)).astype(BF16))
    o_mem = _dot(jnp.concatenate(ps, axis=1), vb_ref[0, 0])
    wm = om_ref.shape[1]
    y = _dot(om_ref[...], wo_ref[0:wm, :]) + _dot(o_mem.astype(BF16), wo_ref[wm:wm + MEM_W, :])
    o_ref[...] = h_ref[...] + y


def _mix_out(h, o_main, q_mem, kb, vb, wo, layer, seq):
    n = h.shape[0]
    wm = o_main.shape[1]
    per_b = seq // PROJ_TM
    row = lambda i: (i, 0)
    return pl.pallas_call(
        _mix_out_kernel,
        out_shape=jax.ShapeDtypeStruct((n, D_MODEL), F32),
        grid=(n // PROJ_TM,),
        in_specs=[
            pl.BlockSpec((PROJ_TM, D_MODEL), row),
            pl.BlockSpec((PROJ_TM, wm), row),
            pl.BlockSpec((PROJ_TM, MEM_W), row),
            pl.BlockSpec((1, 1, MEM_W, MEM_BW), lambda i: (layer, i // per_b, 0, 0)),
            pl.BlockSpec((1, 1, MEM_BW, MEM_W), lambda i: (layer, i // per_b, 0, 0)),
            pl.BlockSpec((None, wm + MEM_W, D_MODEL), lambda i: (layer, 0, 0)),
        ],
        out_specs=pl.BlockSpec((PROJ_TM, D_MODEL), row),
        compiler_params=_cparams(("parallel",)),
        name="mix_out",
    )(h, o_main, q_mem, kb, vb, wo)


SH_PW = MAIN_W + LANES


ONE_LANE = LANES - 1
AUX_SLOT = BF16_ROWS


def _fox_aux_consts():
    pqt = np.zeros((LANES, 3 * LANES), np.float32)
    pk = np.zeros((3 * LANES, LANES), np.float32)
    for h in range(FOX_HEADS):
        for p in range(3):
            pqt[h * AUX_SLOT + p, p * LANES + h] = 1.0
            pqt[h * AUX_SLOT + 3 + p, ONE_LANE] = 1.0
            pk[ONE_LANE, h * AUX_SLOT + p] = 1.0
            pk[p * LANES + h, h * AUX_SLOT + 3 + p] = -1.0
    tril = np.tril(np.ones((TM, TM), np.float32))
    return jnp.asarray(tril, dtype=BF16), jnp.asarray(pqt, dtype=BF16), jnp.asarray(pk, dtype=BF16)


def _shared_kernel(h_ref, g_ref, w_ref, wvt_ref, bf_ref, tril_ref, pqt_ref, pk_ref,
                   k_ref, vt_ref, aqt_ref, ak_ref, carry_ref):
    @pl.when(pl.program_id(1) == 0)
    def _():
        carry_ref[...] = jnp.zeros_like(carry_ref)

    xn = _rmsnorm(h_ref[...], g_ref[...]).astype(BF16)
    log_f = _log_sigmoid(_dot(xn, w_ref[:, MAIN_W:SH_PW]) + bf_ref[...])
    k_ref[...] = _dot(xn, w_ref[:, 0:MAIN_W]).astype(BF16)
    d = _exact_sum3(_dot(tril_ref[...], _split3(log_f)), LANES) + carry_ref[...]
    carry_ref[...] = d[TM - 1:TM, :]
    d3 = _split3(d * LOG2E)
    lane = lax.broadcasted_iota(jnp.int32, d3.shape, 1)
    d3 = jnp.where(lane == ONE_LANE, jnp.ones_like(d3), d3)
    vt_ref[0, 0] = _dot_nt(wvt_ref[...], xn).astype(BF16)
    aqt_ref[0, 0] = _dot_nt(pqt_ref[...], d3).astype(BF16)
    ak_ref[...] = _dot(d3, pk_ref[...]).astype(BF16)


def _shared_kv(h, g, w, wvt, bf, batch, seq):
    n = h.shape[0]
    nt = seq // TM
    tril, pqt, pk = _fox_aux_consts()
    row = lambda b, t: (b * nt + t, 0)
    fixed = lambda b, t: (0, 0)
    slab = lambda b, t: (b, t, 0, 0)
    return pl.pallas_call(
        _shared_kernel,
        out_shape=(
            jax.ShapeDtypeStruct((n, MAIN_W), BF16),
            jax.ShapeDtypeStruct((batch, nt, MAIN_W, TM), BF16),
            jax.ShapeDtypeStruct((batch, nt, LANES, TM), BF16),
            jax.ShapeDtypeStruct((n, LANES), BF16),
        ),
        grid=(batch, nt),
        in_specs=[
            pl.BlockSpec((TM, D_MODEL), row),
            pl.BlockSpec((1, D_MODEL), fixed),
            pl.BlockSpec((D_MODEL, SH_PW), fixed),
            pl.BlockSpec((MAIN_W, D_MODEL), fixed),
            pl.BlockSpec((1, LANES), fixed),
            pl.BlockSpec((TM, TM), fixed),
            pl.BlockSpec((LANES, 3 * LANES), fixed),
            pl.BlockSpec((3 * LANES, LANES), fixed),
        ],
        out_specs=(
            pl.BlockSpec((TM, MAIN_W), row),
            pl.BlockSpec((1, 1, MAIN_W, TM), slab),
            pl.BlockSpec((1, 1, LANES, TM), slab),
            pl.BlockSpec((TM, LANES), row),
        ),
        scratch_shapes=[pltpu.VMEM((1, LANES), F32)],
        compiler_params=_cparams(("arbitrary", "arbitrary")),
        name="shared_kv",
    )(h, g, w, wvt, bf, tril, pqt, pk)


def _fox_proj_kernel(h_ref, g_ref, wqt_ref, wm_ref, qt_ref, qm_ref):
    xn = _rmsnorm(h_ref[...], g_ref[...]).astype(BF16)
    qt = (_dot_nt(wqt_ref[...], xn) * (FOX_DH ** -0.5 * LOG2E)).astype(BF16)
    for s in range(PROJ_TM // TM):
        qt_ref[s] = qt[:, s * TM:(s + 1) * TM]
    qm_ref[...] = _dot(xn, wm_ref[...]).astype(BF16)


def _fox_proj(h, g, wqt, wm, layer, j):
    n = h.shape[0]
    row = lambda i: (i, 0)
    return pl.pallas_call(
        _fox_proj_kernel,
        out_shape=(jax.ShapeDtypeStruct((n // TM, MAIN_W, TM), BF16), jax.ShapeDtypeStruct((n, MEM_W), BF16)),
        grid=(n // PROJ_TM,),
        in_specs=[
            pl.BlockSpec((PROJ_TM, D_MODEL), row),
            pl.BlockSpec((None, 1, D_MODEL), lambda i: (layer, 0, 0)),
            pl.BlockSpec((None, MAIN_W, D_MODEL), lambda i: (j, 0, 0)),
            pl.BlockSpec((None, D_MODEL, MEM_W), lambda i: (j, 0, 0)),
        ],
        out_specs=(pl.BlockSpec((PROJ_TM // TM, MAIN_W, TM), lambda i: (i, 0, 0)),
                   pl.BlockSpec((PROJ_TM, MEM_W), row)),
        compiler_params=_cparams(("parallel",)),
        name="fox_proj",
    )(h, g, wqt, wm)


FOX_HPS = 3


FOX_SUB = MXU_N
FOX_NSUB = FOX_BK // FOX_SUB


def _fox_attn_kernel(qt_ref, aqt_ref, k_ref, ak_ref, vt_ref, o_ref):
    qi = pl.program_id(2)
    chains = [(hh, qs) for hh in range(FOX_HPS) for qs in range(FOX_BQ // FOX_SUB)]

    def head(hh):
        return slice(hh * FOX_DH, (hh + 1) * FOX_DH)

    def strip(s):
        return slice(s * FOX_SUB, (s + 1) * FOX_SUB)

    slot = lax.broadcasted_iota(jnp.int32, (LANES, FOX_SUB), 0) // AUX_SLOT
    first_head = pl.program_id(1) * FOX_HPS
    qq = [jnp.concatenate([qt_ref[0, 0, head(hh), strip(qs)],
                           jnp.where(slot == first_head + hh, aqt_ref[0, 0, :, strip(qs)], jnp.zeros((), BF16))],
                          axis=0)
          for hh, qs in chains]

    def scores(kj, diagonal):
        c0 = pl.multiple_of(kj * FOX_BK, FOX_BK)
        sts = {}
        for ks in range(FOX_NSUB):
            rows = pl.ds(c0 + ks * FOX_SUB, FOX_SUB)
            ak = ak_ref[0, rows, :]
            kk = [jnp.concatenate([k_ref[0, rows, head(hh)], ak], axis=1) for hh in range(FOX_HPS)]
            for ci, (hh, qs) in enumerate(chains):
                if diagonal and ks > qs:
                    continue
                st = _dot(kk[hh], qq[ci])
                if diagonal and ks == qs:
                    key = lax.broadcasted_iota(jnp.int32, st.shape, 0)
                    qry = lax.broadcasted_iota(jnp.int32, st.shape, 1)
                    st = jnp.where(key <= qry, st, -jnp.inf)
                sts[ks, ci] = st
        return sts

    def update(kj, carry, sts):
        carry = list(carry)
        for ks in range(FOX_NSUB):
            vt1 = [jnp.concatenate([vt_ref[0, kj, head(hh), strip(ks)], ones], axis=0) for hh in range(FOX_HPS)]
            for ci, (hh, qs) in enumerate(chains):
                if (ks, ci) not in sts:
                    continue
                m, acc = carry[ci]
                st = sts[ks, ci]
                m_new = jnp.maximum(m, jnp.max(st, axis=0, keepdims=True))
                alpha = jnp.exp2(m - m_new)
                p = jnp.exp2(st - m_new).astype(BF16)
                acc = alpha * acc + _dot(vt1[hh], p)
                carry[ci] = (m_new, acc)
        return tuple(carry)

    ones = jnp.ones((BF16_ROWS, FOX_SUB), BF16)
    init = tuple((jnp.full((1, FOX_SUB), -jnp.inf, F32), jnp.zeros((FOX_DH + BF16_ROWS, FOX_SUB), F32))
                 for _ in chains)

    def pair(i, carry):
        kj = 2 * i
        sts0, sts1 = scores(kj, False), scores(kj + 1, False)
        return update(kj + 1, update(kj, carry, sts0), sts1)

    def tail_with_leftover(carry):
        sts0, sts1 = scores(qi - 1, False), scores(qi, True)
        return update(qi, update(qi - 1, carry, sts0), sts1)

    carry = lax.fori_loop(0, qi // 2, pair, init)
    carry = lax.cond(qi % 2 == 1, tail_with_leftover, lambda cr: update(qi, cr, scores(qi, True)), carry)
    for ci, (hh, qs) in enumerate(chains):
        _, acc = carry[ci]
        o_ref[0, strip(qs), head(hh)] = (acc[:FOX_DH] / acc[FOX_DH:FOX_DH + 1]).T.astype(BF16)


def _fox_attn(qt, aqt, k, ak, vt, batch, seq):
    w = FOX_HPS * FOX_DH
    nq = seq // FOX_BQ
    qblk = pl.BlockSpec((1, 1, w, FOX_BQ), lambda b, g, i: (b, i, g, 0))
    kblk = pl.BlockSpec((1, seq, w), lambda b, g, i: (b, 0, g))
    vblk = pl.BlockSpec((1, seq // FOX_BK, w, FOX_BK), lambda b, g, i: (b, 0, g, 0))
    aqblk = pl.BlockSpec((1, 1, LANES, FOX_BQ), lambda b, g, i: (b, i, 0, 0))
    akblk = pl.BlockSpec((1, seq, LANES), lambda b, g, i: (b, 0, 0))
    out = pl.pallas_call(
        _fox_attn_kernel,
        out_shape=jax.ShapeDtypeStruct((batch, seq, MAIN_W), BF16),
        grid=(batch, FOX_HEADS // FOX_HPS, nq),
        in_specs=[qblk, aqblk, kblk, akblk, vblk],
        out_specs=pl.BlockSpec((1, FOX_BQ, w), lambda b, g, i: (b, i, g)),
        compiler_params=_cparams(("parallel", "parallel", "arbitrary")),
        name="fox_attn",
    )(qt.reshape(batch, nq, MAIN_W, FOX_BQ), aqt, k.reshape(batch, seq, MAIN_W), ak.reshape(batch, seq, LANES), vt)
    return out.reshape(batch * seq, MAIN_W)


def _pad_heads_cols(w, heads, width, padded):
    lead = w.shape[:-1]
    w = w.reshape(lead + (heads, width))
    w = jnp.pad(w, [(0, 0)] * len(lead) + [(0, 0), (0, padded - width)])
    return w.reshape(lead + (heads * padded,))


def _gla_weights(w_in, w_gate_up, b_gate, g_head):
    dk = GLA_HEADS * GLA_DK_HEAD
    dv = GLA_HEADS * GLA_DV_HEAD
    o1, o2, o3, o4, o5 = dk, 2 * dk, 2 * dk + dv, 2 * dk + 2 * dv, 2 * dk + 2 * dv + GLA_RANK
    w = jnp.concatenate([
        _pad_heads_cols(w_in[..., :o1], GLA_HEADS, GLA_DK_HEAD, DKP),
        _pad_heads_cols(w_in[..., o1:o2], GLA_HEADS, GLA_DK_HEAD, DKP),
        w_in[..., o2:o4],
        w_in[..., o5:],
        jnp.pad(w_in[..., o4:o5], [(0, 0), (0, 0), (0, LANES - GLA_RANK)]),
    ], axis=-1).astype(BF16)
    wup = jnp.pad(_pad_heads_cols(w_gate_up, GLA_HEADS, GLA_DK_HEAD, DKP),
                  [(0, 0), (0, LANES - GLA_RANK), (0, 0)]).astype(BF16)
    bg = _pad_heads_cols(b_gate[:, None, :], GLA_HEADS, GLA_DK_HEAD, DKP)
    gh = jnp.pad(g_head[:, None, :], [(0, 0), (0, 0), (0, DVP - GLA_DV_HEAD)])
    return w, wup, bg, gh


def kernel(x, mem, norm_ffn, w_ffn_in, w_ffn_out, norm_mix, norm_mem, w_mem_kv, w_out, w_in_a, w_gate_up, b_gate,
           norm_gla_head, w_in_b, norm_shared, w_kv_shared, w_fgate, b_fgate, norm_final):
    batch, seq, _ = x.shape
    h = x.reshape(batch * seq, D_MODEL)

    w1 = w_ffn_in.astype(BF16)
    w2 = w_ffn_out.astype(BF16)
    g_ffn = norm_ffn[:, :, None, :]
    g_mix = norm_mix[:, None, :]
    wkt = jnp.swapaxes(w_mem_kv[:, :, :MEM_W], 1, 2).astype(BF16)
    wv = w_mem_kv[:, :, MEM_W:].astype(BF16)
    kb, vb = _mem_kv(mem, norm_mem[:, None, :], wkt, wv)
    w_a, wup, bg, gh = _gla_weights(w_in_a, w_gate_up, b_gate, norm_gla_head)
    wqt_b = jnp.swapaxes(w_in_b[:, :, :MAIN_W], 1, 2).astype(BF16)
    wm_b = w_in_b[:, :, MAIN_W:].astype(BF16)
    wo = w_out.astype(BF16)

    k_sh = v_sh = aq_sh = ak_sh = None
    for l in range(DEPTH):
        if l == N_A:
            w_sh = jnp.concatenate([w_kv_shared[:, :MAIN_W], jnp.pad(w_fgate, [(0, 0), (0, LANES - FOX_HEADS)])],
                                   axis=1).astype(BF16)
            wvt = w_kv_shared[:, MAIN_W:].T.astype(BF16)
            bf = jnp.pad(b_fgate[None, :], [(0, 0), (0, LANES - FOX_HEADS)])
            k_sh, v_sh, aq_sh, ak_sh = _shared_kv(h, norm_shared[None, :], w_sh, wvt, bf, batch, seq)
        h = _ffn(h, g_ffn, w1, w2, l, 0)
        if l < N_A:
            q, k, v, r, la, q_mem = _gla_proj(h, g_mix, w_a, wup, bg, l, l)
            o_main = _gla(q, k, la, v, r, gh, l, batch, seq)
            h = _mix_out(h, o_main, q_mem, kb, vb, wo, l, seq)
        else:
            qt, q_mem = _fox_proj(h, g_mix, wqt_b, wm_b, l, l - N_A)
            o_main = _fox_attn(qt, aq_sh, k_sh, ak_sh, v_sh, batch, seq)
            h = _mix_out(h, o_main, q_mem, kb, vb, wo, l, seq)
        h = _ffn(h, g_ffn, w1, w2, l, 1, norm_final[None, :] if l == DEPTH - 1 else None)
    return h.reshape(batch, seq, D_MODEL)
```

```python
import numpy as np
import jax
import jax.numpy as jnp
from jax import lax
from jax.experimental import pallas as pl
from jax.experimental.pallas import tpu as pltpu

F32 = jnp.float32
BF16 = jnp.bfloat16

D_MODEL = 1024
DEPTH = 4
N_A = DEPTH // 2
MAIN_W = 768
MEM_W = 256
GLA_HEADS = 4
GLA_DK_HEAD = 96
GLA_DV_HEAD = 192
GLA_RANK = 16
GLA_TEMP = 16.0
FOX_HEADS = 6
FOX_DH = 128
MEM_HEADS = 4
MEM_DH = 64
MEM_LEN = 256
D_FF = 2816
EPS = 1e-6
LOG2E = 1.4426950408889634

LANES = 128
MXU_N = 256
BF16_ROWS = 16
VMEM_LIMIT = 56 * 1024 * 1024

DKP = LANES
DVP = MXU_N
GLA_QW = GLA_HEADS * DKP
GLA_CHUNK = MXU_N
GLA_LEVELS = 8
GLA_MM_LEVELS = 3

TM = 512
PROJ_TM = 1024
FOX_BQ = TM
FOX_BK = TM


def _cparams(sem):
    return pltpu.CompilerParams(dimension_semantics=sem, vmem_limit_bytes=VMEM_LIMIT)


def _rmsnorm(x, g):
    return x * lax.rsqrt(jnp.mean(x * x, axis=-1, keepdims=True) + EPS) * g


def _log_sigmoid(x):
    return jnp.minimum(x, 0.0) - jnp.log(1.0 + jnp.exp(-jnp.abs(x)))


def _sigmoid(x):
    return 1.0 / (1.0 + jnp.exp(-x))


def _split3(x):
    hi = x.astype(BF16)
    r1 = x - hi.astype(F32)
    mid = r1.astype(BF16)
    lo = (r1 - mid.astype(F32)).astype(BF16)
    return jnp.concatenate([hi, mid, lo], axis=1)


def _split2(x):
    hi = x.astype(BF16)
    lo = (x - hi.astype(F32)).astype(BF16)
    return jnp.concatenate([hi, lo], axis=1)


def _dot(a, b):
    return jnp.dot(a, b, preferred_element_type=F32)


def _dot_nt(a, b):
    return lax.dot_general(a, b, (((1,), (1,)), ((), ())), preferred_element_type=F32)


def _dot_tn(a, b):
    return lax.dot_general(a, b, (((0,), (0,)), ((), ())), preferred_element_type=F32)


def _exact_sum3(r, w):
    return r[:, :w] + r[:, w:2 * w] + r[:, 2 * w:]


FFN_BOUNDS = (0, 6 * MXU_N, D_FF)
FFN_TM = 1024


def _ffn_kernel(h_ref, g_ref, w1_ref, w2_ref, *rest):
    o_ref = rest[-1]
    h = h_ref[...]
    xn = _rmsnorm(h, g_ref[0, 0]).astype(BF16)
    y = None
    for f0, f1 in zip(FFN_BOUNDS[:-1], FFN_BOUNDS[1:]):
        a = _dot(xn, w1_ref[0, 0, :, f0:f1])
        c = _dot(xn, w1_ref[0, 0, :, D_FF + f0:D_FF + f1])
        act = (a * _sigmoid(a) * c).astype(BF16)
        part = _dot(act, w2_ref[0, 0, f0:f1, :])
        y = part if y is None else y + part
    out = h + 0.5 * y
    o_ref[...] = _rmsnorm(out, rest[0][...]) if len(rest) == 2 else out


def _ffn(h, g_all, w1_all, w2_all, layer, half, g_final=None):
    n = h.shape[0]
    pick = lambda i: (layer, half, 0, 0)
    in_specs = [
        pl.BlockSpec((FFN_TM, D_MODEL), lambda i: (i, 0)),
        pl.BlockSpec((1, 1, 1, D_MODEL), pick),
        pl.BlockSpec((1, 1, D_MODEL, 2 * D_FF), pick, pipeline_mode=pl.Buffered(1)),
        pl.BlockSpec((1, 1, D_FF, D_MODEL), pick, pipeline_mode=pl.Buffered(1)),
    ]
    args = [h, g_all, w1_all, w2_all]
    if g_final is not None:
        in_specs.append(pl.BlockSpec((1, D_MODEL), lambda i: (0, 0)))
        args.append(g_final)
    return pl.pallas_call(
        _ffn_kernel,
        out_shape=jax.ShapeDtypeStruct((n, D_MODEL), F32),
        grid=(n // FFN_TM,),
        in_specs=in_specs,
        out_specs=pl.BlockSpec((FFN_TM, D_MODEL), lambda i: (i, 0)),
        compiler_params=_cparams(("parallel",)),
        name="ffn",
    )(*args)


GLA_QIN = GLA_HEADS * GLA_DK_HEAD
GLA_VIN = GLA_HEADS * GLA_DV_HEAD
GLA_PW = 2 * GLA_QIN + 2 * GLA_VIN + MEM_W + LANES


def _gla_proj_kernel(h_ref, g_ref, w_ref, wup_ref, bg_ref, q_ref, k_ref, v_ref, r_ref, la_ref, qm_ref):
    xn = _rmsnorm(h_ref[...], g_ref[...]).astype(BF16)
    oq, ov = 0, 2 * GLA_QIN
    orr, om, og = ov + GLA_VIN, ov + 2 * GLA_VIN, ov + 2 * GLA_VIN + MEM_W
    g_low = _dot(xn, w_ref[:, og:og + LANES]).astype(BF16)
    qk = _dot(xn, w_ref[:, oq:ov])
    x = _dot(g_low, wup_ref[...]) + bg_ref[...]
    q_ref[...] = (qk[:, :GLA_QIN] * (GLA_DK_HEAD ** -0.5)).astype(BF16)
    k_ref[...] = qk[:, GLA_QIN:].astype(BF16)
    la_ref[...] = _log_sigmoid(x) * (LOG2E / GLA_TEMP)
    v_ref[...] = _dot(xn, w_ref[:, ov:ov + GLA_VIN]).astype(BF16)
    r_ref[...] = _dot(xn, w_ref[:, orr:orr + GLA_VIN]).astype(BF16)
    qm_ref[...] = _dot(xn, w_ref[:, om:om + MEM_W]).astype(BF16)


def _gla_proj(h, g, w, wup, bg, layer, a):
    n = h.shape[0]
    row = lambda i: (i, 0)
    return pl.pallas_call(
        _gla_proj_kernel,
        out_shape=(
            jax.ShapeDtypeStruct((n, GLA_QIN), BF16),
            jax.ShapeDtypeStruct((n, GLA_QIN), BF16),
            jax.ShapeDtypeStruct((n, GLA_VIN), BF16),
            jax.ShapeDtypeStruct((n, GLA_VIN), BF16),
            jax.ShapeDtypeStruct((n, GLA_QW), F32),
            jax.ShapeDtypeStruct((n, MEM_W), BF16),
        ),
        grid=(n // PROJ_TM,),
        in_specs=[
            pl.BlockSpec((PROJ_TM, D_MODEL), row),
            pl.BlockSpec((None, 1, D_MODEL), lambda i: (layer, 0, 0)),
            pl.BlockSpec((None, D_MODEL, GLA_PW), lambda i: (a, 0, 0), pipeline_mode=pl.Buffered(1)),
            pl.BlockSpec((None, LANES, GLA_QW), lambda i: (a, 0, 0)),
            pl.BlockSpec((None, 1, GLA_QW), lambda i: (a, 0, 0)),
        ],
        out_specs=(
            pl.BlockSpec((PROJ_TM, GLA_QIN), row),
            pl.BlockSpec((PROJ_TM, GLA_QIN), row),
            pl.BlockSpec((PROJ_TM, GLA_VIN), row),
            pl.BlockSpec((PROJ_TM, GLA_VIN), row),
            pl.BlockSpec((PROJ_TM, GLA_QW), row),
            pl.BlockSpec((PROJ_TM, MEM_W), row),
        ),
        compiler_params=_cparams(("parallel",)),
        name="gla_proj",
    )(h, g, w, wup, bg)


GLA_TS = 1024
GLA_UNROLL = 2


def _gla_consts():
    c = GLA_CHUNK
    i = np.arange(c)[:, None]
    j = np.arange(c)[None, :]
    tril = (j <= i)
    b_rows, masks = [], []
    for lv in range(GLA_LEVELS):
        hs = 1 << lv
        mid = (i // (2 * hs)) * (2 * hs) + hs
        lower = (i % (2 * hs)) >= hs
        b_rows.append(np.where(lower, (j >= mid) & (j <= i), (j > i) & (j < mid)))
        masks.append(((i // (2 * hs)) == (j // (2 * hs))) & lower & ((j % (2 * hs)) < hs))
    masks.append(i == j)
    asb = lambda x: jnp.asarray(x.astype(np.float32), dtype=BF16)
    return asb(tril), asb(np.concatenate(b_rows[:GLA_MM_LEVELS], axis=0)), asb(np.stack(masks, axis=0))


def _gla_kernel(q_ref, k_ref, la_ref, v_ref, r_ref, gh_ref, tril_ref, b_ref, m_ref, o_ref, st_ref):
    c = GLA_CHUNK

    @pl.when(pl.program_id(1) == 0)
    def _():
        st_ref[...] = jnp.zeros_like(st_ref)

    gh = gh_ref[...]

    heads = [(slice(h * DKP, (h + 1) * DKP), slice(h * DVP, (h + 1) * DVP)) for h in range(GLA_HEADS)]

    def head_cols(x, h, width=GLA_DV_HEAD, padded=DVP):
        piece = x[:, h * width:(h + 1) * width]
        return jnp.concatenate([piece, jnp.zeros((c, padded - width), x.dtype)], axis=1)

    def decays(r0):
        g = la_ref[pl.ds(r0, c), :]
        cum = _dot(tril_ref[...], _split2(g))
        cum = cum[:, :GLA_QW] + cum[:, GLA_QW:]
        small = _dot(b_ref[...], g.astype(BF16))
        lev = [small[lv * c:(lv + 1) * c] for lv in range(GLA_MM_LEVELS)]
        for lv in range(GLA_MM_LEVELS, GLA_LEVELS):
            hs = 1 << lv
            parts = []
            for mid in range(hs, c, 2 * hs):
                parts += [cum[mid - 1:mid, :] - cum[mid - hs:mid, :], cum[mid:mid + hs, :] - cum[mid - 1:mid, :]]
            lev.append(jnp.concatenate(parts, axis=0))
        return cum, lev

    def intra(r0, cum, lev):
        qa = q_ref[pl.ds(r0, c), :]
        ka = k_ref[pl.ds(r0, c), :]
        v = v_ref[pl.ds(r0, c), :]
        per_head = []
        for h, (ks, vs) in enumerate(heads):
            qb, kb, vh = head_cols(qa, h, GLA_DK_HEAD, DKP), head_cols(ka, h, GLA_DK_HEAD, DKP), head_cols(v, h)
            cum_h = cum[:, ks]
            last_h = cum_h[c - 1:c, :]
            s = _dot_nt(qb, kb).astype(BF16) * m_ref[GLA_LEVELS]
            for lv in range(GLA_LEVELS):
                e = jnp.exp2(lev[lv][:, ks]).astype(BF16)
                s = s + _dot_nt(qb * e, kb * e).astype(BF16) * m_ref[lv]
            kd = (kb.astype(F32) * jnp.exp2(last_h - cum_h)).astype(BF16)
            qg = (qb.astype(F32) * jnp.exp2(cum_h)).astype(BF16)
            per_head.append((qg, s, _dot_tn(vh, kd), jnp.exp2(last_h)))
        return per_head

    def recur(r0, per_head):
        v = v_ref[pl.ds(r0, c), :]
        r = r_ref[pl.ds(r0, c), :]
        for h, (ks, vs) in enumerate(heads):
            qg, s, upd, last_decay = per_head[h]
            st = st_ref[h]
            o = _dot_nt(qg, st.astype(BF16)) + _dot(s, head_cols(v, h))
            st_ref[h] = st * last_decay + upd
            ms = jnp.sum(o * o, axis=-1, keepdims=True) * (1.0 / GLA_DV_HEAD)
            rh = head_cols(r, h).astype(F32)
            y = o * lax.rsqrt(ms + EPS) * gh * (rh * _sigmoid(rh))
            o_ref[pl.ds(r0, c), h * GLA_DV_HEAD:(h + 1) * GLA_DV_HEAD] = y[:, :GLA_DV_HEAD].astype(BF16)

    def group(gi, carry):
        offs = [pl.multiple_of((gi * GLA_UNROLL + u) * c, c) for u in range(GLA_UNROLL)]
        dec = [decays(r0) for r0 in offs]
        cur = [intra(r0, *d) for r0, d in zip(offs, dec)]
        for r0, per_head in zip(offs, cur):
            recur(r0, per_head)
        return carry

    lax.fori_loop(0, GLA_TS // (c * GLA_UNROLL), group, 0)


def _gla(q, k, la, v, r, gh, a, batch, seq):
    n = q.shape[0]
    nt = seq // GLA_TS
    tril, bmat, m = _gla_consts()
    row = lambda b, t: (b * nt + t, 0)
    fixed2 = lambda b, t: (0, 0)
    fixed3 = lambda b, t: (0, 0, 0)
    return pl.pallas_call(
        _gla_kernel,
        out_shape=jax.ShapeDtypeStruct((n, GLA_VIN), BF16),
        grid=(batch, nt),
        in_specs=[
            pl.BlockSpec((GLA_TS, GLA_QIN), row),
            pl.BlockSpec((GLA_TS, GLA_QIN), row),
            pl.BlockSpec((GLA_TS, GLA_QW), row),
            pl.BlockSpec((GLA_TS, GLA_VIN), row),
            pl.BlockSpec((GLA_TS, GLA_VIN), row),
            pl.BlockSpec((None, 1, DVP), lambda b, t: (a, 0, 0)),
            pl.BlockSpec(tril.shape, fixed2),
            pl.BlockSpec(bmat.shape, fixed2),
            pl.BlockSpec(m.shape, fixed3),
        ],
        out_specs=pl.BlockSpec((GLA_TS, GLA_VIN), row),
        scratch_shapes=[pltpu.VMEM((GLA_HEADS, DVP, DKP), F32)],
        compiler_params=_cparams(("arbitrary", "arbitrary")),
        name="gla",
    )(q, k, la, v, r, gh, tril, bmat, m)


MEM_BW = MEM_HEADS * MEM_LEN


def _mem_kv_kernel(mem_ref, g_ref, wkt_ref, wv_ref, kb_ref, vb_ref):
    d_row = lax.broadcasted_iota(jnp.int32, (MEM_W, MEM_LEN), 0) // MEM_DH
    d_col = lax.broadcasted_iota(jnp.int32, (MEM_LEN, MEM_W), 1) // MEM_DH
    for b in range(mem_ref.shape[0]):
        mn = _rmsnorm(mem_ref[b], g_ref[0]).astype(BF16)
        kt = _dot_nt(wkt_ref[0], mn) * (MEM_DH ** -0.5)
        v = _dot(mn, wv_ref[0])
        for hh in range(MEM_HEADS):
            kb_ref[0, b, :, hh * MEM_LEN:(hh + 1) * MEM_LEN] = jnp.where(d_row == hh, kt, 0.0).astype(BF16)
            vb_ref[0, b, hh * MEM_LEN:(hh + 1) * MEM_LEN, :] = jnp.where(d_col == hh, v, 0.0).astype(BF16)


def _mem_kv(mem, g, wkt, wv):
    b = mem.shape[0]
    return pl.pallas_call(
        _mem_kv_kernel,
        out_shape=(
            jax.ShapeDtypeStruct((DEPTH, b, MEM_W, MEM_BW), BF16),
            jax.ShapeDtypeStruct((DEPTH, b, MEM_BW, MEM_W), BF16),
        ),
        grid=(DEPTH,),
        in_specs=[
            pl.BlockSpec((b, MEM_LEN, D_MODEL), lambda l: (0, 0, 0)),
            pl.BlockSpec((1, 1, D_MODEL), lambda l: (l, 0, 0)),
            pl.BlockSpec((1, MEM_W, D_MODEL), lambda l: (l, 0, 0)),
            pl.BlockSpec((1, D_MODEL, MEM_W), lambda l: (l, 0, 0)),
        ],
        out_specs=(
            pl.BlockSpec((1, b, MEM_W, MEM_BW), lambda l: (l, 0, 0, 0)),
            pl.BlockSpec((1, b, MEM_BW, MEM_W), lambda l: (l, 0, 0, 0)),
        ),
        compiler_params=_cparams(("parallel",)),
        name="mem_kv",
    )(mem, g, wkt, wv)


def _mix_out_kernel(h_ref, om_ref, qm_ref, kb_ref, vb_ref, wo_ref, o_ref):
    lg = _dot(qm_ref[...], kb_ref[0, 0])
    ps = []
    for hh in range(MEM_HEADS):
        s = lg[:, hh * MEM_LEN:(hh + 1) * MEM_LEN]
        e = jnp.exp(s - jnp.max(s, axis=-1, keepdims=True))
        ps.append((e / jnp.sum(e, axis=-1, keepdims=True)).astype(BF16))
    o_mem = _dot(jnp.concatenate(ps, axis=1), vb_ref[0, 0])
    wm = om_ref.shape[1]
    y = _dot(om_ref[...], wo_ref[0:wm, :]) + _dot(o_mem.astype(BF16), wo_ref[wm:wm + MEM_W, :])
    o_ref[...] = h_ref[...] + y


def _mix_out(h, o_main, q_mem, kb, vb, wo, layer, seq):
    n = h.shape[0]
    wm = o_main.shape[1]
    per_b = seq // PROJ_TM
    row = lambda i: (i, 0)
    return pl.pallas_call(
        _mix_out_kernel,
        out_shape=jax.ShapeDtypeStruct((n, D_MODEL), F32),
        grid=(n // PROJ_TM,),
        in_specs=[
            pl.BlockSpec((PROJ_TM, D_MODEL), row),
            pl.BlockSpec((PROJ_TM, wm), row),
            pl.BlockSpec((PROJ_TM, MEM_W), row),
            pl.BlockSpec((1, 1, MEM_W, MEM_BW), lambda i: (layer, i // per_b, 0, 0)),
            pl.BlockSpec((1, 1, MEM_BW, MEM_W), lambda i: (layer, i // per_b, 0, 0)),
            pl.BlockSpec((None, wm + MEM_W, D_MODEL), lambda i: (layer, 0, 0)),
        ],
        out_specs=pl.BlockSpec((PROJ_TM, D_MODEL), row),
        compiler_params=_cparams(("parallel",)),
        name="mix_out",
    )(h, o_main, q_mem, kb, vb, wo)


SH_PW = MAIN_W + LANES


ONE_LANE = LANES - 1
AUX_SLOT = BF16_ROWS


def _fox_aux_consts():
    pqt = np.zeros((LANES, 3 * LANES), np.float32)
    pk = np.zeros((3 * LANES, LANES), np.float32)
    for h in range(FOX_HEADS):
        for p in range(3):
            pqt[h * AUX_SLOT + p, p * LANES + h] = 1.0
            pqt[h * AUX_SLOT + 3 + p, ONE_LANE] = 1.0
            pk[ONE_LANE, h * AUX_SLOT + p] = 1.0
            pk[p * LANES + h, h * AUX_SLOT + 3 + p] = -1.0
    tril = np.tril(np.ones((TM, TM), np.float32))
    return jnp.asarray(tril, dtype=BF16), jnp.asarray(pqt, dtype=BF16), jnp.asarray(pk, dtype=BF16)


def _shared_kernel(h_ref, g_ref, w_ref, wvt_ref, bf_ref, tril_ref, pqt_ref, pk_ref,
                   k_ref, vt_ref, aqt_ref, ak_ref, carry_ref):
    @pl.when(pl.program_id(1) == 0)
    def _():
        carry_ref[...] = jnp.zeros_like(carry_ref)

    xn = _rmsnorm(h_ref[...], g_ref[...]).astype(BF16)
    log_f = _log_sigmoid(_dot(xn, w_ref[:, MAIN_W:SH_PW]) + bf_ref[...])
    k_ref[...] = _dot(xn, w_ref[:, 0:MAIN_W]).astype(BF16)
    d = _exact_sum3(_dot(tril_ref[...], _split3(log_f)), LANES) + carry_ref[...]
    carry_ref[...] = d[TM - 1:TM, :]
    d3 = _split3(d * LOG2E)
    lane = lax.broadcasted_iota(jnp.int32, d3.shape, 1)
    d3 = jnp.where(lane == ONE_LANE, jnp.ones_like(d3), d3)
    vt_ref[0, 0] = _dot_nt(wvt_ref[...], xn).astype(BF16)
    aqt_ref[0, 0] = _dot_nt(pqt_ref[...], d3).astype(BF16)
    ak_ref[...] = _dot(d3, pk_ref[...]).astype(BF16)


def _shared_kv(h, g, w, wvt, bf, batch, seq):
    n = h.shape[0]
    nt = seq // TM
    tril, pqt, pk = _fox_aux_consts()
    row = lambda b, t: (b * nt + t, 0)
    fixed = lambda b, t: (0, 0)
    slab = lambda b, t: (b, t, 0, 0)
    return pl.pallas_call(
        _shared_kernel,
        out_shape=(
            jax.ShapeDtypeStruct((n, MAIN_W), BF16),
            jax.ShapeDtypeStruct((batch, nt, MAIN_W, TM), BF16),
            jax.ShapeDtypeStruct((batch, nt, LANES, TM), BF16),
            jax.ShapeDtypeStruct((n, LANES), BF16),
        ),
        grid=(batch, nt),
        in_specs=[
            pl.BlockSpec((TM, D_MODEL), row),
            pl.BlockSpec((1, D_MODEL), fixed),
            pl.BlockSpec((D_MODEL, SH_PW), fixed),
            pl.BlockSpec((MAIN_W, D_MODEL), fixed),
            pl.BlockSpec((1, LANES), fixed),
            pl.BlockSpec((TM, TM), fixed),
            pl.BlockSpec((LANES, 3 * LANES), fixed),
            pl.BlockSpec((3 * LANES, LANES), fixed),
        ],
        out_specs=(
            pl.BlockSpec((TM, MAIN_W), row),
            pl.BlockSpec((1, 1, MAIN_W, TM), slab),
            pl.BlockSpec((1, 1, LANES, TM), slab),
            pl.BlockSpec((TM, LANES), row),
        ),
        scratch_shapes=[pltpu.VMEM((1, LANES), F32)],
        compiler_params=_cparams(("arbitrary", "arbitrary")),
        name="shared_kv",
    )(h, g, w, wvt, bf, tril, pqt, pk)


def _fox_proj_kernel(h_ref, g_ref, wqt_ref, wm_ref, qt_ref, qm_ref):
    xn = _rmsnorm(h_ref[...], g_ref[...]).astype(BF16)
    qt = (_dot_nt(wqt_ref[...], xn) * (FOX_DH ** -0.5 * LOG2E)).astype(BF16)
    for s in range(PROJ_TM // TM):
        qt_ref[s] = qt[:, s * TM:(s + 1) * TM]
    qm_ref[...] = _dot(xn, wm_ref[...]).astype(BF16)


def _fox_proj(h, g, wqt, wm, layer, j):
    n = h.shape[0]
    row = lambda i: (i, 0)
    return pl.pallas_call(
        _fox_proj_kernel,
        out_shape=(jax.ShapeDtypeStruct((n // TM, MAIN_W, TM), BF16), jax.ShapeDtypeStruct((n, MEM_W), BF16)),
        grid=(n // PROJ_TM,),
        in_specs=[
            pl.BlockSpec((PROJ_TM, D_MODEL), row),
            pl.BlockSpec((None, 1, D_MODEL), lambda i: (layer, 0, 0)),
            pl.BlockSpec((None, MAIN_W, D_MODEL), lambda i: (j, 0, 0)),
            pl.BlockSpec((None, D_MODEL, MEM_W), lambda i: (j, 0, 0)),
        ],
        out_specs=(pl.BlockSpec((PROJ_TM // TM, MAIN_W, TM), lambda i: (i, 0, 0)),
                   pl.BlockSpec((PROJ_TM, MEM_W), row)),
        compiler_params=_cparams(("parallel",)),
        name="fox_proj",
    )(h, g, wqt, wm)


FOX_HPS = 3


FOX_SUB = MXU_N
FOX_NSUB = FOX_BK // FOX_SUB


def _fox_attn_kernel(qt_ref, aqt_ref, k_ref, ak_ref, vt_ref, o_ref):
    qi = pl.program_id(2)
    chains = [(hh, qs) for hh in range(FOX_HPS) for qs in range(FOX_BQ // FOX_SUB)]

    def head(hh):
        return slice(hh * FOX_DH, (hh + 1) * FOX_DH)

    def strip(s):
        return slice(s * FOX_SUB, (s + 1) * FOX_SUB)

    slot = lax.broadcasted_iota(jnp.int32, (LANES, FOX_SUB), 0) // AUX_SLOT
    first_head = pl.program_id(1) * FOX_HPS
    qq = [jnp.concatenate([qt_ref[0, 0, head(hh), strip(qs)],
                           jnp.where(slot == first_head + hh, aqt_ref[0, 0, :, strip(qs)], jnp.zeros((), BF16))],
                          axis=0)
          for hh, qs in chains]

    def scores(kj, diagonal):
        c0 = pl.multiple_of(kj * FOX_BK, FOX_BK)
        sts = {}
        for ks in range(FOX_NSUB):
            rows = pl.ds(c0 + ks * FOX_SUB, FOX_SUB)
            ak = ak_ref[0, rows, :]
            kk = [jnp.concatenate([k_ref[0, rows, head(hh)], ak], axis=1) for hh in range(FOX_HPS)]
            for ci, (hh, qs) in enumerate(chains):
                if diagonal and ks > qs:
                    continue
                st = _dot(kk[hh], qq[ci])
                if diagonal and ks == qs:
                    key = lax.broadcasted_iota(jnp.int32, st.shape, 0)
                    qry = lax.broadcasted_iota(jnp.int32, st.shape, 1)
                    st = jnp.where(key <= qry, st, -jnp.inf)
                sts[ks, ci] = st
        return sts

    def update(kj, carry, sts):
        carry = list(carry)
        for ks in range(FOX_NSUB):
            vt1 = [jnp.concatenate([vt_ref[0, kj, head(hh), strip(ks)], ones], axis=0) for hh in range(FOX_HPS)]
            for ci, (hh, qs) in enumerate(chains):
                if (ks, ci) not in sts:
                    continue
                m, acc = carry[ci]
                st = sts[ks, ci]
                m_new = jnp.maximum(m, jnp.max(st, axis=0, keepdims=True))
                alpha = jnp.exp2(m - m_new)
                p = jnp.exp2(st - m_new).astype(BF16)
                acc = alpha * acc + _dot(vt1[hh], p)
                carry[ci] = (m_new, acc)
        return tuple(carry)

    ones = jnp.ones((BF16_ROWS, FOX_SUB), BF16)
    init = tuple((jnp.full((1, FOX_SUB), -jnp.inf, F32), jnp.zeros((FOX_DH + BF16_ROWS, FOX_SUB), F32))
                 for _ in chains)

    def pair(i, carry):
        kj = 2 * i
        sts0, sts1 = scores(kj, False), scores(kj + 1, False)
        return update(kj + 1, update(kj, carry, sts0), sts1)

    def tail_with_leftover(carry):
        sts0, sts1 = scores(qi - 1, False), scores(qi, True)
        return update(qi, update(qi - 1, carry, sts0), sts1)

    carry = lax.fori_loop(0, qi // 2, pair, init)
    carry = lax.cond(qi % 2 == 1, tail_with_leftover, lambda cr: update(qi, cr, scores(qi, True)), carry)
    for ci, (hh, qs) in enumerate(chains):
        _, acc = carry[ci]
        o_ref[0, strip(qs), head(hh)] = (acc[:FOX_DH] / acc[FOX_DH:FOX_DH + 1]).T.astype(BF16)


def _fox_attn(qt, aqt, k, ak, vt, batch, seq):
    w = FOX_HPS * FOX_DH
    nq = seq // FOX_BQ
    qblk = pl.BlockSpec((1, 1, w, FOX_BQ), lambda b, g, i: (b, i, g, 0))
    kblk = pl.BlockSpec((1, seq, w), lambda b, g, i: (b, 0, g))
    vblk = pl.BlockSpec((1, seq // FOX_BK, w, FOX_BK), lambda b, g, i: (b, 0, g, 0))
    aqblk = pl.BlockSpec((1, 1, LANES, FOX_BQ), lambda b, g, i: (b, i, 0, 0))
    akblk = pl.BlockSpec((1, seq, LANES), lambda b, g, i: (b, 0, 0))
    out = pl.pallas_call(
        _fox_attn_kernel,
        out_shape=jax.ShapeDtypeStruct((batch, seq, MAIN_W), BF16),
        grid=(batch, FOX_HEADS // FOX_HPS, nq),
        in_specs=[qblk, aqblk, kblk, akblk, vblk],
        out_specs=pl.BlockSpec((1, FOX_BQ, w), lambda b, g, i: (b, i, g)),
        compiler_params=_cparams(("parallel", "parallel", "arbitrary")),
        name="fox_attn",
    )(qt.reshape(batch, nq, MAIN_W, FOX_BQ), aqt, k.reshape(batch, seq, MAIN_W), ak.reshape(batch, seq, LANES), vt)
    return out.reshape(batch * seq, MAIN_W)


def _pad_heads_cols(w, heads, width, padded):
    lead = w.shape[:-1]
    w = w.reshape(lead + (heads, width))
    w = jnp.pad(w, [(0, 0)] * len(lead) + [(0, 0), (0, padded - width)])
    return w.reshape(lead + (heads * padded,))


def _gla_weights(w_in, w_gate_up, b_gate, g_head):
    dk = GLA_HEADS * GLA_DK_HEAD
    dv = GLA_HEADS * GLA_DV_HEAD
    o1, o2, o3, o4, o5 = dk, 2 * dk, 2 * dk + dv, 2 * dk + 2 * dv, 2 * dk + 2 * dv + GLA_RANK
    w = jnp.concatenate([
        w_in[..., :o4],
        w_in[..., o5:],
        jnp.pad(w_in[..., o4:o5], [(0, 0), (0, 0), (0, LANES - GLA_RANK)]),
    ], axis=-1).astype(BF16)
    wup = jnp.pad(_pad_heads_cols(w_gate_up, GLA_HEADS, GLA_DK_HEAD, DKP),
                  [(0, 0), (0, LANES - GLA_RANK), (0, 0)]).astype(BF16)
    bg = _pad_heads_cols(b_gate[:, None, :], GLA_HEADS, GLA_DK_HEAD, DKP)
    gh = jnp.pad(g_head[:, None, :], [(0, 0), (0, 0), (0, DVP - GLA_DV_HEAD)])
    return w, wup, bg, gh


def kernel(x, mem, norm_ffn, w_ffn_in, w_ffn_out, norm_mix, norm_mem, w_mem_kv, w_out, w_in_a, w_gate_up, b_gate,
           norm_gla_head, w_in_b, norm_shared, w_kv_shared, w_fgate, b_fgate, norm_final):
    batch, seq, _ = x.shape
    h = x.reshape(batch * seq, D_MODEL)

    w1 = w_ffn_in.astype(BF16)
    w2 = w_ffn_out.astype(BF16)
    g_ffn = norm_ffn[:, :, None, :]
    g_mix = norm_mix[:, None, :]
    wkt = jnp.swapaxes(w_mem_kv[:, :, :MEM_W], 1, 2).astype(BF16)
    wv = w_mem_kv[:, :, MEM_W:].astype(BF16)
    kb, vb = _mem_kv(mem, norm_mem[:, None, :], wkt, wv)
    w_a, wup, bg, gh = _gla_weights(w_in_a, w_gate_up, b_gate, norm_gla_head)
    wqt_b = jnp.swapaxes(w_in_b[:, :, :MAIN_W], 1, 2).astype(BF16)
    wm_b = w_in_b[:, :, MAIN_W:].astype(BF16)
    wo = w_out.astype(BF16)

    k_sh = v_sh = aq_sh = ak_sh = None
    for l in range(DEPTH):
        if l == N_A:
            w_sh = jnp.concatenate([w_kv_shared[:, :MAIN_W], jnp.pad(w_fgate, [(0, 0), (0, LANES - FOX_HEADS)])],
                                   axis=1).astype(BF16)
            wvt = w_kv_shared[:, MAIN_W:].T.astype(BF16)
            bf = jnp.pad(b_fgate[None, :], [(0, 0), (0, LANES - FOX_HEADS)])
            k_sh, v_sh, aq_sh, ak_sh = _shared_kv(h, norm_shared[None, :], w_sh, wvt, bf, batch, seq)
        h = _ffn(h, g_ffn, w1, w2, l, 0)
        if l < N_A:
            q, k, v, r, la, q_mem = _gla_proj(h, g_mix, w_a, wup, bg, l, l)
            o_main = _gla(q, k, la, v, r, gh, l, batch, seq)
            h = _mix_out(h, o_main, q_mem, kb, vb, wo, l, seq)
        else:
            qt, q_mem = _fox_proj(h, g_mix, wqt_b, wm_b, l, l - N_A)
            o_main = _fox_attn(qt, aq_sh, k_sh, ak_sh, v_sh, batch, seq)
            h = _mix_out(h, o_main, q_mem, kb, vb, wo, l, seq)
        h = _ffn(h, g_ffn, w1, w2, l, 1, norm_final[None, :] if l == DEPTH - 1 else None)
    return h.reshape(batch, seq, D_MODEL)
```

```python
import numpy as np
import jax
import jax.numpy as jnp
from jax import lax
from jax.experimental import pallas as pl
from jax.experimental.pallas import tpu as pltpu

F32 = jnp.float32
BF16 = jnp.bfloat16

D_MODEL = 1024
DEPTH = 4
N_A = DEPTH // 2
MAIN_W = 768
MEM_W = 256
GLA_HEADS = 4
GLA_DK_HEAD = 96
GLA_DV_HEAD = 192
GLA_RANK = 16
GLA_TEMP = 16.0
FOX_HEADS = 6
FOX_DH = 128
MEM_HEADS = 4
MEM_DH = 64
MEM_LEN = 256
D_FF = 2816
EPS = 1e-6
LOG2E = 1.4426950408889634

LANES = 128
MXU_N = 256
BF16_ROWS = 16
VMEM_LIMIT = 56 * 1024 * 1024

DKP = LANES
DVP = MXU_N
GLA_QW = GLA_HEADS * DKP
GLA_CHUNK = MXU_N
GLA_LEVELS = 8
GLA_MM_LEVELS = 3

TM = 512
PROJ_TM = 1024
FOX_BQ = TM
FOX_BK = TM


def _cparams(sem):
    return pltpu.CompilerParams(dimension_semantics=sem, vmem_limit_bytes=VMEM_LIMIT)


def _rmsnorm(x, g):
    return x * lax.rsqrt(jnp.mean(x * x, axis=-1, keepdims=True) + EPS) * g


def _log_sigmoid(x):
    return jnp.minimum(x, 0.0) - jnp.log(1.0 + jnp.exp(-jnp.abs(x)))


def _sigmoid(x):
    return 1.0 / (1.0 + jnp.exp(-x))


def _split3(x):
    hi = x.astype(BF16)
    r1 = x - hi.astype(F32)
    mid = r1.astype(BF16)
    lo = (r1 - mid.astype(F32)).astype(BF16)
    return jnp.concatenate([hi, mid, lo], axis=1)


def _split2(x):
    hi = x.astype(BF16)
    lo = (x - hi.astype(F32)).astype(BF16)
    return jnp.concatenate([hi, lo], axis=1)


def _dot(a, b):
    return jnp.dot(a, b, preferred_element_type=F32)


def _dot_nt(a, b):
    return lax.dot_general(a, b, (((1,), (1,)), ((), ())), preferred_element_type=F32)


def _dot_tn(a, b):
    return lax.dot_general(a, b, (((0,), (0,)), ((), ())), preferred_element_type=F32)


def _exact_sum3(r, w):
    return r[:, :w] + r[:, w:2 * w] + r[:, 2 * w:]


def _memory_heads(qm, kb, vb):
    lg = _dot(qm, kb)
    ps = []
    for hh in range(MEM_HEADS):
        s = lg[:, hh * MEM_LEN:(hh + 1) * MEM_LEN]
        e = jnp.exp(s - jnp.max(s, axis=-1, keepdims=True))
        ps.append((e / jnp.sum(e, axis=-1, keepdims=True)).astype(BF16))
    return _dot(jnp.concatenate(ps, axis=1), vb).astype(BF16)


FFN_BOUNDS = (0, 6 * MXU_N, D_FF)
FFN_TM = 1024


def _ffn_kernel(h_ref, g_ref, w1_ref, w2_ref, *rest):
    o_ref = rest[-1]
    h = h_ref[...]
    xn = _rmsnorm(h, g_ref[0, 0]).astype(BF16)
    y = None
    for f0, f1 in zip(FFN_BOUNDS[:-1], FFN_BOUNDS[1:]):
        a = _dot(xn, w1_ref[0, 0, :, f0:f1])
        c = _dot(xn, w1_ref[0, 0, :, D_FF + f0:D_FF + f1])
        act = (a * _sigmoid(a) * c).astype(BF16)
        part = _dot(act, w2_ref[0, 0, f0:f1, :])
        y = part if y is None else y + part
    out = h + 0.5 * y
    o_ref[...] = _rmsnorm(out, rest[0][...]) if len(rest) == 2 else out


def _ffn(h, g_all, w1_all, w2_all, layer, half, g_final=None):
    n = h.shape[0]
    pick = lambda i: (layer, half, 0, 0)
    in_specs = [
        pl.BlockSpec((FFN_TM, D_MODEL), lambda i: (i, 0)),
        pl.BlockSpec((1, 1, 1, D_MODEL), pick),
        pl.BlockSpec((1, 1, D_MODEL, 2 * D_FF), pick, pipeline_mode=pl.Buffered(1)),
        pl.BlockSpec((1, 1, D_FF, D_MODEL), pick, pipeline_mode=pl.Buffered(1)),
    ]
    args = [h, g_all, w1_all, w2_all]
    if g_final is not None:
        in_specs.append(pl.BlockSpec((1, D_MODEL), lambda i: (0, 0)))
        args.append(g_final)
    return pl.pallas_call(
        _ffn_kernel,
        out_shape=jax.ShapeDtypeStruct((n, D_MODEL), F32),
        grid=(n // FFN_TM,),
        in_specs=in_specs,
        out_specs=pl.BlockSpec((FFN_TM, D_MODEL), lambda i: (i, 0)),
        compiler_params=_cparams(("parallel",)),
        name="ffn",
    )(*args)


GLA_QIN = GLA_HEADS * GLA_DK_HEAD
GLA_VIN = GLA_HEADS * GLA_DV_HEAD
GLA_PW = 2 * GLA_QIN + 2 * GLA_VIN + MEM_W + LANES


def _gla_proj_kernel(h_ref, g_ref, w_ref, wup_ref, bg_ref, q_ref, k_ref, v_ref, r_ref, la_ref, qm_ref):
    xn = _rmsnorm(h_ref[...], g_ref[...]).astype(BF16)
    oq, ov = 0, 2 * GLA_QIN
    orr, om, og = ov + GLA_VIN, ov + 2 * GLA_VIN, ov + 2 * GLA_VIN + MEM_W
    g_low = _dot(xn, w_ref[:, og:og + LANES]).astype(BF16)
    qk = _dot(xn, w_ref[:, oq:ov])
    x = _dot(g_low, wup_ref[...]) + bg_ref[...]
    q_ref[...] = (qk[:, :GLA_QIN] * (GLA_DK_HEAD ** -0.5)).astype(BF16)
    k_ref[...] = qk[:, GLA_QIN:].astype(BF16)
    la_ref[...] = _log_sigmoid(x) * (LOG2E / GLA_TEMP)
    v_ref[...] = _dot(xn, w_ref[:, ov:ov + GLA_VIN]).astype(BF16)
    r_ref[...] = _dot(xn, w_ref[:, orr:orr + GLA_VIN]).astype(BF16)
    qm_ref[...] = _dot(xn, w_ref[:, om:om + MEM_W]).astype(BF16)


def _gla_proj(h, g, w, wup, bg, layer, a):
    n = h.shape[0]
    row = lambda i: (i, 0)
    return pl.pallas_call(
        _gla_proj_kernel,
        out_shape=(
            jax.ShapeDtypeStruct((n, GLA_QIN), BF16),
            jax.ShapeDtypeStruct((n, GLA_QIN), BF16),
            jax.ShapeDtypeStruct((n, GLA_VIN), BF16),
            jax.ShapeDtypeStruct((n, GLA_VIN), BF16),
            jax.ShapeDtypeStruct((n, GLA_QW), F32),
            jax.ShapeDtypeStruct((n, MEM_W), BF16),
        ),
        grid=(n // PROJ_TM,),
        in_specs=[
            pl.BlockSpec((PROJ_TM, D_MODEL), row),
            pl.BlockSpec((None, 1, D_MODEL), lambda i: (layer, 0, 0)),
            pl.BlockSpec((None, D_MODEL, GLA_PW), lambda i: (a, 0, 0), pipeline_mode=pl.Buffered(1)),
            pl.BlockSpec((None, LANES, GLA_QW), lambda i: (a, 0, 0)),
            pl.BlockSpec((None, 1, GLA_QW), lambda i: (a, 0, 0)),
        ],
        out_specs=(
            pl.BlockSpec((PROJ_TM, GLA_QIN), row),
            pl.BlockSpec((PROJ_TM, GLA_QIN), row),
            pl.BlockSpec((PROJ_TM, GLA_VIN), row),
            pl.BlockSpec((PROJ_TM, GLA_VIN), row),
            pl.BlockSpec((PROJ_TM, GLA_QW), row),
            pl.BlockSpec((PROJ_TM, MEM_W), row),
        ),
        compiler_params=_cparams(("parallel",)),
        name="gla_proj",
    )(h, g, w, wup, bg)


GLA_TS = 1024
GLA_UNROLL = 2


def _gla_consts():
    c = GLA_CHUNK
    i = np.arange(c)[:, None]
    j = np.arange(c)[None, :]
    tril = (j <= i)
    b_rows, masks = [], []
    for lv in range(GLA_LEVELS):
        hs = 1 << lv
        mid = (i // (2 * hs)) * (2 * hs) + hs
        lower = (i % (2 * hs)) >= hs
        b_rows.append(np.where(lower, (j >= mid) & (j <= i), (j > i) & (j < mid)))
        masks.append(((i // (2 * hs)) == (j // (2 * hs))) & lower & ((j % (2 * hs)) < hs))
    masks.append(i == j)
    asb = lambda x: jnp.asarray(x.astype(np.float32), dtype=BF16)
    return asb(tril), asb(np.concatenate(b_rows[:GLA_MM_LEVELS], axis=0)), asb(np.stack(masks, axis=0))


def _gla_kernel(q_ref, k_ref, la_ref, v_ref, r_ref, gh_ref, tril_ref, b_ref, m_ref,
                h_ref, qm_ref, kb_ref, vb_ref, wo_ref, o_ref, st_ref):
    c = GLA_CHUNK

    @pl.when(pl.program_id(1) == 0)
    def _():
        st_ref[...] = jnp.zeros_like(st_ref)

    gh = gh_ref[...]

    heads = [(slice(h * DKP, (h + 1) * DKP), slice(h * DVP, (h + 1) * DVP)) for h in range(GLA_HEADS)]

    def head_cols(x, h, width=GLA_DV_HEAD, padded=DVP):
        piece = x[:, h * width:(h + 1) * width]
        return jnp.concatenate([piece, jnp.zeros((c, padded - width), x.dtype)], axis=1)

    def decays(r0):
        g = la_ref[pl.ds(r0, c), :]
        cum = _dot(tril_ref[...], _split2(g))
        cum = cum[:, :GLA_QW] + cum[:, GLA_QW:]
        small = _dot(b_ref[...], g.astype(BF16))
        lev = [small[lv * c:(lv + 1) * c] for lv in range(GLA_MM_LEVELS)]
        for lv in range(GLA_MM_LEVELS, GLA_LEVELS):
            hs = 1 << lv
            parts = []
            for mid in range(hs, c, 2 * hs):
                parts += [cum[mid - 1:mid, :] - cum[mid - hs:mid, :], cum[mid:mid + hs, :] - cum[mid - 1:mid, :]]
            lev.append(jnp.concatenate(parts, axis=0))
        return cum, lev

    def intra(r0, cum, lev):
        qa = q_ref[pl.ds(r0, c), :]
        ka = k_ref[pl.ds(r0, c), :]
        v = v_ref[pl.ds(r0, c), :]
        per_head = []
        for h, (ks, vs) in enumerate(heads):
            qb, kb, vh = head_cols(qa, h, GLA_DK_HEAD, DKP), head_cols(ka, h, GLA_DK_HEAD, DKP), head_cols(v, h)
            cum_h = cum[:, ks]
            last_h = cum_h[c - 1:c, :]
            s = _dot_nt(qb, kb).astype(BF16) * m_ref[GLA_LEVELS]
            for lv in range(GLA_LEVELS):
                e = jnp.exp2(lev[lv][:, ks]).astype(BF16)
                s = s + _dot_nt(qb * e, kb * e).astype(BF16) * m_ref[lv]
            kd = (kb.astype(F32) * jnp.exp2(last_h - cum_h)).astype(BF16)
            qg = (qb.astype(F32) * jnp.exp2(cum_h)).astype(BF16)
            per_head.append((qg, s, _dot_tn(vh, kd), jnp.exp2(last_h)))
        return per_head

    def recur(r0, per_head):
        v = v_ref[pl.ds(r0, c), :]
        r = r_ref[pl.ds(r0, c), :]
        ys = []
        for h, (ks, vs) in enumerate(heads):
            qg, s, upd, last_decay = per_head[h]
            st = st_ref[h]
            o = _dot_nt(qg, st.astype(BF16)) + _dot(s, head_cols(v, h))
            st_ref[h] = st * last_decay + upd
            ms = jnp.sum(o * o, axis=-1, keepdims=True) * (1.0 / GLA_DV_HEAD)
            rh = head_cols(r, h).astype(F32)
            y = o * lax.rsqrt(ms + EPS) * gh * (rh * _sigmoid(rh))
            ys.append(y[:, :GLA_DV_HEAD].astype(BF16))
        return jnp.concatenate(ys, axis=1)

    def group(gi, carry):
        offs = [pl.multiple_of((gi * GLA_UNROLL + u) * c, c) for u in range(GLA_UNROLL)]
        dec = [decays(r0) for r0 in offs]
        cur = [intra(r0, *d) for r0, d in zip(offs, dec)]
        mem = [_memory_heads(qm_ref[pl.ds(r0, c), :], kb_ref[0, 0], vb_ref[0, 0]) for r0 in offs]
        main = [recur(r0, per_head) for r0, per_head in zip(offs, cur)]
        for r0, o_main, o_mem in zip(offs, main, mem):
            mixed = _dot(o_main, wo_ref[0:MAIN_W, :]) + _dot(o_mem, wo_ref[MAIN_W:MAIN_W + MEM_W, :])
            o_ref[pl.ds(r0, c), :] = h_ref[pl.ds(r0, c), :] + mixed
        return carry

    lax.fori_loop(0, GLA_TS // (c * GLA_UNROLL), group, 0)


def _gla(q, k, la, v, r, gh, h, q_mem, kb, vb, wo, layer, a, batch, seq):
    n = q.shape[0]
    nt = seq // GLA_TS
    tril, bmat, m = _gla_consts()
    row = lambda b, t: (b * nt + t, 0)
    fixed2 = lambda b, t: (0, 0)
    fixed3 = lambda b, t: (0, 0, 0)
    mem = lambda b, t: (layer, b, 0, 0)
    return pl.pallas_call(
        _gla_kernel,
        out_shape=jax.ShapeDtypeStruct((n, D_MODEL), F32),
        grid=(batch, nt),
        in_specs=[
            pl.BlockSpec((GLA_TS, GLA_QIN), row),
            pl.BlockSpec((GLA_TS, GLA_QIN), row),
            pl.BlockSpec((GLA_TS, GLA_QW), row),
            pl.BlockSpec((GLA_TS, GLA_VIN), row),
            pl.BlockSpec((GLA_TS, GLA_VIN), row),
            pl.BlockSpec((None, 1, DVP), lambda b, t: (a, 0, 0)),
            pl.BlockSpec(tril.shape, fixed2),
            pl.BlockSpec(bmat.shape, fixed2),
            pl.BlockSpec(m.shape, fixed3),
            pl.BlockSpec((GLA_TS, D_MODEL), row),
            pl.BlockSpec((GLA_TS, MEM_W), row),
            pl.BlockSpec((1, 1, MEM_W, MEM_BW), mem),
            pl.BlockSpec((1, 1, MEM_BW, MEM_W), mem),
            pl.BlockSpec((None, MAIN_W + MEM_W, D_MODEL), lambda b, t: (layer, 0, 0), pipeline_mode=pl.Buffered(1)),
        ],
        out_specs=pl.BlockSpec((GLA_TS, D_MODEL), row),
        scratch_shapes=[pltpu.VMEM((GLA_HEADS, DVP, DKP), F32)],
        compiler_params=_cparams(("arbitrary", "arbitrary")),
        name="gla",
    )(q, k, la, v, r, gh, tril, bmat, m, h, q_mem, kb, vb, wo)


MEM_BW = MEM_HEADS * MEM_LEN


def _mem_kv_kernel(mem_ref, g_ref, wkt_ref, wv_ref, kb_ref, vb_ref):
    d_row = lax.broadcasted_iota(jnp.int32, (MEM_W, MEM_LEN), 0) // MEM_DH
    d_col = lax.broadcasted_iota(jnp.int32, (MEM_LEN, MEM_W), 1) // MEM_DH
    for b in range(mem_ref.shape[0]):
        mn = _rmsnorm(mem_ref[b], g_ref[0]).astype(BF16)
        kt = _dot_nt(wkt_ref[0], mn) * (MEM_DH ** -0.5)
        v = _dot(mn, wv_ref[0])
        for hh in range(MEM_HEADS):
            kb_ref[0, b, :, hh * MEM_LEN:(hh + 1) * MEM_LEN] = jnp.where(d_row == hh, kt, 0.0).astype(BF16)
            vb_ref[0, b, hh * MEM_LEN:(hh + 1) * MEM_LEN, :] = jnp.where(d_col == hh, v, 0.0).astype(BF16)


def _mem_kv(mem, g, wkt, wv):
    b = mem.shape[0]
    return pl.pallas_call(
        _mem_kv_kernel,
        out_shape=(
            jax.ShapeDtypeStruct((DEPTH, b, MEM_W, MEM_BW), BF16),
            jax.ShapeDtypeStruct((DEPTH, b, MEM_BW, MEM_W), BF16),
        ),
        grid=(DEPTH,),
        in_specs=[
            pl.BlockSpec((b, MEM_LEN, D_MODEL), lambda l: (0, 0, 0)),
            pl.BlockSpec((1, 1, D_MODEL), lambda l: (l, 0, 0)),
            pl.BlockSpec((1, MEM_W, D_MODEL), lambda l: (l, 0, 0)),
            pl.BlockSpec((1, D_MODEL, MEM_W), lambda l: (l, 0, 0)),
        ],
        out_specs=(
            pl.BlockSpec((1, b, MEM_W, MEM_BW), lambda l: (l, 0, 0, 0)),
            pl.BlockSpec((1, b, MEM_BW, MEM_W), lambda l: (l, 0, 0, 0)),
        ),
        compiler_params=_cparams(("parallel",)),
        name="mem_kv",
    )(mem, g, wkt, wv)


def _mix_out_kernel(h_ref, om_ref, qm_ref, kb_ref, vb_ref, wo_ref, o_ref):
    o_mem = _memory_heads(qm_ref[...], kb_ref[0, 0], vb_ref[0, 0])
    wm = om_ref.shape[1]
    y = _dot(om_ref[...], wo_ref[0:wm, :]) + _dot(o_mem, wo_ref[wm:wm + MEM_W, :])
    o_ref[...] = h_ref[...] + y


def _mix_out(h, o_main, q_mem, kb, vb, wo, layer, seq):
    n = h.shape[0]
    wm = o_main.shape[1]
    per_b = seq // PROJ_TM
    row = lambda i: (i, 0)
    return pl.pallas_call(
        _mix_out_kernel,
        out_shape=jax.ShapeDtypeStruct((n, D_MODEL), F32),
        grid=(n // PROJ_TM,),
        in_specs=[
            pl.BlockSpec((PROJ_TM, D_MODEL), row),
            pl.BlockSpec((PROJ_TM, wm), row),
            pl.BlockSpec((PROJ_TM, MEM_W), row),
            pl.BlockSpec((1, 1, MEM_W, MEM_BW), lambda i: (layer, i // per_b, 0, 0)),
            pl.BlockSpec((1, 1, MEM_BW, MEM_W), lambda i: (layer, i // per_b, 0, 0)),
            pl.BlockSpec((None, wm + MEM_W, D_MODEL), lambda i: (layer, 0, 0)),
        ],
        out_specs=pl.BlockSpec((PROJ_TM, D_MODEL), row),
        compiler_params=_cparams(("parallel",)),
        name="mix_out",
    )(h, o_main, q_mem, kb, vb, wo)


SH_PW = MAIN_W + LANES


ONE_LANE = LANES - 1
AUX_SLOT = BF16_ROWS


def _fox_aux_consts():
    pqt = np.zeros((LANES, 3 * LANES), np.float32)
    pk = np.zeros((3 * LANES, LANES), np.float32)
    for h in range(FOX_HEADS):
        for p in range(3):
            pqt[h * AUX_SLOT + p, p * LANES + h] = 1.0
            pqt[h * AUX_SLOT + 3 + p, ONE_LANE] = 1.0
            pk[ONE_LANE, h * AUX_SLOT + p] = 1.0
            pk[p * LANES + h, h * AUX_SLOT + 3 + p] = -1.0
    tril = np.tril(np.ones((TM, TM), np.float32))
    return jnp.asarray(tril, dtype=BF16), jnp.asarray(pqt, dtype=BF16), jnp.asarray(pk, dtype=BF16)


def _shared_kernel(h_ref, g_ref, w_ref, wvt_ref, bf_ref, tril_ref, pqt_ref, pk_ref,
                   k_ref, vt_ref, aqt_ref, ak_ref, carry_ref):
    @pl.when(pl.program_id(1) == 0)
    def _():
        carry_ref[...] = jnp.zeros_like(carry_ref)

    xn = _rmsnorm(h_ref[...], g_ref[...]).astype(BF16)
    log_f = _log_sigmoid(_dot(xn, w_ref[:, MAIN_W:SH_PW]) + bf_ref[...])
    k_ref[...] = _dot(xn, w_ref[:, 0:MAIN_W]).astype(BF16)
    d = _exact_sum3(_dot(tril_ref[...], _split3(log_f)), LANES) + carry_ref[...]
    carry_ref[...] = d[TM - 1:TM, :]
    d3 = _split3(d * LOG2E)
    lane = lax.broadcasted_iota(jnp.int32, d3.shape, 1)
    d3 = jnp.where(lane == ONE_LANE, jnp.ones_like(d3), d3)
    vt_ref[0, 0] = _dot_nt(wvt_ref[...], xn).astype(BF16)
    aqt_ref[0, 0] = _dot_nt(pqt_ref[...], d3).astype(BF16)
    ak_ref[...] = _dot(d3, pk_ref[...]).astype(BF16)


def _shared_kv(h, g, w, wvt, bf, batch, seq):
    n = h.shape[0]
    nt = seq // TM
    tril, pqt, pk = _fox_aux_consts()
    row = lambda b, t: (b * nt + t, 0)
    fixed = lambda b, t: (0, 0)
    slab = lambda b, t: (b, t, 0, 0)
    return pl.pallas_call(
        _shared_kernel,
        out_shape=(
            jax.ShapeDtypeStruct((n, MAIN_W), BF16),
            jax.ShapeDtypeStruct((batch, nt, MAIN_W, TM), BF16),
            jax.ShapeDtypeStruct((batch, nt, LANES, TM), BF16),
            jax.ShapeDtypeStruct((n, LANES), BF16),
        ),
        grid=(batch, nt),
        in_specs=[
            pl.BlockSpec((TM, D_MODEL), row),
            pl.BlockSpec((1, D_MODEL), fixed),
            pl.BlockSpec((D_MODEL, SH_PW), fixed),
            pl.BlockSpec((MAIN_W, D_MODEL), fixed),
            pl.BlockSpec((1, LANES), fixed),
            pl.BlockSpec((TM, TM), fixed),
            pl.BlockSpec((LANES, 3 * LANES), fixed),
            pl.BlockSpec((3 * LANES, LANES), fixed),
        ],
        out_specs=(
            pl.BlockSpec((TM, MAIN_W), row),
            pl.BlockSpec((1, 1, MAIN_W, TM), slab),
            pl.BlockSpec((1, 1, LANES, TM), slab),
            pl.BlockSpec((TM, LANES), row),
        ),
        scratch_shapes=[pltpu.VMEM((1, LANES), F32)],
        compiler_params=_cparams(("arbitrary", "arbitrary")),
        name="shared_kv",
    )(h, g, w, wvt, bf, tril, pqt, pk)


def _fox_proj_kernel(h_ref, g_ref, wqt_ref, wm_ref, qt_ref, qm_ref):
    xn = _rmsnorm(h_ref[...], g_ref[...]).astype(BF16)
    qt = (_dot_nt(wqt_ref[...], xn) * (FOX_DH ** -0.5 * LOG2E)).astype(BF16)
    for s in range(PROJ_TM // TM):
        qt_ref[s] = qt[:, s * TM:(s + 1) * TM]
    qm_ref[...] = _dot(xn, wm_ref[...]).astype(BF16)


def _fox_proj(h, g, wqt, wm, layer, j):
    n = h.shape[0]
    row = lambda i: (i, 0)
    return pl.pallas_call(
        _fox_proj_kernel,
        out_shape=(jax.ShapeDtypeStruct((n // TM, MAIN_W, TM), BF16), jax.ShapeDtypeStruct((n, MEM_W), BF16)),
        grid=(n // PROJ_TM,),
        in_specs=[
            pl.BlockSpec((PROJ_TM, D_MODEL), row),
            pl.BlockSpec((None, 1, D_MODEL), lambda i: (layer, 0, 0)),
            pl.BlockSpec((None, MAIN_W, D_MODEL), lambda i: (j, 0, 0)),
            pl.BlockSpec((None, D_MODEL, MEM_W), lambda i: (j, 0, 0)),
        ],
        out_specs=(pl.BlockSpec((PROJ_TM // TM, MAIN_W, TM), lambda i: (i, 0, 0)),
                   pl.BlockSpec((PROJ_TM, MEM_W), row)),
        compiler_params=_cparams(("parallel",)),
        name="fox_proj",
    )(h, g, wqt, wm)


FOX_HPS = 3


FOX_SUB = MXU_N
FOX_NSUB = FOX_BK // FOX_SUB


def _fox_attn_kernel(qt_ref, aqt_ref, k_ref, ak_ref, vt_ref, o_ref):
    qi = pl.program_id(2)
    chains = [(hh, qs) for hh in range(FOX_HPS) for qs in range(FOX_BQ // FOX_SUB)]

    def head(hh):
        return slice(hh * FOX_DH, (hh + 1) * FOX_DH)

    def strip(s):
        return slice(s * FOX_SUB, (s + 1) * FOX_SUB)

    slot = lax.broadcasted_iota(jnp.int32, (LANES, FOX_SUB), 0) // AUX_SLOT
    first_head = pl.program_id(1) * FOX_HPS
    qq = [jnp.concatenate([qt_ref[0, 0, head(hh), strip(qs)],
                           jnp.where(slot == first_head + hh, aqt_ref[0, 0, :, strip(qs)], jnp.zeros((), BF16))],
                          axis=0)
          for hh, qs in chains]

    def scores(kj, diagonal):
        c0 = pl.multiple_of(kj * FOX_BK, FOX_BK)
        sts = {}
        for ks in range(FOX_NSUB):
            rows = pl.ds(c0 + ks * FOX_SUB, FOX_SUB)
            ak = ak_ref[0, rows, :]
            kk = [jnp.concatenate([k_ref[0, rows, head(hh)], ak], axis=1) for hh in range(FOX_HPS)]
            for ci, (hh, qs) in enumerate(chains):
                if diagonal and ks > qs:
                    continue
                st = _dot(kk[hh], qq[ci])
                if diagonal and ks == qs:
                    key = lax.broadcasted_iota(jnp.int32, st.shape, 0)
                    qry = lax.broadcasted_iota(jnp.int32, st.shape, 1)
                    st = jnp.where(key <= qry, st, -jnp.inf)
                sts[ks, ci] = st
        return sts

    def update(kj, carry, sts):
        carry = list(carry)
        for ks in range(FOX_NSUB):
            vt1 = [jnp.concatenate([vt_ref[0, kj, head(hh), strip(ks)], ones], axis=0) for hh in range(FOX_HPS)]
            for ci, (hh, qs) in enumerate(chains):
                if (ks, ci) not in sts:
                    continue
                m, acc = carry[ci]
                st = sts[ks, ci]
                m_new = jnp.maximum(m, jnp.max(st, axis=0, keepdims=True))
                alpha = jnp.exp2(m - m_new)
                p = jnp.exp2(st - m_new).astype(BF16)
                acc = alpha * acc + _dot(vt1[hh], p)
                carry[ci] = (m_new, acc)
        return tuple(carry)

    ones = jnp.ones((BF16_ROWS, FOX_SUB), BF16)
    init = tuple((jnp.full((1, FOX_SUB), -jnp.inf, F32), jnp.zeros((FOX_DH + BF16_ROWS, FOX_SUB), F32))
                 for _ in chains)

    def pair(i, carry):
        kj = 2 * i
        sts0, sts1 = scores(kj, False), scores(kj + 1, False)
        return update(kj + 1, update(kj, carry, sts0), sts1)

    def tail_with_leftover(carry):
        sts0, sts1 = scores(qi - 1, False), scores(qi, True)
        return update(qi, update(qi - 1, carry, sts0), sts1)

    carry = lax.fori_loop(0, qi // 2, pair, init)
    carry = lax.cond(qi % 2 == 1, tail_with_leftover, lambda cr: update(qi, cr, scores(qi, True)), carry)
    for ci, (hh, qs) in enumerate(chains):
        _, acc = carry[ci]
        o_ref[0, strip(qs), head(hh)] = (acc[:FOX_DH] / acc[FOX_DH:FOX_DH + 1]).T.astype(BF16)


def _fox_attn(qt, aqt, k, ak, vt, batch, seq):
    w = FOX_HPS * FOX_DH
    nq = seq // FOX_BQ
    qblk = pl.BlockSpec((1, 1, w, FOX_BQ), lambda b, g, i: (b, i, g, 0))
    kblk = pl.BlockSpec((1, seq, w), lambda b, g, i: (b, 0, g))
    vblk = pl.BlockSpec((1, seq // FOX_BK, w, FOX_BK), lambda b, g, i: (b, 0, g, 0))
    aqblk = pl.BlockSpec((1, 1, LANES, FOX_BQ), lambda b, g, i: (b, i, 0, 0))
    akblk = pl.BlockSpec((1, seq, LANES), lambda b, g, i: (b, 0, 0))
    out = pl.pallas_call(
        _fox_attn_kernel,
        out_shape=jax.ShapeDtypeStruct((batch, seq, MAIN_W), BF16),
        grid=(batch, FOX_HEADS // FOX_HPS, nq),
        in_specs=[qblk, aqblk, kblk, akblk, vblk],
        out_specs=pl.BlockSpec((1, FOX_BQ, w), lambda b, g, i: (b, i, g)),
        compiler_params=_cparams(("parallel", "parallel", "arbitrary")),
        name="fox_attn",
    )(qt.reshape(batch, nq, MAIN_W, FOX_BQ), aqt, k.reshape(batch, seq, MAIN_W), ak.reshape(batch, seq, LANES), vt)
    return out.reshape(batch * seq, MAIN_W)


def _pad_heads_cols(w, heads, width, padded):
    lead = w.shape[:-1]
    w = w.reshape(lead + (heads, width))
    w = jnp.pad(w, [(0, 0)] * len(lead) + [(0, 0), (0, padded - width)])
    return w.reshape(lead + (heads * padded,))


def _gla_weights(w_in, w_gate_up, b_gate, g_head):
    dk = GLA_HEADS * GLA_DK_HEAD
    dv = GLA_HEADS * GLA_DV_HEAD
    o1, o2, o3, o4, o5 = dk, 2 * dk, 2 * dk + dv, 2 * dk + 2 * dv, 2 * dk + 2 * dv + GLA_RANK
    w = jnp.concatenate([
        w_in[..., :o4],
        w_in[..., o5:],
        jnp.pad(w_in[..., o4:o5], [(0, 0), (0, 0), (0, LANES - GLA_RANK)]),
    ], axis=-1).astype(BF16)
    wup = jnp.pad(_pad_heads_cols(w_gate_up, GLA_HEADS, GLA_DK_HEAD, DKP),
                  [(0, 0), (0, LANES - GLA_RANK), (0, 0)]).astype(BF16)
    bg = _pad_heads_cols(b_gate[:, None, :], GLA_HEADS, GLA_DK_HEAD, DKP)
    gh = jnp.pad(g_head[:, None, :], [(0, 0), (0, 0), (0, DVP - GLA_DV_HEAD)])
    return w, wup, bg, gh


def kernel(x, mem, norm_ffn, w_ffn_in, w_ffn_out, norm_mix, norm_mem, w_mem_kv, w_out, w_in_a, w_gate_up, b_gate,
           norm_gla_head, w_in_b, norm_shared, w_kv_shared, w_fgate, b_fgate, norm_final):
    batch, seq, _ = x.shape
    h = x.reshape(batch * seq, D_MODEL)

    w1 = w_ffn_in.astype(BF16)
    w2 = w_ffn_out.astype(BF16)
    g_ffn = norm_ffn[:, :, None, :]
    g_mix = norm_mix[:, None, :]
    wkt = jnp.swapaxes(w_mem_kv[:, :, :MEM_W], 1, 2).astype(BF16)
    wv = w_mem_kv[:, :, MEM_W:].astype(BF16)
    kb, vb = _mem_kv(mem, norm_mem[:, None, :], wkt, wv)
    w_a, wup, bg, gh = _gla_weights(w_in_a, w_gate_up, b_gate, norm_gla_head)
    wqt_b = jnp.swapaxes(w_in_b[:, :, :MAIN_W], 1, 2).astype(BF16)
    wm_b = w_in_b[:, :, MAIN_W:].astype(BF16)
    wo = w_out.astype(BF16)

    k_sh = v_sh = aq_sh = ak_sh = None
    for l in range(DEPTH):
        if l == N_A:
            w_sh = jnp.concatenate([w_kv_shared[:, :MAIN_W], jnp.pad(w_fgate, [(0, 0), (0, LANES - FOX_HEADS)])],
                                   axis=1).astype(BF16)
            wvt = w_kv_shared[:, MAIN_W:].T.astype(BF16)
            bf = jnp.pad(b_fgate[None, :], [(0, 0), (0, LANES - FOX_HEADS)])
            k_sh, v_sh, aq_sh, ak_sh = _shared_kv(h, norm_shared[None, :], w_sh, wvt, bf, batch, seq)
        h = _ffn(h, g_ffn, w1, w2, l, 0)
        if l < N_A:
            q, k, v, r, la, q_mem = _gla_proj(h, g_mix, w_a, wup, bg, l, l)
            h = _gla(q, k, la, v, r, gh, h, q_mem, kb, vb, wo, l, l, batch, seq)
        else:
            qt, q_mem = _fox_proj(h, g_mix, wqt_b, wm_b, l, l - N_A)
            o_main = _fox_attn(qt, aq_sh, k_sh, ak_sh, v_sh, batch, seq)
            h = _mix_out(h, o_main, q_mem, kb, vb, wo, l, seq)
        h = _ffn(h, g_ffn, w1, w2, l, 1, norm_final[None, :] if l == DEPTH - 1 else None)
    return h.reshape(batch, seq, D_MODEL)
```

```python
import numpy as np
import jax
import jax.numpy as jnp
from jax import lax
from jax.experimental import pallas as pl
from jax.experimental.pallas import tpu as pltpu

F32 = jnp.float32
BF16 = jnp.bfloat16

D_MODEL = 1024
DEPTH = 4
N_A = DEPTH // 2
MAIN_W = 768
MEM_W = 256
GLA_HEADS = 4
GLA_DK_HEAD = 96
GLA_DV_HEAD = 192
GLA_RANK = 16
GLA_TEMP = 16.0
FOX_HEADS = 6
FOX_DH = 128
MEM_HEADS = 4
MEM_DH = 64
MEM_LEN = 256
D_FF = 2816
EPS = 1e-6
LOG2E = 1.4426950408889634

LANES = 128
MXU_N = 256
BF16_ROWS = 16
VMEM_LIMIT = 56 * 1024 * 1024

DKP = LANES
DVP = MXU_N
GLA_QW = GLA_HEADS * DKP
GLA_CHUNK = MXU_N
GLA_LEVELS = 8
GLA_MM_LEVELS = 3

TM = 512
PROJ_TM = 1024
FOX_BQ = TM
FOX_BK = TM


def _cparams(sem):
    return pltpu.CompilerParams(dimension_semantics=sem, vmem_limit_bytes=VMEM_LIMIT)


def _rmsnorm(x, g):
    return x * lax.rsqrt(jnp.mean(x * x, axis=-1, keepdims=True) + EPS) * g


def _log_sigmoid(x):
    return jnp.minimum(x, 0.0) - jnp.log(1.0 + jnp.exp(-jnp.abs(x)))


def _sigmoid(x):
    return 1.0 / (1.0 + jnp.exp(-x))


def _split3(x):
    hi = x.astype(BF16)
    r1 = x - hi.astype(F32)
    mid = r1.astype(BF16)
    lo = (r1 - mid.astype(F32)).astype(BF16)
    return jnp.concatenate([hi, mid, lo], axis=1)


def _split2(x):
    hi = x.astype(BF16)
    lo = (x - hi.astype(F32)).astype(BF16)
    return jnp.concatenate([hi, lo], axis=1)


def _dot(a, b):
    return jnp.dot(a, b, preferred_element_type=F32)


def _dot_nt(a, b):
    return lax.dot_general(a, b, (((1,), (1,)), ((), ())), preferred_element_type=F32)


def _dot_tn(a, b):
    return lax.dot_general(a, b, (((0,), (0,)), ((), ())), preferred_element_type=F32)


def _exact_sum3(r, w):
    return r[:, :w] + r[:, w:2 * w] + r[:, 2 * w:]


def _memory_heads(qm, kb, vb):
    lg = _dot(qm, kb)
    ps = []
    for hh in range(MEM_HEADS):
        s = lg[:, hh * MEM_LEN:(hh + 1) * MEM_LEN]
        e = jnp.exp(s - jnp.max(s, axis=-1, keepdims=True))
        ps.append((e / jnp.sum(e, axis=-1, keepdims=True)).astype(BF16))
    return _dot(jnp.concatenate(ps, axis=1), vb).astype(BF16)


FFN_BOUNDS = (0, 6 * MXU_N, D_FF)
FFN_TM = 1024


def _ffn_kernel(h_ref, g_ref, w1_ref, w2_ref, *rest):
    o_ref = rest[-1]
    h = h_ref[...]
    xn = _rmsnorm(h, g_ref[0, 0]).astype(BF16)
    y = None
    for f0, f1 in zip(FFN_BOUNDS[:-1], FFN_BOUNDS[1:]):
        a = _dot(xn, w1_ref[0, 0, :, f0:f1])
        c = _dot(xn, w1_ref[0, 0, :, D_FF + f0:D_FF + f1])
        act = (a * _sigmoid(a) * c).astype(BF16)
        part = _dot(act, w2_ref[0, 0, f0:f1, :])
        y = part if y is None else y + part
    out = h + 0.5 * y
    o_ref[...] = _rmsnorm(out, rest[0][...]) if len(rest) == 2 else out


def _ffn(h, g_all, w1_all, w2_all, layer, half, g_final=None):
    n = h.shape[0]
    pick = lambda i: (layer, half, 0, 0)
    in_specs = [
        pl.BlockSpec((FFN_TM, D_MODEL), lambda i: (i, 0)),
        pl.BlockSpec((1, 1, 1, D_MODEL), pick),
        pl.BlockSpec((1, 1, D_MODEL, 2 * D_FF), pick, pipeline_mode=pl.Buffered(1)),
        pl.BlockSpec((1, 1, D_FF, D_MODEL), pick, pipeline_mode=pl.Buffered(1)),
    ]
    args = [h, g_all, w1_all, w2_all]
    if g_final is not None:
        in_specs.append(pl.BlockSpec((1, D_MODEL), lambda i: (0, 0)))
        args.append(g_final)
    return pl.pallas_call(
        _ffn_kernel,
        out_shape=jax.ShapeDtypeStruct((n, D_MODEL), F32),
        grid=(n // FFN_TM,),
        in_specs=in_specs,
        out_specs=pl.BlockSpec((FFN_TM, D_MODEL), lambda i: (i, 0)),
        compiler_params=_cparams(("parallel",)),
        name="ffn",
    )(*args)


GLA_QIN = GLA_HEADS * GLA_DK_HEAD
GLA_VIN = GLA_HEADS * GLA_DV_HEAD
GLA_PW = 2 * GLA_QIN + 2 * GLA_VIN + MEM_W + LANES


def _gla_proj_kernel(h_ref, g_ref, w_ref, wup_ref, bg_ref, q_ref, k_ref, v_ref, r_ref, la_ref, qm_ref):
    xn = _rmsnorm(h_ref[...], g_ref[...]).astype(BF16)
    oq, ov = 0, 2 * GLA_QIN
    orr, om, og = ov + GLA_VIN, ov + 2 * GLA_VIN, ov + 2 * GLA_VIN + MEM_W
    g_low = _dot(xn, w_ref[:, og:og + LANES]).astype(BF16)
    qk = _dot(xn, w_ref[:, oq:ov])
    x = _dot(g_low, wup_ref[...]) + bg_ref[...]
    q_ref[...] = (qk[:, :GLA_QIN] * (GLA_DK_HEAD ** -0.5)).astype(BF16)
    k_ref[...] = qk[:, GLA_QIN:].astype(BF16)
    la_ref[...] = _log_sigmoid(x) * (LOG2E / GLA_TEMP)
    v_ref[...] = _dot(xn, w_ref[:, ov:ov + GLA_VIN]).astype(BF16)
    r_ref[...] = _dot(xn, w_ref[:, orr:orr + GLA_VIN]).astype(BF16)
    qm_ref[...] = _dot(xn, w_ref[:, om:om + MEM_W]).astype(BF16)


def _gla_proj(h, g, w, wup, bg, layer, a):
    n = h.shape[0]
    row = lambda i: (i, 0)
    return pl.pallas_call(
        _gla_proj_kernel,
        out_shape=(
            jax.ShapeDtypeStruct((n, GLA_QIN), BF16),
            jax.ShapeDtypeStruct((n, GLA_QIN), BF16),
            jax.ShapeDtypeStruct((n, GLA_VIN), BF16),
            jax.ShapeDtypeStruct((n, GLA_VIN), BF16),
            jax.ShapeDtypeStruct((n, GLA_QW), F32),
            jax.ShapeDtypeStruct((n, MEM_W), BF16),
        ),
        grid=(n // PROJ_TM,),
        in_specs=[
            pl.BlockSpec((PROJ_TM, D_MODEL), row),
            pl.BlockSpec((None, 1, D_MODEL), lambda i: (layer, 0, 0)),
            pl.BlockSpec((None, D_MODEL, GLA_PW), lambda i: (a, 0, 0), pipeline_mode=pl.Buffered(1)),
            pl.BlockSpec((None, LANES, GLA_QW), lambda i: (a, 0, 0)),
            pl.BlockSpec((None, 1, GLA_QW), lambda i: (a, 0, 0)),
        ],
        out_specs=(
            pl.BlockSpec((PROJ_TM, GLA_QIN), row),
            pl.BlockSpec((PROJ_TM, GLA_QIN), row),
            pl.BlockSpec((PROJ_TM, GLA_VIN), row),
            pl.BlockSpec((PROJ_TM, GLA_VIN), row),
            pl.BlockSpec((PROJ_TM, GLA_QW), row),
            pl.BlockSpec((PROJ_TM, MEM_W), row),
        ),
        compiler_params=_cparams(("parallel",)),
        name="gla_proj",
    )(h, g, w, wup, bg)


GLA_TS = 1024
GLA_UNROLL = 2


def _gla_consts():
    c = GLA_CHUNK
    i = np.arange(c)[:, None]
    j = np.arange(c)[None, :]
    tril = (j <= i)
    b_rows, masks = [], []
    for lv in range(GLA_LEVELS):
        hs = 1 << lv
        mid = (i // (2 * hs)) * (2 * hs) + hs
        lower = (i % (2 * hs)) >= hs
        b_rows.append(np.where(lower, (j >= mid) & (j <= i), (j > i) & (j < mid)))
        masks.append(((i // (2 * hs)) == (j // (2 * hs))) & lower & ((j % (2 * hs)) < hs))
    masks.append(i == j)
    asb = lambda x: jnp.asarray(x.astype(np.float32), dtype=BF16)
    return asb(tril), asb(np.concatenate(b_rows[:GLA_MM_LEVELS], axis=0)), asb(np.stack(masks, axis=0))


def _gla_kernel(q_ref, k_ref, la_ref, v_ref, r_ref, gh_ref, tril_ref, b_ref, m_ref,
                h_ref, qm_ref, kb_ref, vb_ref, wo_ref, o_ref, st_ref):
    c = GLA_CHUNK

    @pl.when(pl.program_id(1) == 0)
    def _():
        st_ref[...] = jnp.zeros_like(st_ref)

    gh = gh_ref[...]

    heads = [(slice(h * DKP, (h + 1) * DKP), slice(h * DVP, (h + 1) * DVP)) for h in range(GLA_HEADS)]

    def head_cols(x, h, width=GLA_DV_HEAD, padded=DVP):
        piece = x[:, h * width:(h + 1) * width]
        return jnp.concatenate([piece, jnp.zeros((c, padded - width), x.dtype)], axis=1)

    def decays(r0):
        g = la_ref[pl.ds(r0, c), :]
        cum = _dot(tril_ref[...], _split2(g))
        cum = cum[:, :GLA_QW] + cum[:, GLA_QW:]
        small = _dot(b_ref[...], g.astype(BF16))
        lev = [small[lv * c:(lv + 1) * c] for lv in range(GLA_MM_LEVELS)]
        for lv in range(GLA_MM_LEVELS, GLA_LEVELS):
            hs = 1 << lv
            parts = []
            for mid in range(hs, c, 2 * hs):
                parts += [cum[mid - 1:mid, :] - cum[mid - hs:mid, :], cum[mid:mid + hs, :] - cum[mid - 1:mid, :]]
            lev.append(jnp.concatenate(parts, axis=0))
        return cum, lev

    def intra(r0, cum, lev):
        qa = q_ref[pl.ds(r0, c), :]
        ka = k_ref[pl.ds(r0, c), :]
        v = v_ref[pl.ds(r0, c), :]
        per_head = []
        for h, (ks, vs) in enumerate(heads):
            qb, kb, vh = head_cols(qa, h, GLA_DK_HEAD, DKP), head_cols(ka, h, GLA_DK_HEAD, DKP), head_cols(v, h)
            cum_h = cum[:, ks]
            last_h = cum_h[c - 1:c, :]
            s = _dot_nt(qb, kb).astype(BF16) * m_ref[GLA_LEVELS]
            for lv in range(GLA_LEVELS):
                e = jnp.exp2(lev[lv][:, ks]).astype(BF16)
                s = s + _dot_nt(qb * e, kb * e).astype(BF16) * m_ref[lv]
            kd = (kb.astype(F32) * jnp.exp2(last_h - cum_h)).astype(BF16)
            qg = (qb.astype(F32) * jnp.exp2(cum_h)).astype(BF16)
            per_head.append((qg, s, _dot_tn(vh, kd), jnp.exp2(last_h)))
        return per_head

    def recur(r0, per_head):
        v = v_ref[pl.ds(r0, c), :]
        r = r_ref[pl.ds(r0, c), :]
        ys = []
        for h, (ks, vs) in enumerate(heads):
            qg, s, upd, last_decay = per_head[h]
            st = st_ref[h]
            o = _dot_nt(qg, st.astype(BF16)) + _dot(s, head_cols(v, h))
            st_ref[h] = st * last_decay + upd
            ms = jnp.sum(o * o, axis=-1, keepdims=True) * (1.0 / GLA_DV_HEAD)
            rh = head_cols(r, h).astype(F32)
            y = o * lax.rsqrt(ms + EPS) * gh * (rh * _sigmoid(rh))
            ys.append(y[:, :GLA_DV_HEAD].astype(BF16))
        return jnp.concatenate(ys, axis=1)

    def group(gi, carry):
        offs = [pl.multiple_of((gi * GLA_UNROLL + u) * c, c) for u in range(GLA_UNROLL)]
        dec = [decays(r0) for r0 in offs]
        cur = [intra(r0, *d) for r0, d in zip(offs, dec)]
        mem = [_memory_heads(qm_ref[pl.ds(r0, c), :], kb_ref[0, 0], vb_ref[0, 0]) for r0 in offs]
        main = [recur(r0, per_head) for r0, per_head in zip(offs, cur)]
        for r0, o_main, o_mem in zip(offs, main, mem):
            mixed = _dot(o_main, wo_ref[0:MAIN_W, :]) + _dot(o_mem, wo_ref[MAIN_W:MAIN_W + MEM_W, :])
            o_ref[pl.ds(r0, c), :] = h_ref[pl.ds(r0, c), :] + mixed
        return carry

    lax.fori_loop(0, GLA_TS // (c * GLA_UNROLL), group, 0)


def _gla(q, k, la, v, r, gh, h, q_mem, kb, vb, wo, layer, a, batch, seq):
    n = q.shape[0]
    nt = seq // GLA_TS
    tril, bmat, m = _gla_consts()
    row = lambda b, t: (b * nt + t, 0)
    fixed2 = lambda b, t: (0, 0)
    fixed3 = lambda b, t: (0, 0, 0)
    mem = lambda b, t: (layer, b, 0, 0)
    return pl.pallas_call(
        _gla_kernel,
        out_shape=jax.ShapeDtypeStruct((n, D_MODEL), F32),
        grid=(batch, nt),
        in_specs=[
            pl.BlockSpec((GLA_TS, GLA_QIN), row),
            pl.BlockSpec((GLA_TS, GLA_QIN), row),
            pl.BlockSpec((GLA_TS, GLA_QW), row),
            pl.BlockSpec((GLA_TS, GLA_VIN), row),
            pl.BlockSpec((GLA_TS, GLA_VIN), row),
            pl.BlockSpec((None, 1, DVP), lambda b, t: (a, 0, 0)),
            pl.BlockSpec(tril.shape, fixed2),
            pl.BlockSpec(bmat.shape, fixed2),
            pl.BlockSpec(m.shape, fixed3),
            pl.BlockSpec((GLA_TS, D_MODEL), row),
            pl.BlockSpec((GLA_TS, MEM_W), row),
            pl.BlockSpec((1, 1, MEM_W, MEM_BW), mem),
            pl.BlockSpec((1, 1, MEM_BW, MEM_W), mem),
            pl.BlockSpec((None, MAIN_W + MEM_W, D_MODEL), lambda b, t: (layer, 0, 0), pipeline_mode=pl.Buffered(1)),
        ],
        out_specs=pl.BlockSpec((GLA_TS, D_MODEL), row),
        scratch_shapes=[pltpu.VMEM((GLA_HEADS, DVP, DKP), F32)],
        compiler_params=_cparams(("arbitrary", "arbitrary")),
        name="gla",
    )(q, k, la, v, r, gh, tril, bmat, m, h, q_mem, kb, vb, wo)


MEM_BW = MEM_HEADS * MEM_LEN


def _mem_kv_kernel(mem_ref, g_ref, wkt_ref, wv_ref, kb_ref, vb_ref):
    d_row = lax.broadcasted_iota(jnp.int32, (MEM_W, MEM_LEN), 0) // MEM_DH
    d_col = lax.broadcasted_iota(jnp.int32, (MEM_LEN, MEM_W), 1) // MEM_DH
    for b in range(mem_ref.shape[0]):
        mn = _rmsnorm(mem_ref[b], g_ref[0]).astype(BF16)
        kt = _dot_nt(wkt_ref[0], mn) * (MEM_DH ** -0.5)
        v = _dot(mn, wv_ref[0])
        for hh in range(MEM_HEADS):
            kb_ref[0, b, :, hh * MEM_LEN:(hh + 1) * MEM_LEN] = jnp.where(d_row == hh, kt, 0.0).astype(BF16)
            vb_ref[0, b, hh * MEM_LEN:(hh + 1) * MEM_LEN, :] = jnp.where(d_col == hh, v, 0.0).astype(BF16)


def _mem_kv(mem, g, wkt, wv):
    b = mem.shape[0]
    return pl.pallas_call(
        _mem_kv_kernel,
        out_shape=(
            jax.ShapeDtypeStruct((DEPTH, b, MEM_W, MEM_BW), BF16),
            jax.ShapeDtypeStruct((DEPTH, b, MEM_BW, MEM_W), BF16),
        ),
        grid=(DEPTH,),
        in_specs=[
            pl.BlockSpec((b, MEM_LEN, D_MODEL), lambda l: (0, 0, 0)),
            pl.BlockSpec((1, 1, D_MODEL), lambda l: (l, 0, 0)),
            pl.BlockSpec((1, MEM_W, D_MODEL), lambda l: (l, 0, 0)),
            pl.BlockSpec((1, D_MODEL, MEM_W), lambda l: (l, 0, 0)),
        ],
        out_specs=(
            pl.BlockSpec((1, b, MEM_W, MEM_BW), lambda l: (l, 0, 0, 0)),
            pl.BlockSpec((1, b, MEM_BW, MEM_W), lambda l: (l, 0, 0, 0)),
        ),
        compiler_params=_cparams(("parallel",)),
        name="mem_kv",
    )(mem, g, wkt, wv)


def _mix_out_kernel(h_ref, om_ref, qm_ref, kb_ref, vb_ref, wo_ref, o_ref):
    o_mem = _memory_heads(qm_ref[...], kb_ref[0, 0], vb_ref[0, 0])
    wm = om_ref.shape[1]
    y = _dot(om_ref[...], wo_ref[0:wm, :]) + _dot(o_mem, wo_ref[wm:wm + MEM_W, :])
    o_ref[...] = h_ref[...] + y


def _mix_out(h, o_main, q_mem, kb, vb, wo, layer, seq):
    n = h.shape[0]
    wm = o_main.shape[1]
    per_b = seq // PROJ_TM
    row = lambda i: (i, 0)
    return pl.pallas_call(
        _mix_out_kernel,
        out_shape=jax.ShapeDtypeStruct((n, D_MODEL), F32),
        grid=(n // PROJ_TM,),
        in_specs=[
            pl.BlockSpec((PROJ_TM, D_MODEL), row),
            pl.BlockSpec((PROJ_TM, wm), row),
            pl.BlockSpec((PROJ_TM, MEM_W), row),
            pl.BlockSpec((1, 1, MEM_W, MEM_BW), lambda i: (layer, i // per_b, 0, 0)),
            pl.BlockSpec((1, 1, MEM_BW, MEM_W), lambda i: (layer, i // per_b, 0, 0)),
            pl.BlockSpec((None, wm + MEM_W, D_MODEL), lambda i: (layer, 0, 0)),
        ],
        out_specs=pl.BlockSpec((PROJ_TM, D_MODEL), row),
        compiler_params=_cparams(("parallel",)),
        name="mix_out",
    )(h, o_main, q_mem, kb, vb, wo)


SH_PW = MAIN_W + LANES


ONE_LANE = LANES - 1
AUX_SLOT = BF16_ROWS


def _fox_aux_consts():
    pqt = np.zeros((LANES, 3 * LANES), np.float32)
    pk = np.zeros((3 * LANES, LANES), np.float32)
    for h in range(FOX_HEADS):
        for p in range(3):
            pqt[h * AUX_SLOT + p, p * LANES + h] = 1.0
            pqt[h * AUX_SLOT + 3 + p, ONE_LANE] = 1.0
            pk[ONE_LANE, h * AUX_SLOT + p] = 1.0
            pk[p * LANES + h, h * AUX_SLOT + 3 + p] = -1.0
    tril = np.tril(np.ones((TM, TM), np.float32))
    return jnp.asarray(tril, dtype=BF16), jnp.asarray(pqt, dtype=BF16), jnp.asarray(pk, dtype=BF16)


def _shared_kernel(h_ref, g_ref, w_ref, wvt_ref, bf_ref, tril_ref, pqt_ref, pk_ref,
                   k_ref, vt_ref, aqt_ref, ak_ref, carry_ref):
    @pl.when(pl.program_id(1) == 0)
    def _():
        carry_ref[...] = jnp.zeros_like(carry_ref)

    xn = _rmsnorm(h_ref[...], g_ref[...]).astype(BF16)
    log_f = _log_sigmoid(_dot(xn, w_ref[:, MAIN_W:SH_PW]) + bf_ref[...])
    k_ref[...] = _dot(xn, w_ref[:, 0:MAIN_W]).astype(BF16)
    d = _exact_sum3(_dot(tril_ref[...], _split3(log_f)), LANES) + carry_ref[...]
    carry_ref[...] = d[TM - 1:TM, :]
    d3 = _split3(d * LOG2E)
    lane = lax.broadcasted_iota(jnp.int32, d3.shape, 1)
    d3 = jnp.where(lane == ONE_LANE, jnp.ones_like(d3), d3)
    vt_ref[0, 0] = _dot_nt(wvt_ref[...], xn).astype(BF16)
    aqt_ref[0, 0] = _dot_nt(pqt_ref[...], d3).astype(BF16)
    ak_ref[...] = _dot(d3, pk_ref[...]).astype(BF16)


def _shared_kv(h, g, w, wvt, bf, batch, seq):
    n = h.shape[0]
    nt = seq // TM
    tril, pqt, pk = _fox_aux_consts()
    row = lambda b, t: (b * nt + t, 0)
    fixed = lambda b, t: (0, 0)
    slab = lambda b, t: (b, t, 0, 0)
    return pl.pallas_call(
        _shared_kernel,
        out_shape=(
            jax.ShapeDtypeStruct((n, MAIN_W), BF16),
            jax.ShapeDtypeStruct((batch, nt, MAIN_W, TM), BF16),
            jax.ShapeDtypeStruct((batch, nt, LANES, TM), BF16),
            jax.ShapeDtypeStruct((n, LANES), BF16),
        ),
        grid=(batch, nt),
        in_specs=[
            pl.BlockSpec((TM, D_MODEL), row),
            pl.BlockSpec((1, D_MODEL), fixed),
            pl.BlockSpec((D_MODEL, SH_PW), fixed),
            pl.BlockSpec((MAIN_W, D_MODEL), fixed),
            pl.BlockSpec((1, LANES), fixed),
            pl.BlockSpec((TM, TM), fixed),
            pl.BlockSpec((LANES, 3 * LANES), fixed),
            pl.BlockSpec((3 * LANES, LANES), fixed),
        ],
        out_specs=(
            pl.BlockSpec((TM, MAIN_W), row),
            pl.BlockSpec((1, 1, MAIN_W, TM), slab),
            pl.BlockSpec((1, 1, LANES, TM), slab),
            pl.BlockSpec((TM, LANES), row),
        ),
        scratch_shapes=[pltpu.VMEM((1, LANES), F32)],
        compiler_params=_cparams(("arbitrary", "arbitrary")),
        name="shared_kv",
    )(h, g, w, wvt, bf, tril, pqt, pk)


def _fox_proj_kernel(h_ref, g_ref, wqt_ref, wm_ref, qt_ref, qm_ref):
    xn = _rmsnorm(h_ref[...], g_ref[...]).astype(BF16)
    qt = (_dot_nt(wqt_ref[...], xn) * (FOX_DH ** -0.5 * LOG2E)).astype(BF16)
    for s in range(PROJ_TM // TM):
        qt_ref[s] = qt[:, s * TM:(s + 1) * TM]
    qm_ref[...] = _dot(xn, wm_ref[...]).astype(BF16)


def _fox_proj(h, g, wqt, wm, layer, j):
    n = h.shape[0]
    row = lambda i: (i, 0)
    return pl.pallas_call(
        _fox_proj_kernel,
        out_shape=(jax.ShapeDtypeStruct((n // TM, MAIN_W, TM), BF16), jax.ShapeDtypeStruct((n, MEM_W), BF16)),
        grid=(n // PROJ_TM,),
        in_specs=[
            pl.BlockSpec((PROJ_TM, D_MODEL), row),
            pl.BlockSpec((None, 1, D_MODEL), lambda i: (layer, 0, 0)),
            pl.BlockSpec((None, MAIN_W, D_MODEL), lambda i: (j, 0, 0)),
            pl.BlockSpec((None, D_MODEL, MEM_W), lambda i: (j, 0, 0)),
        ],
        out_specs=(pl.BlockSpec((PROJ_TM // TM, MAIN_W, TM), lambda i: (i, 0, 0)),
                   pl.BlockSpec((PROJ_TM, MEM_W), row)),
        compiler_params=_cparams(("parallel",)),
        name="fox_proj",
    )(h, g, wqt, wm)


FOX_HPS = 6


FOX_SUB = MXU_N
FOX_NSUB = FOX_BK // FOX_SUB


def _fox_attn_kernel(qt_ref, aqt_ref, k_ref, ak_ref, vt_ref, o_ref):
    qi = pl.program_id(2)
    chains = [(hh, qs) for hh in range(FOX_HPS) for qs in range(FOX_BQ // FOX_SUB)]

    def head(hh):
        return slice(hh * FOX_DH, (hh + 1) * FOX_DH)

    def strip(s):
        return slice(s * FOX_SUB, (s + 1) * FOX_SUB)

    slot = lax.broadcasted_iota(jnp.int32, (LANES, FOX_SUB), 0) // AUX_SLOT
    first_head = pl.program_id(1) * FOX_HPS
    qq = [jnp.concatenate([qt_ref[0, 0, head(hh), strip(qs)],
                           jnp.where(slot == first_head + hh, aqt_ref[0, 0, :, strip(qs)], jnp.zeros((), BF16))],
                          axis=0)
          for hh, qs in chains]

    def scores(kj, diagonal):
        c0 = pl.multiple_of(kj * FOX_BK, FOX_BK)
        sts = {}
        for ks in range(FOX_NSUB):
            rows = pl.ds(c0 + ks * FOX_SUB, FOX_SUB)
            ak = ak_ref[0, rows, :]
            kk = [jnp.concatenate([k_ref[0, rows, head(hh)], ak], axis=1) for hh in range(FOX_HPS)]
            for ci, (hh, qs) in enumerate(chains):
                if diagonal and ks > qs:
                    continue
                st = _dot(kk[hh], qq[ci])
                if diagonal and ks == qs:
                    key = lax.broadcasted_iota(jnp.int32, st.shape, 0)
                    qry = lax.broadcasted_iota(jnp.int32, st.shape, 1)
                    st = jnp.where(key <= qry, st, -jnp.inf)
                sts[ks, ci] = st
        return sts

    def update(kj, carry, sts):
        carry = list(carry)
        for ks in range(FOX_NSUB):
            vt1 = [jnp.concatenate([vt_ref[0, kj, head(hh), strip(ks)], ones], axis=0) for hh in range(FOX_HPS)]
            for ci, (hh, qs) in enumerate(chains):
                if (ks, ci) not in sts:
                    continue
                m, acc = carry[ci]
                st = sts[ks, ci]
                m_new = jnp.maximum(m, jnp.max(st, axis=0, keepdims=True))
                alpha = jnp.exp2(m - m_new)
                p = jnp.exp2(st - m_new).astype(BF16)
                acc = alpha * acc + _dot(vt1[hh], p)
                carry[ci] = (m_new, acc)
        return tuple(carry)

    ones = jnp.ones((BF16_ROWS, FOX_SUB), BF16)
    init = tuple((jnp.full((1, FOX_SUB), -jnp.inf, F32), jnp.zeros((FOX_DH + BF16_ROWS, FOX_SUB), F32))
                 for _ in chains)

    def pair(i, carry):
        kj = 2 * i
        sts0, sts1 = scores(kj, False), scores(kj + 1, False)
        return update(kj + 1, update(kj, carry, sts0), sts1)

    def tail_with_leftover(carry):
        sts0, sts1 = scores(qi - 1, False), scores(qi, True)
        return update(qi, update(qi - 1, carry, sts0), sts1)

    carry = lax.fori_loop(0, qi // 2, pair, init)
    carry = lax.cond(qi % 2 == 1, tail_with_leftover, lambda cr: update(qi, cr, scores(qi, True)), carry)
    for ci, (hh, qs) in enumerate(chains):
        _, acc = carry[ci]
        o_ref[0, strip(qs), head(hh)] = (acc[:FOX_DH] / acc[FOX_DH:FOX_DH + 1]).T.astype(BF16)


def _fox_attn(qt, aqt, k, ak, vt, batch, seq):
    w = FOX_HPS * FOX_DH
    nq = seq // FOX_BQ
    qblk = pl.BlockSpec((1, 1, w, FOX_BQ), lambda b, g, i: (b, i, g, 0))
    kblk = pl.BlockSpec((1, seq, w), lambda b, g, i: (b, 0, g))
    vblk = pl.BlockSpec((1, seq // FOX_BK, w, FOX_BK), lambda b, g, i: (b, 0, g, 0))
    aqblk = pl.BlockSpec((1, 1, LANES, FOX_BQ), lambda b, g, i: (b, i, 0, 0))
    akblk = pl.BlockSpec((1, seq, LANES), lambda b, g, i: (b, 0, 0))
    out = pl.pallas_call(
        _fox_attn_kernel,
        out_shape=jax.ShapeDtypeStruct((batch, seq, MAIN_W), BF16),
        grid=(batch, FOX_HEADS // FOX_HPS, nq),
        in_specs=[qblk, aqblk, kblk, akblk, vblk],
        out_specs=pl.BlockSpec((1, FOX_BQ, w), lambda b, g, i: (b, i, g)),
        compiler_params=_cparams(("parallel", "parallel", "arbitrary")),
        name="fox_attn",
    )(qt.reshape(batch, nq, MAIN_W, FOX_BQ), aqt, k.reshape(batch, seq, MAIN_W), ak.reshape(batch, seq, LANES), vt)
    return out.reshape(batch * seq, MAIN_W)


def _pad_heads_cols(w, heads, width, padded):
    lead = w.shape[:-1]
    w = w.reshape(lead + (heads, width))
    w = jnp.pad(w, [(0, 0)] * len(lead) + [(0, 0), (0, padded - width)])
    return w.reshape(lead + (heads * padded,))


def _gla_weights(w_in, w_gate_up, b_gate, g_head):
    dk = GLA_HEADS * GLA_DK_HEAD
    dv = GLA_HEADS * GLA_DV_HEAD
    o1, o2, o3, o4, o5 = dk, 2 * dk, 2 * dk + dv, 2 * dk + 2 * dv, 2 * dk + 2 * dv + GLA_RANK
    w = jnp.concatenate([
        w_in[..., :o4],
        w_in[..., o5:],
        jnp.pad(w_in[..., o4:o5], [(0, 0), (0, 0), (0, LANES - GLA_RANK)]),
    ], axis=-1).astype(BF16)
    wup = jnp.pad(_pad_heads_cols(w_gate_up, GLA_HEADS, GLA_DK_HEAD, DKP),
                  [(0, 0), (0, LANES - GLA_RANK), (0, 0)]).astype(BF16)
    bg = _pad_heads_cols(b_gate[:, None, :], GLA_HEADS, GLA_DK_HEAD, DKP)
    gh = jnp.pad(g_head[:, None, :], [(0, 0), (0, 0), (0, DVP - GLA_DV_HEAD)])
    return w, wup, bg, gh


def kernel(x, mem, norm_ffn, w_ffn_in, w_ffn_out, norm_mix, norm_mem, w_mem_kv, w_out, w_in_a, w_gate_up, b_gate,
           norm_gla_head, w_in_b, norm_shared, w_kv_shared, w_fgate, b_fgate, norm_final):
    batch, seq, _ = x.shape
    h = x.reshape(batch * seq, D_MODEL)

    w1 = w_ffn_in.astype(BF16)
    w2 = w_ffn_out.astype(BF16)
    g_ffn = norm_ffn[:, :, None, :]
    g_mix = norm_mix[:, None, :]
    wkt = jnp.swapaxes(w_mem_kv[:, :, :MEM_W], 1, 2).astype(BF16)
    wv = w_mem_kv[:, :, MEM_W:].astype(BF16)
    kb, vb = _mem_kv(mem, norm_mem[:, None, :], wkt, wv)
    w_a, wup, bg, gh = _gla_weights(w_in_a, w_gate_up, b_gate, norm_gla_head)
    wqt_b = jnp.swapaxes(w_in_b[:, :, :MAIN_W], 1, 2).astype(BF16)
    wm_b = w_in_b[:, :, MAIN_W:].astype(BF16)
    wo = w_out.astype(BF16)

    k_sh = v_sh = aq_sh = ak_sh = None
    for l in range(DEPTH):
        if l == N_A:
            w_sh = jnp.concatenate([w_kv_shared[:, :MAIN_W], jnp.pad(w_fgate, [(0, 0), (0, LANES - FOX_HEADS)])],
                                   axis=1).astype(BF16)
            wvt = w_kv_shared[:, MAIN_W:].T.astype(BF16)
            bf = jnp.pad(b_fgate[None, :], [(0, 0), (0, LANES - FOX_HEADS)])
            k_sh, v_sh, aq_sh, ak_sh = _shared_kv(h, norm_shared[None, :], w_sh, wvt, bf, batch, seq)
        h = _ffn(h, g_ffn, w1, w2, l, 0)
        if l < N_A:
            q, k, v, r, la, q_mem = _gla_proj(h, g_mix, w_a, wup, bg, l, l)
            h = _gla(q, k, la, v, r, gh, h, q_mem, kb, vb, wo, l, l, batch, seq)
        else:
            qt, q_mem = _fox_proj(h, g_mix, wqt_b, wm_b, l, l - N_A)
            o_main = _fox_attn(qt, aq_sh, k_sh, ak_sh, v_sh, batch, seq)
            h = _mix_out(h, o_main, q_mem, kb, vb, wo, l, seq)
        h = _ffn(h, g_ffn, w1, w2, l, 1, norm_final[None, :] if l == DEPTH - 1 else None)
    return h.reshape(batch, seq, D_MODEL)
```

```python
import numpy as np
import jax
import jax.numpy as jnp
from jax import lax
from jax.experimental import pallas as pl
from jax.experimental.pallas import tpu as pltpu

F32 = jnp.float32
BF16 = jnp.bfloat16

D_MODEL = 1024
DEPTH = 4
N_A = DEPTH // 2
MAIN_W = 768
MEM_W = 256
GLA_HEADS = 4
GLA_DK_HEAD = 96
GLA_DV_HEAD = 192
GLA_RANK = 16
GLA_TEMP = 16.0
FOX_HEADS = 6
FOX_DH = 128
MEM_HEADS = 4
MEM_DH = 64
MEM_LEN = 256
D_FF = 2816
EPS = 1e-6
LOG2E = 1.4426950408889634

LANES = 128
MXU_N = 256
BF16_ROWS = 16
VMEM_LIMIT = 56 * 1024 * 1024
FOX_VMEM_LIMIT = 60 * 1024 * 1024

DKP = LANES
DVP = MXU_N
GLA_QW = GLA_HEADS * DKP
GLA_CHUNK = MXU_N
GLA_LEVELS = 8
GLA_MM_LEVELS = 3

TM = 512
PROJ_TM = 1024
FOX_BQ = TM
FOX_BK = TM


def _cparams(sem, vmem_limit=VMEM_LIMIT):
    return pltpu.CompilerParams(dimension_semantics=sem, vmem_limit_bytes=vmem_limit)


def _rmsnorm(x, g):
    return x * lax.rsqrt(jnp.mean(x * x, axis=-1, keepdims=True) + EPS) * g


def _log_sigmoid(x):
    return jnp.minimum(x, 0.0) - jnp.log(1.0 + jnp.exp(-jnp.abs(x)))


def _sigmoid(x):
    return 1.0 / (1.0 + jnp.exp(-x))


def _split3(x):
    hi = x.astype(BF16)
    r1 = x - hi.astype(F32)
    mid = r1.astype(BF16)
    lo = (r1 - mid.astype(F32)).astype(BF16)
    return jnp.concatenate([hi, mid, lo], axis=1)


def _split2(x):
    hi = x.astype(BF16)
    lo = (x - hi.astype(F32)).astype(BF16)
    return jnp.concatenate([hi, lo], axis=1)


def _dot(a, b):
    return jnp.dot(a, b, preferred_element_type=F32)


def _dot_nt(a, b):
    return lax.dot_general(a, b, (((1,), (1,)), ((), ())), preferred_element_type=F32)


def _dot_tn(a, b):
    return lax.dot_general(a, b, (((0,), (0,)), ((), ())), preferred_element_type=F32)


def _exact_sum3(r, w):
    return r[:, :w] + r[:, w:2 * w] + r[:, 2 * w:]


def _memory_heads(qm, kb, vb):
    lg = _dot(qm, kb)
    ps = []
    for hh in range(MEM_HEADS):
        s = lg[:, hh * MEM_LEN:(hh + 1) * MEM_LEN]
        e = jnp.exp(s - jnp.max(s, axis=-1, keepdims=True))
        ps.append((e / jnp.sum(e, axis=-1, keepdims=True)).astype(BF16))
    return _dot(jnp.concatenate(ps, axis=1), vb).astype(BF16)


FFN_BOUNDS = (0, 6 * MXU_N, D_FF)
FFN_TM = 1024


def _ffn_kernel(h_ref, g_ref, w1_ref, w2_ref, *rest):
    o_ref = rest[-1]
    h = h_ref[...]
    xn = _rmsnorm(h, g_ref[0, 0]).astype(BF16)
    y = None
    for f0, f1 in zip(FFN_BOUNDS[:-1], FFN_BOUNDS[1:]):
        a = _dot(xn, w1_ref[0, 0, :, f0:f1])
        c = _dot(xn, w1_ref[0, 0, :, D_FF + f0:D_FF + f1])
        act = (a * _sigmoid(a) * c).astype(BF16)
        part = _dot(act, w2_ref[0, 0, f0:f1, :])
        y = part if y is None else y + part
    out = h + 0.5 * y
    o_ref[...] = _rmsnorm(out, rest[0][...]) if len(rest) == 2 else out


def _ffn(h, g_all, w1_all, w2_all, layer, half, g_final=None):
    n = h.shape[0]
    pick = lambda i: (layer, half, 0, 0)
    in_specs = [
        pl.BlockSpec((FFN_TM, D_MODEL), lambda i: (i, 0)),
        pl.BlockSpec((1, 1, 1, D_MODEL), pick),
        pl.BlockSpec((1, 1, D_MODEL, 2 * D_FF), pick, pipeline_mode=pl.Buffered(1)),
        pl.BlockSpec((1, 1, D_FF, D_MODEL), pick, pipeline_mode=pl.Buffered(1)),
    ]
    args = [h, g_all, w1_all, w2_all]
    if g_final is not None:
        in_specs.append(pl.BlockSpec((1, D_MODEL), lambda i: (0, 0)))
        args.append(g_final)
    return pl.pallas_call(
        _ffn_kernel,
        out_shape=jax.ShapeDtypeStruct((n, D_MODEL), F32),
        grid=(n // FFN_TM,),
        in_specs=in_specs,
        out_specs=pl.BlockSpec((FFN_TM, D_MODEL), lambda i: (i, 0)),
        compiler_params=_cparams(("parallel",)),
        name="ffn",
    )(*args)


GLA_QIN = GLA_HEADS * GLA_DK_HEAD
GLA_VIN = GLA_HEADS * GLA_DV_HEAD
GLA_PW = 2 * GLA_QIN + 2 * GLA_VIN + MEM_W + LANES


def _gla_proj_kernel(h_ref, g_ref, w_ref, wup_ref, bg_ref, q_ref, k_ref, v_ref, r_ref, la_ref, qm_ref):
    xn = _rmsnorm(h_ref[...], g_ref[...]).astype(BF16)
    oq, ov = 0, 2 * GLA_QIN
    orr, om, og = ov + GLA_VIN, ov + 2 * GLA_VIN, ov + 2 * GLA_VIN + MEM_W
    g_low = _dot(xn, w_ref[:, og:og + LANES]).astype(BF16)
    qk = _dot(xn, w_ref[:, oq:ov])
    x = _dot(g_low, wup_ref[...]) + bg_ref[...]
    q_ref[...] = (qk[:, :GLA_QIN] * (GLA_DK_HEAD ** -0.5)).astype(BF16)
    k_ref[...] = qk[:, GLA_QIN:].astype(BF16)
    la_ref[...] = _log_sigmoid(x) * (LOG2E / GLA_TEMP)
    v_ref[...] = _dot(xn, w_ref[:, ov:ov + GLA_VIN]).astype(BF16)
    r_ref[...] = _dot(xn, w_ref[:, orr:orr + GLA_VIN]).astype(BF16)
    qm_ref[...] = _dot(xn, w_ref[:, om:om + MEM_W]).astype(BF16)


def _gla_proj(h, g, w, wup, bg, layer, a):
    n = h.shape[0]
    row = lambda i: (i, 0)
    return pl.pallas_call(
        _gla_proj_kernel,
        out_shape=(
            jax.ShapeDtypeStruct((n, GLA_QIN), BF16),
            jax.ShapeDtypeStruct((n, GLA_QIN), BF16),
            jax.ShapeDtypeStruct((n, GLA_VIN), BF16),
            jax.ShapeDtypeStruct((n, GLA_VIN), BF16),
            jax.ShapeDtypeStruct((n, GLA_QW), F32),
            jax.ShapeDtypeStruct((n, MEM_W), BF16),
        ),
        grid=(n // PROJ_TM,),
        in_specs=[
            pl.BlockSpec((PROJ_TM, D_MODEL), row),
            pl.BlockSpec((None, 1, D_MODEL), lambda i: (layer, 0, 0)),
            pl.BlockSpec((None, D_MODEL, GLA_PW), lambda i: (a, 0, 0), pipeline_mode=pl.Buffered(1)),
            pl.BlockSpec((None, LANES, GLA_QW), lambda i: (a, 0, 0)),
            pl.BlockSpec((None, 1, GLA_QW), lambda i: (a, 0, 0)),
        ],
        out_specs=(
            pl.BlockSpec((PROJ_TM, GLA_QIN), row),
            pl.BlockSpec((PROJ_TM, GLA_QIN), row),
            pl.BlockSpec((PROJ_TM, GLA_VIN), row),
            pl.BlockSpec((PROJ_TM, GLA_VIN), row),
            pl.BlockSpec((PROJ_TM, GLA_QW), row),
            pl.BlockSpec((PROJ_TM, MEM_W), row),
        ),
        compiler_params=_cparams(("parallel",)),
        name="gla_proj",
    )(h, g, w, wup, bg)


GLA_TS = 1024
GLA_UNROLL = 2


def _gla_consts():
    c = GLA_CHUNK
    i = np.arange(c)[:, None]
    j = np.arange(c)[None, :]
    tril = (j <= i)
    b_rows, masks = [], []
    for lv in range(GLA_LEVELS):
        hs = 1 << lv
        mid = (i // (2 * hs)) * (2 * hs) + hs
        lower = (i % (2 * hs)) >= hs
        b_rows.append(np.where(lower, (j >= mid) & (j <= i), (j > i) & (j < mid)))
        masks.append(((i // (2 * hs)) == (j // (2 * hs))) & lower & ((j % (2 * hs)) < hs))
    masks.append(i == j)
    asb = lambda x: jnp.asarray(x.astype(np.float32), dtype=BF16)
    return asb(tril), asb(np.concatenate(b_rows[:GLA_MM_LEVELS], axis=0)), asb(np.stack(masks, axis=0))


def _gla_kernel(q_ref, k_ref, la_ref, v_ref, r_ref, gh_ref, tril_ref, b_ref, m_ref,
                h_ref, qm_ref, kb_ref, vb_ref, wo_ref, o_ref, st_ref):
    c = GLA_CHUNK

    @pl.when(pl.program_id(1) == 0)
    def _():
        st_ref[...] = jnp.zeros_like(st_ref)

    gh = gh_ref[...]

    heads = [(slice(h * DKP, (h + 1) * DKP), slice(h * DVP, (h + 1) * DVP)) for h in range(GLA_HEADS)]

    def head_cols(x, h, width=GLA_DV_HEAD, padded=DVP):
        piece = x[:, h * width:(h + 1) * width]
        return jnp.concatenate([piece, jnp.zeros((c, padded - width), x.dtype)], axis=1)

    def decays(r0):
        g = la_ref[pl.ds(r0, c), :]
        cum = _dot(tril_ref[...], _split2(g))
        cum = cum[:, :GLA_QW] + cum[:, GLA_QW:]
        small = _dot(b_ref[...], g.astype(BF16))
        lev = [small[lv * c:(lv + 1) * c] for lv in range(GLA_MM_LEVELS)]
        for lv in range(GLA_MM_LEVELS, GLA_LEVELS):
            hs = 1 << lv
            parts = []
            for mid in range(hs, c, 2 * hs):
                parts += [cum[mid - 1:mid, :] - cum[mid - hs:mid, :], cum[mid:mid + hs, :] - cum[mid - 1:mid, :]]
            lev.append(jnp.concatenate(parts, axis=0))
        return cum, lev

    def intra(r0, cum, lev):
        qa = q_ref[pl.ds(r0, c), :]
        ka = k_ref[pl.ds(r0, c), :]
        v = v_ref[pl.ds(r0, c), :]
        per_head = []
        for h, (ks, vs) in enumerate(heads):
            qb, kb, vh = head_cols(qa, h, GLA_DK_HEAD, DKP), head_cols(ka, h, GLA_DK_HEAD, DKP), head_cols(v, h)
            cum_h = cum[:, ks]
            last_h = cum_h[c - 1:c, :]
            s = _dot_nt(qb, kb).astype(BF16) * m_ref[GLA_LEVELS]
            for lv in range(GLA_LEVELS):
                e = jnp.exp2(lev[lv][:, ks]).astype(BF16)
                s = s + _dot_nt(qb * e, kb * e).astype(BF16) * m_ref[lv]
            kd = (kb.astype(F32) * jnp.exp2(last_h - cum_h)).astype(BF16)
            qg = (qb.astype(F32) * jnp.exp2(cum_h)).astype(BF16)
            per_head.append((qg, s, _dot_tn(vh, kd), jnp.exp2(last_h)))
        return per_head

    def recur(r0, per_head):
        v = v_ref[pl.ds(r0, c), :]
        r = r_ref[pl.ds(r0, c), :]
        ys = []
        for h, (ks, vs) in enumerate(heads):
            qg, s, upd, last_decay = per_head[h]
            st = st_ref[h]
            o = _dot_nt(qg, st.astype(BF16)) + _dot(s, head_cols(v, h))
            st_ref[h] = st * last_decay + upd
            ms = jnp.sum(o * o, axis=-1, keepdims=True) * (1.0 / GLA_DV_HEAD)
            rh = head_cols(r, h).astype(F32)
            y = o * lax.rsqrt(ms + EPS) * gh * (rh * _sigmoid(rh))
            ys.append(y[:, :GLA_DV_HEAD].astype(BF16))
        return jnp.concatenate(ys, axis=1)

    def group(gi, carry):
        offs = [pl.multiple_of((gi * GLA_UNROLL + u) * c, c) for u in range(GLA_UNROLL)]
        dec = [decays(r0) for r0 in offs]
        cur = [intra(r0, *d) for r0, d in zip(offs, dec)]
        mem = [_memory_heads(qm_ref[pl.ds(r0, c), :], kb_ref[0, 0], vb_ref[0, 0]) for r0 in offs]
        main = [recur(r0, per_head) for r0, per_head in zip(offs, cur)]
        for r0, o_main, o_mem in zip(offs, main, mem):
            mixed = _dot(o_main, wo_ref[0:MAIN_W, :]) + _dot(o_mem, wo_ref[MAIN_W:MAIN_W + MEM_W, :])
            o_ref[pl.ds(r0, c), :] = h_ref[pl.ds(r0, c), :] + mixed
        return carry

    lax.fori_loop(0, GLA_TS // (c * GLA_UNROLL), group, 0)


def _gla(q, k, la, v, r, gh, h, q_mem, kb, vb, wo, layer, a, batch, seq):
    n = q.shape[0]
    nt = seq // GLA_TS
    tril, bmat, m = _gla_consts()
    row = lambda b, t: (b * nt + t, 0)
    fixed2 = lambda b, t: (0, 0)
    fixed3 = lambda b, t: (0, 0, 0)
    mem = lambda b, t: (layer, b, 0, 0)
    return pl.pallas_call(
        _gla_kernel,
        out_shape=jax.ShapeDtypeStruct((n, D_MODEL), F32),
        grid=(batch, nt),
        in_specs=[
            pl.BlockSpec((GLA_TS, GLA_QIN), row),
            pl.BlockSpec((GLA_TS, GLA_QIN), row),
            pl.BlockSpec((GLA_TS, GLA_QW), row),
            pl.BlockSpec((GLA_TS, GLA_VIN), row),
            pl.BlockSpec((GLA_TS, GLA_VIN), row),
            pl.BlockSpec((None, 1, DVP), lambda b, t: (a, 0, 0)),
            pl.BlockSpec(tril.shape, fixed2),
            pl.BlockSpec(bmat.shape, fixed2),
            pl.BlockSpec(m.shape, fixed3),
            pl.BlockSpec((GLA_TS, D_MODEL), row),
            pl.BlockSpec((GLA_TS, MEM_W), row),
            pl.BlockSpec((1, 1, MEM_W, MEM_BW), mem),
            pl.BlockSpec((1, 1, MEM_BW, MEM_W), mem),
            pl.BlockSpec((None, MAIN_W + MEM_W, D_MODEL), lambda b, t: (layer, 0, 0), pipeline_mode=pl.Buffered(1)),
        ],
        out_specs=pl.BlockSpec((GLA_TS, D_MODEL), row),
        scratch_shapes=[pltpu.VMEM((GLA_HEADS, DVP, DKP), F32)],
        compiler_params=_cparams(("arbitrary", "arbitrary")),
        name="gla",
    )(q, k, la, v, r, gh, tril, bmat, m, h, q_mem, kb, vb, wo)


MEM_BW = MEM_HEADS * MEM_LEN


def _mem_kv_kernel(mem_ref, g_ref, wkt_ref, wv_ref, kb_ref, vb_ref):
    d_row = lax.broadcasted_iota(jnp.int32, (MEM_W, MEM_LEN), 0) // MEM_DH
    d_col = lax.broadcasted_iota(jnp.int32, (MEM_LEN, MEM_W), 1) // MEM_DH
    for b in range(mem_ref.shape[0]):
        mn = _rmsnorm(mem_ref[b], g_ref[0]).astype(BF16)
        kt = _dot_nt(wkt_ref[0], mn) * (MEM_DH ** -0.5)
        v = _dot(mn, wv_ref[0])
        for hh in range(MEM_HEADS):
            kb_ref[0, b, :, hh * MEM_LEN:(hh + 1) * MEM_LEN] = jnp.where(d_row == hh, kt, 0.0).astype(BF16)
            vb_ref[0, b, hh * MEM_LEN:(hh + 1) * MEM_LEN, :] = jnp.where(d_col == hh, v, 0.0).astype(BF16)


def _mem_kv(mem, g, wkt, wv):
    b = mem.shape[0]
    return pl.pallas_call(
        _mem_kv_kernel,
        out_shape=(
            jax.ShapeDtypeStruct((DEPTH, b, MEM_W, MEM_BW), BF16),
            jax.ShapeDtypeStruct((DEPTH, b, MEM_BW, MEM_W), BF16),
        ),
        grid=(DEPTH,),
        in_specs=[
            pl.BlockSpec((b, MEM_LEN, D_MODEL), lambda l: (0, 0, 0)),
            pl.BlockSpec((1, 1, D_MODEL), lambda l: (l, 0, 0)),
            pl.BlockSpec((1, MEM_W, D_MODEL), lambda l: (l, 0, 0)),
            pl.BlockSpec((1, D_MODEL, MEM_W), lambda l: (l, 0, 0)),
        ],
        out_specs=(
            pl.BlockSpec((1, b, MEM_W, MEM_BW), lambda l: (l, 0, 0, 0)),
            pl.BlockSpec((1, b, MEM_BW, MEM_W), lambda l: (l, 0, 0, 0)),
        ),
        compiler_params=_cparams(("parallel",)),
        name="mem_kv",
    )(mem, g, wkt, wv)


SH_PW = MAIN_W + LANES


ONE_LANE = LANES - 1
AUX_SLOT = BF16_ROWS


def _fox_aux_consts():
    pqt = np.zeros((LANES, 3 * LANES), np.float32)
    pk = np.zeros((3 * LANES, LANES), np.float32)
    for h in range(FOX_HEADS):
        for p in range(3):
            pqt[h * AUX_SLOT + p, p * LANES + h] = 1.0
            pqt[h * AUX_SLOT + 3 + p, ONE_LANE] = 1.0
            pk[ONE_LANE, h * AUX_SLOT + p] = 1.0
            pk[p * LANES + h, h * AUX_SLOT + 3 + p] = -1.0
    tril = np.tril(np.ones((TM, TM), np.float32))
    return jnp.asarray(tril, dtype=BF16), jnp.asarray(pqt, dtype=BF16), jnp.asarray(pk, dtype=BF16)


def _shared_kernel(h_ref, g_ref, w_ref, wvt_ref, bf_ref, tril_ref, pqt_ref, pk_ref,
                   k_ref, vt_ref, aqt_ref, ak_ref, carry_ref):
    @pl.when(pl.program_id(1) == 0)
    def _():
        carry_ref[...] = jnp.zeros_like(carry_ref)

    xn = _rmsnorm(h_ref[...], g_ref[...]).astype(BF16)
    log_f = _log_sigmoid(_dot(xn, w_ref[:, MAIN_W:SH_PW]) + bf_ref[...])
    k_ref[...] = _dot(xn, w_ref[:, 0:MAIN_W]).astype(BF16)
    d = _exact_sum3(_dot(tril_ref[...], _split3(log_f)), LANES) + carry_ref[...]
    carry_ref[...] = d[TM - 1:TM, :]
    d3 = _split3(d * LOG2E)
    lane = lax.broadcasted_iota(jnp.int32, d3.shape, 1)
    d3 = jnp.where(lane == ONE_LANE, jnp.ones_like(d3), d3)
    vt_ref[0, 0] = _dot_nt(wvt_ref[...], xn).astype(BF16)
    aqt_ref[0, 0] = _dot_nt(pqt_ref[...], d3).astype(BF16)
    ak_ref[...] = _dot(d3, pk_ref[...]).astype(BF16)


def _shared_kv(h, g, w, wvt, bf, batch, seq):
    n = h.shape[0]
    nt = seq // TM
    tril, pqt, pk = _fox_aux_consts()
    row = lambda b, t: (b * nt + t, 0)
    fixed = lambda b, t: (0, 0)
    slab = lambda b, t: (b, t, 0, 0)
    return pl.pallas_call(
        _shared_kernel,
        out_shape=(
            jax.ShapeDtypeStruct((n, MAIN_W), BF16),
            jax.ShapeDtypeStruct((batch, nt, MAIN_W, TM), BF16),
            jax.ShapeDtypeStruct((batch, nt, LANES, TM), BF16),
            jax.ShapeDtypeStruct((n, LANES), BF16),
        ),
        grid=(batch, nt),
        in_specs=[
            pl.BlockSpec((TM, D_MODEL), row),
            pl.BlockSpec((1, D_MODEL), fixed),
            pl.BlockSpec((D_MODEL, SH_PW), fixed),
            pl.BlockSpec((MAIN_W, D_MODEL), fixed),
            pl.BlockSpec((1, LANES), fixed),
            pl.BlockSpec((TM, TM), fixed),
            pl.BlockSpec((LANES, 3 * LANES), fixed),
            pl.BlockSpec((3 * LANES, LANES), fixed),
        ],
        out_specs=(
            pl.BlockSpec((TM, MAIN_W), row),
            pl.BlockSpec((1, 1, MAIN_W, TM), slab),
            pl.BlockSpec((1, 1, LANES, TM), slab),
            pl.BlockSpec((TM, LANES), row),
        ),
        scratch_shapes=[pltpu.VMEM((1, LANES), F32)],
        compiler_params=_cparams(("arbitrary", "arbitrary")),
        name="shared_kv",
    )(h, g, w, wvt, bf, tril, pqt, pk)


def _fox_proj_kernel(h_ref, g_ref, wqt_ref, wm_ref, qt_ref, qm_ref):
    xn = _rmsnorm(h_ref[...], g_ref[...]).astype(BF16)
    qt = (_dot_nt(wqt_ref[...], xn) * (FOX_DH ** -0.5 * LOG2E)).astype(BF16)
    for s in range(PROJ_TM // TM):
        qt_ref[s] = qt[:, s * TM:(s + 1) * TM]
    qm_ref[...] = _dot(xn, wm_ref[...]).astype(BF16)


def _fox_proj(h, g, wqt, wm, layer, j):
    n = h.shape[0]
    row = lambda i: (i, 0)
    return pl.pallas_call(
        _fox_proj_kernel,
        out_shape=(jax.ShapeDtypeStruct((n // TM, MAIN_W, TM), BF16), jax.ShapeDtypeStruct((n, MEM_W), BF16)),
        grid=(n // PROJ_TM,),
        in_specs=[
            pl.BlockSpec((PROJ_TM, D_MODEL), row),
            pl.BlockSpec((None, 1, D_MODEL), lambda i: (layer, 0, 0)),
            pl.BlockSpec((None, MAIN_W, D_MODEL), lambda i: (j, 0, 0)),
            pl.BlockSpec((None, D_MODEL, MEM_W), lambda i: (j, 0, 0)),
        ],
        out_specs=(pl.BlockSpec((PROJ_TM // TM, MAIN_W, TM), lambda i: (i, 0, 0)),
                   pl.BlockSpec((PROJ_TM, MEM_W), row)),
        compiler_params=_cparams(("parallel",)),
        name="fox_proj",
    )(h, g, wqt, wm)


FOX_HPS = 6


FOX_SUB = MXU_N
FOX_NSUB = FOX_BK // FOX_SUB


def _fox_attn_kernel(qt_ref, aqt_ref, k_ref, ak_ref, vt_ref, h_ref, qm_ref, kb_ref, vb_ref, wo_ref, o_ref):
    qi = pl.program_id(2)
    o_mem = _memory_heads(qm_ref[0], kb_ref[0, 0], vb_ref[0, 0])
    chains =[(hh, qs) for hh in range(FOX_HPS) for qs in range(FOX_BQ // FOX_SUB)]

    def head(hh):
        return slice(hh * FOX_DH, (hh + 1) * FOX_DH)

    def strip(s):
        return slice(s * FOX_SUB, (s + 1) * FOX_SUB)

    slot = lax.broadcasted_iota(jnp.int32, (LANES, FOX_SUB), 0) // AUX_SLOT
    first_head = pl.program_id(1) * FOX_HPS
    qq = [jnp.concatenate([qt_ref[0, 0, head(hh), strip(qs)],
                           jnp.where(slot == first_head + hh, aqt_ref[0, 0, :, strip(qs)], jnp.zeros((), BF16))],
                          axis=0)
          for hh, qs in chains]

    def scores(kj, diagonal):
        c0 = pl.multiple_of(kj * FOX_BK, FOX_BK)
        sts = {}
        for ks in range(FOX_NSUB):
            rows = pl.ds(c0 + ks * FOX_SUB, FOX_SUB)
            ak = ak_ref[0, rows, :]
            kk = [jnp.concatenate([k_ref[0, rows, head(hh)], ak], axis=1) for hh in range(FOX_HPS)]
            for ci, (hh, qs) in enumerate(chains):
                if diagonal and ks > qs:
                    continue
                st = _dot(kk[hh], qq[ci])
                if diagonal and ks == qs:
                    key = lax.broadcasted_iota(jnp.int32, st.shape, 0)
                    qry = lax.broadcasted_iota(jnp.int32, st.shape, 1)
                    st = jnp.where(key <= qry, st, -jnp.inf)
                sts[ks, ci] = st
        return sts

    def update(kj, carry, sts):
        carry = list(carry)
        for ks in range(FOX_NSUB):
            vt1 = [jnp.concatenate([vt_ref[0, kj, head(hh), strip(ks)], ones], axis=0) for hh in range(FOX_HPS)]
            for ci, (hh, qs) in enumerate(chains):
                if (ks, ci) not in sts:
                    continue
                m, acc = carry[ci]
                st = sts[ks, ci]
                m_new = jnp.maximum(m, jnp.max(st, axis=0, keepdims=True))
                alpha = jnp.exp2(m - m_new)
                p = jnp.exp2(st - m_new).astype(BF16)
                acc = alpha * acc + _dot(vt1[hh], p)
                carry[ci] = (m_new, acc)
        return tuple(carry)

    ones = jnp.ones((BF16_ROWS, FOX_SUB), BF16)
    init = tuple((jnp.full((1, FOX_SUB), -jnp.inf, F32), jnp.zeros((FOX_DH + BF16_ROWS, FOX_SUB), F32))
                 for _ in chains)

    def pair(i, carry):
        kj = 2 * i
        sts0, sts1 = scores(kj, False), scores(kj + 1, False)
        return update(kj + 1, update(kj, carry, sts0), sts1)

    def tail_with_leftover(carry):
        sts0, sts1 = scores(qi - 1, False), scores(qi, True)
        return update(qi, update(qi - 1, carry, sts0), sts1)

    carry = lax.fori_loop(0, qi // 2, pair, init)
    carry = lax.cond(qi % 2 == 1, tail_with_leftover, lambda cr: update(qi, cr, scores(qi, True)), carry)
    rows = []
    for qs in range(FOX_BQ // FOX_SUB):
        per_head = [carry[ci][1] for ci, (hh, s) in enumerate(chains) if s == qs]
        rows.append(jnp.concatenate([(acc[:FOX_DH] / acc[FOX_DH:FOX_DH + 1]).T.astype(BF16) for acc in per_head], axis=1))
    o_main = jnp.concatenate(rows, axis=0)
    mixed = _dot(o_main, wo_ref[0:MAIN_W, :]) + _dot(o_mem, wo_ref[MAIN_W:MAIN_W + MEM_W, :])
    o_ref[0] = h_ref[0] + mixed


def _fox_attn(qt, aqt, k, ak, vt, h, q_mem, kb, vb, wo, layer, batch, seq):
    assert FOX_HPS == FOX_HEADS
    nq = seq // FOX_BQ
    qblk = pl.BlockSpec((1, 1, MAIN_W, FOX_BQ), lambda b, g, i: (b, i, 0, 0))
    kblk = pl.BlockSpec((1, seq, MAIN_W), lambda b, g, i: (b, 0, 0))
    vblk = pl.BlockSpec((1, seq // FOX_BK, MAIN_W, FOX_BK), lambda b, g, i: (b, 0, 0, 0))
    aqblk = pl.BlockSpec((1, 1, LANES, FOX_BQ), lambda b, g, i: (b, i, 0, 0))
    akblk = pl.BlockSpec((1, seq, LANES), lambda b, g, i: (b, 0, 0))
    tile = lambda width: pl.BlockSpec((1, FOX_BQ, width), lambda b, g, i: (b, i, 0))
    mem = lambda b, g, i: (layer, b, 0, 0)
    out = pl.pallas_call(
        _fox_attn_kernel,
        out_shape=jax.ShapeDtypeStruct((batch, seq, D_MODEL), F32),
        grid=(batch, 1, nq),
        in_specs=[qblk, aqblk, kblk, akblk, vblk, tile(D_MODEL), tile(MEM_W),
                  pl.BlockSpec((1, 1, MEM_W, MEM_BW), mem), pl.BlockSpec((1, 1, MEM_BW, MEM_W), mem),
                  pl.BlockSpec((None, MAIN_W + MEM_W, D_MODEL), lambda b, g, i: (layer, 0, 0),
                               pipeline_mode=pl.Buffered(1))],
        out_specs=tile(D_MODEL),
        compiler_params=_cparams(("parallel", "parallel", "arbitrary"), FOX_VMEM_LIMIT),
        name="fox_attn",
    )(qt.reshape(batch, nq, MAIN_W, FOX_BQ), aqt, k.reshape(batch, seq, MAIN_W), ak.reshape(batch, seq, LANES), vt,
      h.reshape(batch, seq, D_MODEL), q_mem.reshape(batch, seq, MEM_W), kb, vb, wo)
    return out.reshape(batch * seq, D_MODEL)


def _pad_heads_cols(w, heads, width, padded):
    lead = w.shape[:-1]
    w = w.reshape(lead + (heads, width))
    w = jnp.pad(w, [(0, 0)] * len(lead) + [(0, 0), (0, padded - width)])
    return w.reshape(lead + (heads * padded,))


def _gla_weights(w_in, w_gate_up, b_gate, g_head):
    dk = GLA_HEADS * GLA_DK_HEAD
    dv = GLA_HEADS * GLA_DV_HEAD
    o1, o2, o3, o4, o5 = dk, 2 * dk, 2 * dk + dv, 2 * dk + 2 * dv, 2 * dk + 2 * dv + GLA_RANK
    w = jnp.concatenate([
        w_in[..., :o4],
        w_in[..., o5:],
        jnp.pad(w_in[..., o4:o5], [(0, 0), (0, 0), (0, LANES - GLA_RANK)]),
    ], axis=-1).astype(BF16)
    wup = jnp.pad(_pad_heads_cols(w_gate_up, GLA_HEADS, GLA_DK_HEAD, DKP),
                  [(0, 0), (0, LANES - GLA_RANK), (0, 0)]).astype(BF16)
    bg = _pad_heads_cols(b_gate[:, None, :], GLA_HEADS, GLA_DK_HEAD, DKP)
    gh = jnp.pad(g_head[:, None, :], [(0, 0), (0, 0), (0, DVP - GLA_DV_HEAD)])
    return w, wup, bg, gh


def kernel(x, mem, norm_ffn, w_ffn_in, w_ffn_out, norm_mix, norm_mem, w_mem_kv, w_out, w_in_a, w_gate_up, b_gate,
           norm_gla_head, w_in_b, norm_shared, w_kv_shared, w_fgate, b_fgate, norm_final):
    batch, seq, _ = x.shape
    h = x.reshape(batch * seq, D_MODEL)

    w1 = w_ffn_in.astype(BF16)
    w2 = w_ffn_out.astype(BF16)
    g_ffn = norm_ffn[:, :, None, :]
    g_mix = norm_mix[:, None, :]
    wkt = jnp.swapaxes(w_mem_kv[:, :, :MEM_W], 1, 2).astype(BF16)
    wv = w_mem_kv[:, :, MEM_W:].astype(BF16)
    kb, vb = _mem_kv(mem, norm_mem[:, None, :], wkt, wv)
    w_a, wup, bg, gh = _gla_weights(w_in_a, w_gate_up, b_gate, norm_gla_head)
    wqt_b = jnp.swapaxes(w_in_b[:, :, :MAIN_W], 1, 2).astype(BF16)
    wm_b = w_in_b[:, :, MAIN_W:].astype(BF16)
    wo = w_out.astype(BF16)

    k_sh = v_sh = aq_sh = ak_sh = None
    for l in range(DEPTH):
        if l == N_A:
            w_sh = jnp.concatenate([w_kv_shared[:, :MAIN_W], jnp.pad(w_fgate, [(0, 0), (0, LANES - FOX_HEADS)])],
                                   axis=1).astype(BF16)
            wvt = w_kv_shared[:, MAIN_W:].T.astype(BF16)
            bf = jnp.pad(b_fgate[None, :], [(0, 0), (0, LANES - FOX_HEADS)])
            k_sh, v_sh, aq_sh, ak_sh = _shared_kv(h, norm_shared[None, :], w_sh, wvt, bf, batch, seq)
        h = _ffn(h, g_ffn, w1, w2, l, 0)
        if l < N_A:
            q, k, v, r, la, q_mem = _gla_proj(h, g_mix, w_a, wup, bg, l, l)
            h = _gla(q, k, la, v, r, gh, h, q_mem, kb, vb, wo, l, l, batch, seq)
        else:
            qt, q_mem = _fox_proj(h, g_mix, wqt_b, wm_b, l, l - N_A)
            h = _fox_attn(qt, aq_sh, k_sh, ak_sh, v_sh, h, q_mem, kb, vb, wo, l, batch, seq)
        h = _ffn(h, g_ffn, w1, w2, l, 1, norm_final[None, :] if l == DEPTH - 1 else None)
    return h.reshape(batch, seq, D_MODEL)
```
